```python
import jax, jax.numpy as jnp
from jax import lax
import numpy as np

D_MODEL = 1024
BATCH = 8
SEQ = 2048
DEPTH = 1

HEAD_DIM = 64
N_HEADS_A = 8
N_KV_A = 2
N_HEADS_B = 8
N_KV_B = 2
WIDTH_A = N_HEADS_A * HEAD_DIM
WIDTH_B = N_HEADS_B * HEAD_DIM
MIX_WIDTH = WIDTH_A + WIDTH_B
KV_A = N_KV_A * HEAD_DIM
KV_B = N_KV_B * HEAD_DIM
SPLIT_SIZES = (WIDTH_A, KV_A, KV_A, WIDTH_A, WIDTH_B, KV_B, KV_B, WIDTH_B)
IN_COLS = sum(SPLIT_SIZES)
Q_BLOCK = 128
WINDOW = 128
GRID_W = 64
ROPE_THETA = 10000.0
N_BUCKETS = 32
MAX_DISTANCE = 128
EPS = 1e-6
MASK_VALUE = -1e30

kernel_name = "hymba_style_bidir_hybrid_attn_layer"


def rms_norm(x, g):
    xf = x.astype(jnp.float32)
    y = xf * lax.rsqrt(jnp.mean(xf * xf, axis=-1, keepdims=True) + EPS)
    return (y * g.astype(jnp.float32)).astype(x.dtype)


def _rotate(xs, ang):
    ang2 = jnp.concatenate([ang, ang], axis=-1)[None, :, None, :]
    cos, sin = jnp.cos(ang2), jnp.sin(ang2)
    x1, x2 = jnp.split(xs, 2, axis=-1)
    rot = jnp.concatenate([-x2, x1], axis=-1)
    return xs * cos + rot * sin


def axial_rope(x, row, col):
    half = x.shape[-1] // 2
    freqs = ROPE_THETA ** (-jnp.arange(0, half, 2, dtype=jnp.float32) / half)
    xf = x.astype(jnp.float32)
    xr = _rotate(xf[..., :half], row[:, None] * freqs[None, :])
    xc = _rotate(xf[..., half:], col[:, None] * freqs[None, :])
    return jnp.concatenate([xr, xc], axis=-1).astype(x.dtype)


def t5_bucket(rel):
    nb = N_BUCKETS // 2
    max_exact = nb // 2
    ret = (rel > 0).astype(jnp.int32) * nb
    n = jnp.abs(rel)
    nf = jnp.maximum(n, max_exact).astype(jnp.float32)
    large = max_exact + (jnp.log(nf / max_exact) / np.log(MAX_DISTANCE / max_exact)
                         * (nb - max_exact)).astype(jnp.int32)
    large = jnp.minimum(large, nb - 1)
    return ret + jnp.where(n < max_exact, n, large)


def global_attention(q, k, v):
    B, S, H, D = q.shape
    Hkv = k.shape[2]
    G = H // Hkv
    nb = S // Q_BLOCK
    scale = D ** -0.5
    qb = q.reshape(B, nb, Q_BLOCK, Hkv, G, D).transpose(1, 0, 2, 3, 4, 5)

    def one_block(qblk):
        s = jnp.einsum('bqkgd,bskd->bkgqs', qblk, k,
                       preferred_element_type=jnp.float32) * scale
        p = jax.nn.softmax(s, axis=-1)
        return jnp.einsum('bkgqs,bskd->bqkgd', p.astype(v.dtype), v)

    o = lax.map(one_block, qb)
    return o.transpose(1, 0, 2, 3, 4, 5).reshape(B, S, H * D)


def window_attention(q, k, v, rel_table, sink):
    B, S, H, D = q.shape
    Hkv = k.shape[2]
    G = H // Hkv
    nb = S // Q_BLOCK
    scale = D ** -0.5
    qb = q.reshape(B, nb, Q_BLOCK, Hkv, G, D)
    pad = ((0, 0), (WINDOW, WINDOW), (0, 0), (0, 0))
    kp = jnp.pad(k, pad).reshape(B, nb + 2, Q_BLOCK, Hkv, D)
    vp = jnp.pad(v, pad).reshape(B, nb + 2, Q_BLOCK, Hkv, D)
    kb = jnp.concatenate([kp[:, :-2], kp[:, 1:-1], kp[:, 2:]], axis=2)
    vb = jnp.concatenate([vp[:, :-2], vp[:, 1:-1], vp[:, 2:]], axis=2)
    r = jnp.arange(Q_BLOCK)
    j = jnp.arange(3 * Q_BLOCK)
    rel = j[None, :] - Q_BLOCK - r[:, None]
    bias = rel_table[t5_bucket(rel)]
    bias = bias.transpose(2, 0, 1).reshape(Hkv, G, Q_BLOCK, 3 * Q_BLOCK).astype(jnp.float32)
    kpos = jnp.arange(nb)[:, None] * Q_BLOCK - WINDOW + j[None, :]
    valid = ((jnp.abs(rel) <= WINDOW)[None]
             & (kpos >= 0)[:, None, :] & (kpos < S)[:, None, :])
    s = jnp.einsum('bnqkgd,bnskd->bnkgqs', qb, kb,
                   preferred_element_type=jnp.float32) * scale + bias
    s = jnp.where(valid[None, :, None, None], s, MASK_VALUE)
    sk = sink.astype(jnp.float32).reshape(Hkv, G)[None, None, :, :, None, None]
    m = jnp.maximum(jnp.max(s, axis=-1, keepdims=True), sk)
    e = jnp.exp(s - m)
    p = e / (jnp.sum(e, axis=-1, keepdims=True) + jnp.exp(sk - m))
    o = jnp.einsum('bnkgqs,bnskd->bnqkgd', p.astype(v.dtype), vb)
    return o.reshape(B, S, H * D)


def setup_inputs(seed: int = 0) -> dict:
    key = jax.random.key(seed)
    ks = jax.random.split(key, 14)
    f32 = jnp.float32
    x = jax.random.normal(ks[0], (BATCH, SEQ, D_MODEL), f32)
    c = jax.random.normal(ks[1], (BATCH, D_MODEL), f32)
    w_ada = jax.random.normal(ks[2], (DEPTH, D_MODEL, 3 * D_MODEL), f32) * (0.3 * D_MODEL ** -0.5)
    b_ada = jax.random.normal(ks[3], (DEPTH, 3 * D_MODEL), f32) * 0.01
    g_pre = 1.0 + 0.05 * jax.random.normal(ks[4], (DEPTH, D_MODEL), f32)
    g_post = 1.0 + 0.05 * jax.random.normal(ks[5], (DEPTH, D_MODEL), f32)
    w_in = jax.random.normal(ks[6], (DEPTH, D_MODEL, IN_COLS), f32) * D_MODEL ** -0.5
    qn_a = 1.0 + 0.05 * jax.random.normal(ks[7], (DEPTH, HEAD_DIM), f32)
    kn_a = 1.0 + 0.05 * jax.random.normal(ks[8], (DEPTH, HEAD_DIM), f32)
    sink_b = jax.random.normal(ks[9], (DEPTH, N_HEADS_B), f32)
    w_out = jax.random.normal(ks[10], (DEPTH, MIX_WIDTH, D_MODEL), f32) * MIX_WIDTH ** -0.5
    rel_table = jax.random.normal(ks[11], (N_BUCKETS, N_HEADS_B), f32) * 0.5
    return {"x": x, "c": c, "w_ada": w_ada, "b_ada": b_ada, "g_pre": g_pre,
            "g_post": g_post, "w_in": w_in, "qn_a": qn_a, "kn_a": kn_a,
            "sink_b": sink_b, "w_out": w_out, "rel_table": rel_table}


def reference(x, c, w_ada, b_ada, g_pre, g_post, w_in, qn_a, kn_a, sink_b, w_out, rel_table):
    B, S, _ = x.shape
    rows = S // GRID_W
    row = jnp.repeat(jnp.arange(rows), GRID_W).astype(jnp.float32)
    col = jnp.tile(jnp.arange(GRID_W), rows).astype(jnp.float32)
    split_idx = list(np.cumsum(SPLIT_SIZES)[:-1])
    c_act = jax.nn.silu(c)
    for l in range(DEPTH):
        mod = c_act @ w_ada[l] + b_ada[l]
        shift, scale, gate = jnp.split(mod, 3, axis=-1)
        h = rms_norm(x, g_pre[l]) * (1.0 + scale[:, None, :]) + shift[:, None, :]
        proj = h @ w_in[l]
        qa, ka, va, ga, qb, kb, vb, gb = jnp.split(proj, split_idx, axis=-1)
        qa = axial_rope(rms_norm(qa.reshape(B, S, N_HEADS_A, HEAD_DIM), qn_a[l]), row, col)
        ka = axial_rope(rms_norm(ka.reshape(B, S, N_KV_A, HEAD_DIM), kn_a[l]), row, col)
        va = va.reshape(B, S, N_KV_A, HEAD_DIM)
        oa = global_attention(qa, ka, va) * jax.nn.silu(ga)
        qb = qb.reshape(B, S, N_HEADS_B, HEAD_DIM)
        kb = kb.reshape(B, S, N_KV_B, HEAD_DIM)
        vb = vb.reshape(B, S, N_KV_B, HEAD_DIM)
        ob = window_attention(qb, kb, vb, rel_table, sink_b[l]) * jax.nn.silu(gb)
        y = jnp.concatenate([oa, ob], axis=-1) @ w_out[l]
        x = x + gate[:, None, :] * rms_norm(y, g_post[l])
    return x
```

```python
import functools

import jax
import jax.numpy as jnp
import numpy as np
from jax import lax
from jax.experimental import pallas as pl
from jax.experimental.pallas import tpu as pltpu

D_MODEL = 1024
HEAD_DIM = 64
N_HEADS = 8
N_KV = 2
GROUP = N_HEADS // N_KV
WIDTH = N_HEADS * HEAD_DIM
KV_WIDTH = N_KV * HEAD_DIM
IN_COLS = 2 * (2 * WIDTH + 2 * KV_WIDTH)
Q_BLOCK = 128
WINDOW = 128
GRID_W = 64
ROPE_THETA = 10000.0
N_BUCKETS = 32
MAX_DISTANCE = 128
EPS = 1e-6
MASK_VALUE = -1e30
QK_SCALE = HEAD_DIM ** -0.5

_QA, _KA, _VA, _GA = 0, WIDTH, WIDTH + KV_WIDTH, WIDTH + 2 * KV_WIDTH
_HALF = 2 * WIDTH + 2 * KV_WIDTH
_QB, _KB, _VB, _GB = _HALF + _QA, _HALF + _KA, _HALF + _VA, _HALF + _GA

_F32 = jnp.float32
_BF16 = jnp.bfloat16

_VMEM_LIMIT_BYTES = 48 * 1024 * 1024

_TM_IN = 512
_TQ_A = 256
_TQ_B = 256
_KEYS_B = _TQ_B + 2 * WINDOW
_TM_OUT = 512
_TN_MOD = 512

_NT = (((1,), (1,)), ((), ()))


def _silu(v):
    return v * (1.0 / (1.0 + jnp.exp(-v)))


def _params(n_grid):
    return pltpu.CompilerParams(
        dimension_semantics=("arbitrary",) * n_grid,
        vmem_limit_bytes=_VMEM_LIMIT_BYTES)


def _mod_kernel(c_ref, w_ref, b_ref, o_ref):
    ca = _silu(c_ref[...])
    o_ref[...] = jnp.dot(ca.astype(_BF16), w_ref[...].astype(_BF16),
                         preferred_element_type=_F32) + b_ref[...]


def _adaln_mod(c, w, b):
    bsz, d = c.shape
    n = w.shape[1]
    return pl.pallas_call(
        _mod_kernel,
        out_shape=jax.ShapeDtypeStruct((bsz, n), _F32),
        grid=(n // _TN_MOD,),
        in_specs=[pl.BlockSpec((bsz, d), lambda j: (0, 0)),
                  pl.BlockSpec((d, _TN_MOD), lambda j: (0, j)),
                  pl.BlockSpec((1, _TN_MOD), lambda j: (0, j))],
        out_specs=pl.BlockSpec((bsz, _TN_MOD), lambda j: (0, j)),
        compiler_params=_params(1),
        name="adaln_mod",
    )(c, w, b.reshape(1, n))


def _bias_kernel(table_ref, bucket_ref, o_ref):
    masked = jnp.full((Q_BLOCK, Q_BLOCK), MASK_VALUE, _F32)
    for h in range(N_HEADS):
        tiles = []
        for d in range(3):
            bucket = bucket_ref[d]
            acc = masked
            for b in range(N_BUCKETS):
                acc = jnp.where(bucket == b, table_ref[b, h], acc)
            tiles.append(acc)
        lo, mid, hi = tiles
        for jb, (left, right) in enumerate(((lo, masked), (mid, lo), (hi, mid), (masked, hi))):
            rows = slice(jb * Q_BLOCK, (jb + 1) * Q_BLOCK)
            o_ref[h, rows, 0:Q_BLOCK] = left
            o_ref[h, rows, Q_BLOCK:2 * Q_BLOCK] = right


def _t5_bucket(rel):
    nb = N_BUCKETS // 2
    max_exact = nb // 2
    ret = (rel > 0).astype(jnp.int32) * nb
    n = jnp.abs(rel)
    nf = jnp.maximum(n, max_exact).astype(jnp.float32)
    large = max_exact + (jnp.log(nf / max_exact) / np.log(MAX_DISTANCE / max_exact)
                         * (nb - max_exact)).astype(jnp.int32)
    large = jnp.minimum(large, nb - 1)
    return ret + jnp.where(n < max_exact, n, large)


def _t5_bias_tiles(rel_table):
    r = jnp.arange(Q_BLOCK)
    rel = ((jnp.arange(3) - 1)[:, None, None] * Q_BLOCK
           + r[None, :, None] - r[None, None, :])
    bucket = jnp.where(jnp.abs(rel) <= WINDOW, _t5_bucket(rel), -1).astype(jnp.int32)
    return pl.pallas_call(
        _bias_kernel,
        out_shape=jax.ShapeDtypeStruct((N_HEADS, _KEYS_B, _TQ_B), _F32),
        in_specs=[pl.BlockSpec(memory_space=pltpu.SMEM),
                  pl.BlockSpec(memory_space=pltpu.VMEM)],
        out_specs=pl.BlockSpec(memory_space=pltpu.VMEM),
        compiler_params=pltpu.CompilerParams(vmem_limit_bytes=_VMEM_LIMIT_BYTES),
        name="t5_bias",
    )(rel_table, bucket)


def _norm_rope(xh, gain, cos, sin):
    ms = jnp.mean(xh * xh, axis=0, keepdims=True)
    y = xh * lax.rsqrt(ms + EPS) * gain
    q = HEAD_DIM // 4
    rot = jnp.concatenate([-y[q:2 * q], y[0:q], -y[3 * q:4 * q], y[2 * q:3 * q]], axis=0)
    return y * cos + rot * sin


def _inproj_kernel(x_ref, shift_ref, scale_ref, gpre_ref, w_ref, cos_ref, sin_ref,
                   qn_ref, kn_ref,
                   qa_ref, ka_ref, va_ref, ga_ref, qb_ref, kb_ref, vb_ref, gb_ref):
    x = x_ref[0]
    ms = jnp.mean(x * x, axis=-1, keepdims=True)
    h = x * lax.rsqrt(ms + EPS) * gpre_ref[...]
    h = h * (1.0 + scale_ref[0]) + shift_ref[0]
    hb = h.astype(_BF16)

    def proj(lo, n):
        return lax.dot_general(w_ref[lo:lo + n, :], hb, _NT, preferred_element_type=_F32)

    cos = cos_ref[...]
    sin = sin_ref[...]
    qn = qn_ref[...]
    kn = kn_ref[...]

    qa = proj(_QA, WIDTH)
    for hd in range(N_HEADS):
        rows = slice(hd * HEAD_DIM, (hd + 1) * HEAD_DIM)
        qa_ref[0, rows, :] = (_norm_rope(qa[rows], qn, cos, sin) * QK_SCALE).astype(_BF16)

    ka = proj(_KA, KV_WIDTH)
    ka = jnp.concatenate(
        [_norm_rope(ka[hd * HEAD_DIM:(hd + 1) * HEAD_DIM], kn, cos, sin) for hd in range(N_KV)],
        axis=0)
    ka_ref[0] = ka.T.astype(_BF16)

    va_ref[0] = proj(_VA, KV_WIDTH).astype(_BF16)
    ga_ref[0] = _silu(proj(_GA, WIDTH)).astype(_BF16)
    qb_ref[0] = (proj(_QB, WIDTH) * QK_SCALE).astype(_BF16)
    kb_ref[0] = proj(_KB, KV_WIDTH).T.astype(_BF16)
    vb_ref[0] = proj(_VB, KV_WIDTH).astype(_BF16)
    gb_ref[0] = _silu(proj(_GB, WIDTH)).astype(_BF16)


def _rope_tables(seq):
    rows = seq // GRID_W
    row = jnp.repeat(jnp.arange(rows), GRID_W).astype(_F32)
    col = jnp.tile(jnp.arange(GRID_W), rows).astype(_F32)
    half = HEAD_DIM // 2
    freqs = ROPE_THETA ** (-jnp.arange(0, half, 2, dtype=_F32) / half)

    def ang2(pos):
        ang = pos[:, None] * freqs[None, :]
        return jnp.concatenate([ang, ang], axis=-1)

    ang = jnp.concatenate([ang2(row), ang2(col)], axis=-1)
    return jnp.cos(ang).T, jnp.sin(ang).T


def _in_proj(x, shift, scale, g_pre, w_in_t, cos_t, sin_t, qn, kn):
    bsz, seq, d = x.shape
    tm = _TM_IN
    tok_t = lambda n: pl.BlockSpec((1, n, tm), lambda b, i: (b, 0, i))
    tok_m = lambda n: pl.BlockSpec((1, tm, n), lambda b, i: (b, i, 0))
    per_b = pl.BlockSpec((1, 1, d), lambda b, i: (b, 0, 0))
    tab = pl.BlockSpec((HEAD_DIM, tm), lambda b, i: (0, i))
    gain = pl.BlockSpec((HEAD_DIM, tm), lambda b, i: (0, 0))
    shp_t = lambda n: jax.ShapeDtypeStruct((bsz, n, seq), _BF16)
    shp_m = lambda n: jax.ShapeDtypeStruct((bsz, seq, n), _BF16)
    return pl.pallas_call(
        _inproj_kernel,
        out_shape=(shp_t(WIDTH), shp_m(KV_WIDTH), shp_t(KV_WIDTH), shp_t(WIDTH),
                   shp_t(WIDTH), shp_m(KV_WIDTH), shp_t(KV_WIDTH), shp_t(WIDTH)),
        grid=(bsz, seq // tm),
        in_specs=[pl.BlockSpec((1, tm, d), lambda b, i: (b, i, 0)),
                  per_b, per_b,
                  pl.BlockSpec((1, d), lambda b, i: (0, 0)),
                  pl.BlockSpec((IN_COLS, d), lambda b, i: (0, 0)),
                  tab, tab, gain, gain],
        out_specs=(tok_t(WIDTH), tok_m(KV_WIDTH), tok_t(KV_WIDTH), tok_t(WIDTH),
                   tok_t(WIDTH), tok_m(KV_WIDTH), tok_t(KV_WIDTH), tok_t(WIDTH)),
        compiler_params=_params(2),
        name="in_proj",
    )(x, shift, scale, g_pre.reshape(1, d), w_in_t, cos_t, sin_t,
      jnp.broadcast_to(qn[:, None], (HEAD_DIM, tm)),
      jnp.broadcast_to(kn[:, None], (HEAD_DIM, tm)))


def _padded_q(q_ref, hd):
    qh = q_ref[0, pl.ds(pl.multiple_of(hd * HEAD_DIM, HEAD_DIM), HEAD_DIM), :]
    z = jnp.zeros_like(qh)
    first = (hd // GROUP) == 0
    return jnp.concatenate([jnp.where(first, qh, z), jnp.where(first, z, qh)], axis=0)


def _kv_rows(hd):
    return pl.ds(pl.multiple_of((hd // GROUP) * HEAD_DIM, HEAD_DIM), HEAD_DIM)


def _head_rows(hd):
    return pl.ds(pl.multiple_of(hd * HEAD_DIM, HEAD_DIM), HEAD_DIM)


def _global_attn_kernel(q_ref, k_ref, v_ref, g_ref, o_ref, acc_ref):
    k = k_ref[0]

    def head(hd, carry):
        s = jnp.dot(k, _padded_q(q_ref, hd), preferred_element_type=_F32)
        m = jnp.max(s, axis=0, keepdims=True)
        e = jnp.exp(s - m)
        l = jnp.sum(e, axis=0, keepdims=True)
        o = jnp.dot(v_ref[0, _kv_rows(hd), :], e.astype(_BF16),
                    preferred_element_type=_F32)
        acc_ref[_head_rows(hd), :] = o * (1.0 / l) * g_ref[0, _head_rows(hd), :].astype(_F32)
        return carry

    lax.fori_loop(0, N_HEADS, head, 0)
    o_ref[0] = acc_ref[...].T.astype(_BF16)


def _global_attn(q_t, k, v_t, g_t):
    bsz, _, seq = q_t.shape
    tq = _TQ_A
    return pl.pallas_call(
        _global_attn_kernel,
        out_shape=jax.ShapeDtypeStruct((bsz, seq, WIDTH), _BF16),
        grid=(bsz, seq // tq),
        in_specs=[pl.BlockSpec((1, WIDTH, tq), lambda b, i: (b, 0, i)),
                  pl.BlockSpec((1, seq, KV_WIDTH), lambda b, i: (b, 0, 0)),
                  pl.BlockSpec((1, KV_WIDTH, seq), lambda b, i: (b, 0, 0)),
                  pl.BlockSpec((1, WIDTH, tq), lambda b, i: (b, 0, i))],
        out_specs=pl.BlockSpec((1, tq, WIDTH), lambda b, i: (b, i, 0)),
        scratch_shapes=[pltpu.VMEM((WIDTH, tq), _F32)],
        compiler_params=_params(2),
        name="global_attn",
    )(q_t, k, v_t, g_t)


def _window_attn_kernel(sink_ref, q_ref, k0_ref, k1_ref, k2_ref, k3_ref,
                        v0_ref, v1_ref, v2_ref, v3_ref, g_ref, bias_ref, o_ref, acc_ref):
    t = pl.program_id(1)
    last = pl.num_programs(1) - 1
    k = jnp.concatenate([k0_ref[0], k1_ref[0], k2_ref[0], k3_ref[0]], axis=0)
    v = jnp.concatenate([v0_ref[0], v1_ref[0], v2_ref[0], v3_ref[0]], axis=1)
    key_block = lax.broadcasted_iota(jnp.int32, (_KEYS_B, _TQ_B), 0) // Q_BLOCK
    outside = ((key_block == 0) & (t == 0)) | ((key_block == 3) & (t == last))

    def head(hd, carry):
        s = jnp.dot(k, _padded_q(q_ref, hd), preferred_element_type=_F32) + bias_ref[hd]
        s = jnp.where(outside, MASK_VALUE, s)
        sink = sink_ref[hd]
        m = jnp.maximum(jnp.max(s, axis=0, keepdims=True), sink)
        e = jnp.exp(s - m)
        l = jnp.sum(e, axis=0, keepdims=True) + jnp.exp(sink - m)
        vh = jnp.where((hd // GROUP) == 0, v[0:HEAD_DIM], v[HEAD_DIM:2 * HEAD_DIM])
        o = jnp.dot(vh, e.astype(_BF16), preferred_element_type=_F32)
        acc_ref[_head_rows(hd), :] = o * (1.0 / l) * g_ref[0, _head_rows(hd), :].astype(_F32)
        return carry

    lax.fori_loop(0, N_HEADS, head, 0)
    o_ref[0] = acc_ref[...].T.astype(_BF16)


def _window_attn(sink, q_t, k, v_t, g_t, bias):
    bsz, _, seq = q_t.shape
    tq = _TQ_B
    nblk = seq // Q_BLOCK
    per_tile = tq // Q_BLOCK

    def blk(j):
        return lambda i: jnp.clip(per_tile * i - 1 + j, 0, nblk - 1)

    k_spec = lambda j: pl.BlockSpec((1, Q_BLOCK, KV_WIDTH), lambda b, i: (b, blk(j)(i), 0))
    v_spec = lambda j: pl.BlockSpec((1, KV_WIDTH, Q_BLOCK), lambda b, i: (b, 0, blk(j)(i)))
    q_spec = pl.BlockSpec((1, WIDTH, tq), lambda b, i: (b, 0, i))
    return pl.pallas_call(
        _window_attn_kernel,
        out_shape=jax.ShapeDtypeStruct((bsz, seq, WIDTH), _BF16),
        grid=(bsz, seq // tq),
        in_specs=[pl.BlockSpec(memory_space=pltpu.SMEM), q_spec,
                  k_spec(0), k_spec(1), k_spec(2), k_spec(3),
                  v_spec(0), v_spec(1), v_spec(2), v_spec(3),
                  q_spec,
                  pl.BlockSpec((N_HEADS, _KEYS_B, tq), lambda b, i: (0, 0, 0))],
        out_specs=pl.BlockSpec((1, tq, WIDTH), lambda b, i: (b, i, 0)),
        scratch_shapes=[pltpu.VMEM((WIDTH, tq), _F32)],
        compiler_params=_params(2),
        name="window_attn",
    )(sink, q_t, k, k, k, k, v_t, v_t, v_t, v_t, g_t, bias)


def _outproj_kernel(oa_ref, ob_ref, w_ref, x_ref, gate_ref, gpost_ref, o_ref):
    y = (jnp.dot(oa_ref[0], w_ref[0:WIDTH, :], preferred_element_type=_F32)
         + jnp.dot(ob_ref[0], w_ref[WIDTH:2 * WIDTH, :], preferred_element_type=_F32))
    ms = jnp.mean(y * y, axis=-1, keepdims=True)
    yn = y * lax.rsqrt(ms + EPS) * gpost_ref[...]
    o_ref[0] = x_ref[0] + gate_ref[0] * yn


def _out_proj(oa, ob, w_out, x, gate, g_post):
    bsz, seq, d = x.shape
    tm = _TM_OUT
    act = pl.BlockSpec((1, tm, WIDTH), lambda b, i: (b, i, 0))
    xo = pl.BlockSpec((1, tm, d), lambda b, i: (b, i, 0))
    return pl.pallas_call(
        _outproj_kernel,
        out_shape=jax.ShapeDtypeStruct((bsz, seq, d), _F32),
        grid=(bsz, seq // tm),
        in_specs=[act, act,
                  pl.BlockSpec((2 * WIDTH, d), lambda b, i: (0, 0)),
                  xo,
                  pl.BlockSpec((1, 1, d), lambda b, i: (b, 0, 0)),
                  pl.BlockSpec((1, d), lambda b, i: (0, 0))],
        out_specs=xo,
        compiler_params=_params(2),
        name="out_proj",
    )(oa, ob, w_out, x, gate, g_post.reshape(1, d))


def kernel(x, c, w_ada, b_ada, g_pre, g_post, w_in, qn_a, kn_a, sink_b, w_out, rel_table):
    bsz, seq, d = x.shape
    depth = w_ada.shape[0]
    cos_t, sin_t = _rope_tables(seq)
    bias = _t5_bias_tiles(rel_table)
    for l in range(depth):
        mod = _adaln_mod(c, w_ada[l], b_ada[l])
        shift, scale, gate = (mod[:, i * d:(i + 1) * d].reshape(bsz, 1, d) for i in range(3))
        w_in_t = w_in[l].T.astype(_BF16)
        qa, ka, va, ga, qb, kb, vb, gb = _in_proj(
            x, shift, scale, g_pre[l], w_in_t, cos_t, sin_t, qn_a[l], kn_a[l])
        oa = _global_attn(qa, ka, va, ga)
        ob = _window_attn(sink_b[l], qb, kb, vb, gb, bias)
        x = _out_proj(oa, ob, w_out[l].astype(_BF16), x, gate, g_post[l])
    return x
```

```python
import functools

import jax
import jax.numpy as jnp
import numpy as np
from jax import lax
from jax.experimental import pallas as pl
from jax.experimental.pallas import tpu as pltpu

D_MODEL = 1024
HEAD_DIM = 64
N_HEADS = 8
N_KV = 2
GROUP = N_HEADS // N_KV
WIDTH = N_HEADS * HEAD_DIM
KV_WIDTH = N_KV * HEAD_DIM
IN_COLS = 2 * (2 * WIDTH + 2 * KV_WIDTH)
Q_BLOCK = 128
WINDOW = 128
GRID_W = 64
ROPE_THETA = 10000.0
N_BUCKETS = 32
MAX_DISTANCE = 128
EPS = 1e-6
MASK_VALUE = -1e30
LOG2E = 1.4426950408889634
Q_SCALE = HEAD_DIM ** -0.5 * LOG2E

_QA, _KA, _VA, _GA = 0, WIDTH, WIDTH + KV_WIDTH, WIDTH + 2 * KV_WIDTH
_HALF = 2 * WIDTH + 2 * KV_WIDTH
_QB, _KB, _VB, _GB = _HALF + _QA, _HALF + _KA, _HALF + _VA, _HALF + _GA

_F32 = jnp.float32
_BF16 = jnp.bfloat16

_VMEM_LIMIT_BYTES = 48 * 1024 * 1024

_TM_IN = 512
_TQ_A = 256
_TQ_B = 256
_KEYS_B = _TQ_B + 2 * WINDOW
_TM_OUT = 512
_TN_MOD = 512

_NT = (((1,), (1,)), ((), ()))


def _silu(v):
    return v * (1.0 / (1.0 + jnp.exp(-v)))


def _params(n_grid):
    return pltpu.CompilerParams(
        dimension_semantics=("arbitrary",) * n_grid,
        vmem_limit_bytes=_VMEM_LIMIT_BYTES)


def _mod_kernel(c_ref, w_ref, b_ref, o_ref):
    ca = _silu(c_ref[...])
    o_ref[...] = jnp.dot(ca.astype(_BF16), w_ref[...].astype(_BF16),
                         preferred_element_type=_F32) + b_ref[...]


def _adaln_mod(c, w, b):
    bsz, d = c.shape
    n = w.shape[1]
    return pl.pallas_call(
        _mod_kernel,
        out_shape=jax.ShapeDtypeStruct((bsz, n), _F32),
        grid=(n // _TN_MOD,),
        in_specs=[pl.BlockSpec((bsz, d), lambda j: (0, 0)),
                  pl.BlockSpec((d, _TN_MOD), lambda j: (0, j)),
                  pl.BlockSpec((1, _TN_MOD), lambda j: (0, j))],
        out_specs=pl.BlockSpec((bsz, _TN_MOD), lambda j: (0, j)),
        compiler_params=_params(1),
        name="adaln_mod",
    )(c, w, b.reshape(1, n))


def _bias_kernel(table_ref, bucket_ref, o_ref):
    masked = jnp.full((Q_BLOCK, Q_BLOCK), MASK_VALUE, _F32)
    for h in range(N_HEADS):
        tiles = []
        for d in range(3):
            bucket = bucket_ref[d]
            acc = masked
            for b in range(N_BUCKETS):
                acc = jnp.where(bucket == b, table_ref[b, h] * LOG2E, acc)
            tiles.append(acc)
        lo, mid, hi = tiles
        for jb, (left, right) in enumerate(((lo, masked), (mid, lo), (hi, mid), (masked, hi))):
            rows = slice(jb * Q_BLOCK, (jb + 1) * Q_BLOCK)
            o_ref[h, rows, 0:Q_BLOCK] = left
            o_ref[h, rows, Q_BLOCK:2 * Q_BLOCK] = right


def _t5_bucket(rel):
    nb = N_BUCKETS // 2
    max_exact = nb // 2
    ret = (rel > 0).astype(jnp.int32) * nb
    n = jnp.abs(rel)
    nf = jnp.maximum(n, max_exact).astype(jnp.float32)
    large = max_exact + (jnp.log(nf / max_exact) / np.log(MAX_DISTANCE / max_exact)
                         * (nb - max_exact)).astype(jnp.int32)
    large = jnp.minimum(large, nb - 1)
    return ret + jnp.where(n < max_exact, n, large)


def _t5_bias_tiles(rel_table):
    r = jnp.arange(Q_BLOCK)
    rel = ((jnp.arange(3) - 1)[:, None, None] * Q_BLOCK
           + r[None, :, None] - r[None, None, :])
    bucket = jnp.where(jnp.abs(rel) <= WINDOW, _t5_bucket(rel), -1).astype(jnp.int32)
    return pl.pallas_call(
        _bias_kernel,
        out_shape=jax.ShapeDtypeStruct((N_HEADS, _KEYS_B, _TQ_B), _F32),
        in_specs=[pl.BlockSpec(memory_space=pltpu.SMEM),
                  pl.BlockSpec(memory_space=pltpu.VMEM)],
        out_specs=pl.BlockSpec(memory_space=pltpu.VMEM),
        compiler_params=pltpu.CompilerParams(vmem_limit_bytes=_VMEM_LIMIT_BYTES),
        name="t5_bias",
    )(rel_table, bucket)


def _norm_rope(xh, gain, cos, sin):
    ms = jnp.mean(xh * xh, axis=0, keepdims=True)
    y = xh * lax.rsqrt(ms + EPS) * gain
    q = HEAD_DIM // 4
    rot = jnp.concatenate([-y[q:2 * q], y[0:q], -y[3 * q:4 * q], y[2 * q:3 * q]], axis=0)
    return y * cos + rot * sin


def _inproj_kernel(x_ref, shift_ref, scale_ref, gpre_ref, w_ref, cos_ref, sin_ref,
                   qn_ref, kn_ref,
                   qa_ref, ka_ref, va_ref, ga_ref, qb_ref, kb_ref, vb_ref, gb_ref):
    x = x_ref[0]
    ms = jnp.mean(x * x, axis=-1, keepdims=True)
    h = x * lax.rsqrt(ms + EPS) * gpre_ref[...]
    h = h * (1.0 + scale_ref[0]) + shift_ref[0]
    hb = h.astype(_BF16)

    def proj(lo, n):
        return lax.dot_general(w_ref[lo:lo + n, :], hb, _NT, preferred_element_type=_F32)

    cos = cos_ref[...]
    sin = sin_ref[...]
    qn = qn_ref[...]
    kn = kn_ref[...]

    qa = proj(_QA, WIDTH)
    for hd in range(N_HEADS):
        rows = slice(hd * HEAD_DIM, (hd + 1) * HEAD_DIM)
        qa_ref[0, rows, :] = (_norm_rope(qa[rows], qn, cos, sin) * Q_SCALE).astype(_BF16)

    ka = proj(_KA, KV_WIDTH)
    ka = jnp.concatenate(
        [_norm_rope(ka[hd * HEAD_DIM:(hd + 1) * HEAD_DIM], kn, cos, sin) for hd in range(N_KV)],
        axis=0)
    ka_ref[0] = ka.T.astype(_BF16)

    va_ref[0] = proj(_VA, KV_WIDTH).astype(_BF16)
    ga_ref[0] = _silu(proj(_GA, WIDTH)).astype(_BF16)
    qb_ref[0] = (proj(_QB, WIDTH) * Q_SCALE).astype(_BF16)
    kb_ref[0] = proj(_KB, KV_WIDTH).T.astype(_BF16)
    vb_ref[0] = proj(_VB, KV_WIDTH).astype(_BF16)
    gb_ref[0] = _silu(proj(_GB, WIDTH)).astype(_BF16)


def _rope_tables(seq):
    rows = seq // GRID_W
    row = jnp.repeat(jnp.arange(rows), GRID_W).astype(_F32)
    col = jnp.tile(jnp.arange(GRID_W), rows).astype(_F32)
    half = HEAD_DIM // 2
    freqs = ROPE_THETA ** (-jnp.arange(0, half, 2, dtype=_F32) / half)

    def ang2(pos):
        ang = pos[:, None] * freqs[None, :]
        return jnp.concatenate([ang, ang], axis=-1)

    ang = jnp.concatenate([ang2(row), ang2(col)], axis=-1)
    return jnp.cos(ang).T, jnp.sin(ang).T


def _in_proj(x, shift, scale, g_pre, w_in_t, cos_t, sin_t, qn, kn):
    bsz, seq, d = x.shape
    tm = _TM_IN
    tok_t = lambda n: pl.BlockSpec((1, n, tm), lambda b, i: (b, 0, i))
    tok_m = lambda n: pl.BlockSpec((1, tm, n), lambda b, i: (b, i, 0))
    per_b = pl.BlockSpec((1, 1, d), lambda b, i: (b, 0, 0))
    tab = pl.BlockSpec((HEAD_DIM, tm), lambda b, i: (0, i))
    gain = pl.BlockSpec((HEAD_DIM, tm), lambda b, i: (0, 0))
    shp_t = lambda n: jax.ShapeDtypeStruct((bsz, n, seq), _BF16)
    shp_m = lambda n: jax.ShapeDtypeStruct((bsz, seq, n), _BF16)
    return pl.pallas_call(
        _inproj_kernel,
        out_shape=(shp_t(WIDTH), shp_m(KV_WIDTH), shp_t(KV_WIDTH), shp_t(WIDTH),
                   shp_t(WIDTH), shp_m(KV_WIDTH), shp_t(KV_WIDTH), shp_t(WIDTH)),
        grid=(bsz, seq // tm),
        in_specs=[pl.BlockSpec((1, tm, d), lambda b, i: (b, i, 0)),
                  per_b, per_b,
                  pl.BlockSpec((1, d), lambda b, i: (0, 0)),
                  pl.BlockSpec((IN_COLS, d), lambda b, i: (0, 0)),
                  tab, tab, gain, gain],
        out_specs=(tok_t(WIDTH), tok_m(KV_WIDTH), tok_t(KV_WIDTH), tok_t(WIDTH),
                   tok_t(WIDTH), tok_m(KV_WIDTH), tok_t(KV_WIDTH), tok_t(WIDTH)),
        compiler_params=_params(2),
        name="in_proj",
    )(x, shift, scale, g_pre.reshape(1, d), w_in_t, cos_t, sin_t,
      jnp.broadcast_to(qn[:, None], (HEAD_DIM, tm)),
      jnp.broadcast_to(kn[:, None], (HEAD_DIM, tm)))


_PAIRS = N_HEADS // 2


def _head_rows(hd):
    return slice(hd * HEAD_DIM, (hd + 1) * HEAD_DIM)


def _pair_q(q_ref, pair):
    def padded(hd):
        qh = q_ref[0, _head_rows(hd), :]
        z = jnp.zeros_like(qh)
        return jnp.concatenate([qh, z] if hd // GROUP == 0 else [z, qh], axis=0)
    return jnp.concatenate([padded(2 * pair), padded(2 * pair + 1)], axis=1)


def _pair_out(o, inv_l, g_ref, pair, tq):
    o = o * inv_l
    g0 = g_ref[0, _head_rows(2 * pair), :].astype(_F32)
    g1 = g_ref[0, _head_rows(2 * pair + 1), :].astype(_F32)
    return jnp.concatenate([o[:, :tq] * g0, o[:, tq:] * g1], axis=0).T.astype(_BF16)


def _global_attn_kernel(q_ref, k_ref, v_ref, g_ref, o_ref, s0_ref, s1_ref):
    tq = q_ref.shape[2]
    k = k_ref[0]
    for pair in range(_PAIRS):
        s_ref = (s0_ref, s1_ref)[pair % 2]
        s = jnp.dot(k, _pair_q(q_ref, pair), preferred_element_type=_F32)
        s_ref[...] = s
        m = jnp.max(s, axis=0, keepdims=True)
        e = jnp.exp2(s_ref[...] - m)
        l = jnp.sum(e, axis=0, keepdims=True)
        o = jnp.dot(v_ref[0, _head_rows(pair // (GROUP // 2)), :], e.astype(_BF16),
                    preferred_element_type=_F32)
        o_ref[0, :, pair * 2 * HEAD_DIM:(pair + 1) * 2 * HEAD_DIM] = _pair_out(
            o, 1.0 / l, g_ref, pair, tq)


def _global_attn(q_t, k, v_t, g_t):
    bsz, _, seq = q_t.shape
    tq = _TQ_A
    return pl.pallas_call(
        _global_attn_kernel,
        out_shape=jax.ShapeDtypeStruct((bsz, seq, WIDTH), _BF16),
        grid=(bsz, seq // tq),
        in_specs=[pl.BlockSpec((1, WIDTH, tq), lambda b, i: (b, 0, i)),
                  pl.BlockSpec((1, seq, KV_WIDTH), lambda b, i: (b, 0, 0)),
                  pl.BlockSpec((1, KV_WIDTH, seq), lambda b, i: (b, 0, 0)),
                  pl.BlockSpec((1, WIDTH, tq), lambda b, i: (b, 0, i))],
        out_specs=pl.BlockSpec((1, tq, WIDTH), lambda b, i: (b, i, 0)),
        scratch_shapes=[pltpu.VMEM((seq, 2 * tq), _F32), pltpu.VMEM((seq, 2 * tq), _F32)],
        compiler_params=_params(2),
        name="global_attn",
    )(q_t, k, v_t, g_t)


def _window_attn_kernel(sink_ref, q_ref, k0_ref, k1_ref, k2_ref, k3_ref,
                        v0_ref, v1_ref, v2_ref, v3_ref, g_ref, bias_ref, o_ref):
    tq = q_ref.shape[2]
    t = pl.program_id(1)
    last = pl.num_programs(1) - 1
    k = jnp.concatenate([k0_ref[0], k1_ref[0], k2_ref[0], k3_ref[0]], axis=0)
    v = jnp.concatenate([v0_ref[0], v1_ref[0], v2_ref[0], v3_ref[0]], axis=1)
    first_lanes = lax.broadcasted_iota(jnp.int32, (1, 2 * tq), 1) < tq
    for pair in range(_PAIRS):
        bias = jnp.concatenate([bias_ref[2 * pair], bias_ref[2 * pair + 1]], axis=1)
        s = jnp.dot(k, _pair_q(q_ref, pair), preferred_element_type=_F32) + bias
        s = jnp.concatenate([jnp.where(t == 0, MASK_VALUE, s[:Q_BLOCK]),
                             s[Q_BLOCK:_KEYS_B - Q_BLOCK],
                             jnp.where(t == last, MASK_VALUE, s[_KEYS_B - Q_BLOCK:])], axis=0)
        sink = jnp.where(first_lanes, sink_ref[2 * pair], sink_ref[2 * pair + 1]) * LOG2E
        m = jnp.maximum(jnp.max(s, axis=0, keepdims=True), sink)
        e = jnp.exp2(s - m)
        l = jnp.sum(e, axis=0, keepdims=True) + jnp.exp2(sink - m)
        o = jnp.dot(v[_head_rows(pair // (GROUP // 2)), :], e.astype(_BF16),
                    preferred_element_type=_F32)
        o_ref[0, :, pair * 2 * HEAD_DIM:(pair + 1) * 2 * HEAD_DIM] = _pair_out(
            o, 1.0 / l, g_ref, pair, tq)


def _window_attn(sink, q_t, k, v_t, g_t, bias):
    bsz, _, seq = q_t.shape
    tq = _TQ_B
    nblk = seq // Q_BLOCK
    per_tile = tq // Q_BLOCK

    def blk(j):
        return lambda i: jnp.clip(per_tile * i - 1 + j, 0, nblk - 1)

    k_spec = lambda j: pl.BlockSpec((1, Q_BLOCK, KV_WIDTH), lambda b, i: (b, blk(j)(i), 0))
    v_spec = lambda j: pl.BlockSpec((1, KV_WIDTH, Q_BLOCK), lambda b, i: (b, 0, blk(j)(i)))
    q_spec = pl.BlockSpec((1, WIDTH, tq), lambda b, i: (b, 0, i))
    return pl.pallas_call(
        _window_attn_kernel,
        out_shape=jax.ShapeDtypeStruct((bsz, seq, WIDTH), _BF16),
        grid=(bsz, seq // tq),
        in_specs=[pl.BlockSpec(memory_space=pltpu.SMEM), q_spec,
                  k_spec(0), k_spec(1), k_spec(2), k_spec(3),
                  v_spec(0), v_spec(1), v_spec(2), v_spec(3),
                  q_spec,
                  pl.BlockSpec((N_HEADS, _KEYS_B, tq), lambda b, i: (0, 0, 0))],
        out_specs=pl.BlockSpec((1, tq, WIDTH), lambda b, i: (b, i, 0)),
        compiler_params=_params(2),
        name="window_attn",
    )(sink, q_t, k, k, k, k, v_t, v_t, v_t, v_t, g_t, bias)


def _outproj_kernel(oa_ref, ob_ref, w_ref, x_ref, gate_ref, gpost_ref, o_ref):
    y = (jnp.dot(oa_ref[0], w_ref[0:WIDTH, :], preferred_element_type=_F32)
         + jnp.dot(ob_ref[0], w_ref[WIDTH:2 * WIDTH, :], preferred_element_type=_F32))
    ms = jnp.mean(y * y, axis=-1, keepdims=True)
    yn = y * lax.rsqrt(ms + EPS) * gpost_ref[...]
    o_ref[0] = x_ref[0] + gate_ref[0] * yn


def _out_proj(oa, ob, w_out, x, gate, g_post):
    bsz, seq, d = x.shape
    tm = _TM_OUT
    act = pl.BlockSpec((1, tm, WIDTH), lambda b, i: (b, i, 0))
    xo = pl.BlockSpec((1, tm, d), lambda b, i: (b, i, 0))
    return pl.pallas_call(
        _outproj_kernel,
        out_shape=jax.ShapeDtypeStruct((bsz, seq, d), _F32),
        grid=(bsz, seq // tm),
        in_specs=[act, act,
                  pl.BlockSpec((2 * WIDTH, d), lambda b, i: (0, 0)),
                  xo,
                  pl.BlockSpec((1, 1, d), lambda b, i: (b, 0, 0)),
                  pl.BlockSpec((1, d), lambda b, i: (0, 0))],
        out_specs=xo,
        compiler_params=_params(2),
        name="out_proj",
    )(oa, ob, w_out, x, gate, g_post.reshape(1, d))


def kernel(x, c, w_ada, b_ada, g_pre, g_post, w_in, qn_a, kn_a, sink_b, w_out, rel_table):
    bsz, seq, d = x.shape
    depth = w_ada.shape[0]
    cos_t, sin_t = _rope_tables(seq)
    bias = _t5_bias_tiles(rel_table)
    for l in range(depth):
        mod = _adaln_mod(c, w_ada[l], b_ada[l])
        shift, scale, gate = (mod[:, i * d:(i + 1) * d].reshape(bsz, 1, d) for i in range(3))
        w_in_t = w_in[l].T.astype(_BF16)
        qa, ka, va, ga, qb, kb, vb, gb = _in_proj(
            x, shift, scale, g_pre[l], w_in_t, cos_t, sin_t, qn_a[l], kn_a[l])
        oa = _global_attn(qa, ka, va, ga)
        ob = _window_attn(sink_b[l], qb, kb, vb, gb, bias)
        x = _out_proj(oa, ob, w_out[l].astype(_BF16), x, gate, g_post[l])
    return x
```

```python
import functools

import jax
import jax.numpy as jnp
import numpy as np
from jax import lax
from jax.experimental import pallas as pl
from jax.experimental.pallas import tpu as pltpu

D_MODEL = 1024
HEAD_DIM = 64
N_HEADS = 8
N_KV = 2
GROUP = N_HEADS // N_KV
WIDTH = N_HEADS * HEAD_DIM
KV_WIDTH = N_KV * HEAD_DIM
IN_COLS = 2 * (2 * WIDTH + 2 * KV_WIDTH)
Q_BLOCK = 128
WINDOW = 128
GRID_W = 64
ROPE_THETA = 10000.0
N_BUCKETS = 32
MAX_DISTANCE = 128
EPS = 1e-6
MASK_VALUE = -1e30
LOG2E = 1.4426950408889634
Q_SCALE = HEAD_DIM ** -0.5 * LOG2E

_QA, _KA, _VA, _GA = 0, WIDTH, WIDTH + KV_WIDTH, WIDTH + 2 * KV_WIDTH
_HALF = 2 * WIDTH + 2 * KV_WIDTH
_QB, _KB, _VB, _GB = _HALF + _QA, _HALF + _KA, _HALF + _VA, _HALF + _GA

_F32 = jnp.float32
_BF16 = jnp.bfloat16

_VMEM_LIMIT_BYTES = 48 * 1024 * 1024

_TM_IN = 512
_TQ_A = 256
_TK_A = 512
_TQ_B = 256
_KEYS_B = _TQ_B + 2 * WINDOW
_TM_OUT = 512
_TN_MOD = 512

_NT = (((1,), (1,)), ((), ()))


def _silu(v):
    return v * (1.0 / (1.0 + jnp.exp(-v)))


def _params(n_grid, flags=None):
    return pltpu.CompilerParams(
        dimension_semantics=("arbitrary",) * n_grid,
        vmem_limit_bytes=_VMEM_LIMIT_BYTES, flags=flags)


def _mod_kernel(c_ref, w_ref, b_ref, o_ref):
    ca = _silu(c_ref[...])
    o_ref[...] = jnp.dot(ca.astype(_BF16), w_ref[...].astype(_BF16),
                         preferred_element_type=_F32) + b_ref[...]


def _adaln_mod(c, w, b):
    bsz, d = c.shape
    n = w.shape[1]
    return pl.pallas_call(
        _mod_kernel,
        out_shape=jax.ShapeDtypeStruct((bsz, n), _F32),
        grid=(n // _TN_MOD,),
        in_specs=[pl.BlockSpec((bsz, d), lambda j: (0, 0)),
                  pl.BlockSpec((d, _TN_MOD), lambda j: (0, j)),
                  pl.BlockSpec((1, _TN_MOD), lambda j: (0, j))],
        out_specs=pl.BlockSpec((bsz, _TN_MOD), lambda j: (0, j)),
        compiler_params=_params(1),
        name="adaln_mod",
    )(c, w, b.reshape(1, n))


def _bias_kernel(table_ref, bucket_ref, o_ref):
    masked = jnp.full((Q_BLOCK, Q_BLOCK), MASK_VALUE, _F32)
    for h in range(N_HEADS):
        table = table_ref[h] * LOG2E
        tiles = []
        for d in range(3):
            bucket = bucket_ref[d]
            acc = masked
            for b in range(N_BUCKETS):
                acc = jnp.where(bucket == b, table[b:b + 1, :], acc)
            tiles.append(acc)
        lo, mid, hi = tiles
        for jb, (left, right) in enumerate(((lo, masked), (mid, lo), (hi, mid), (masked, hi))):
            rows = slice(jb * Q_BLOCK, (jb + 1) * Q_BLOCK)
            o_ref[h, rows, 0:Q_BLOCK] = left
            o_ref[h, rows, Q_BLOCK:2 * Q_BLOCK] = right


def _t5_bucket(rel):
    nb = N_BUCKETS // 2
    max_exact = nb // 2
    ret = (rel > 0).astype(jnp.int32) * nb
    n = jnp.abs(rel)
    assert MAX_DISTANCE == 2 * max_exact ** 2 and nb - max_exact == 8
    n2 = n * n
    large = max_exact + sum((n2 >= 2 ** k).astype(jnp.int32) for k in range(7, 31))
    large = jnp.minimum(large, nb - 1)
    return ret + jnp.where(n < max_exact, n, large)


def _t5_bias_tiles(rel_table):
    r = jnp.arange(Q_BLOCK)
    rel = ((jnp.arange(3) - 1)[:, None, None] * Q_BLOCK
           + r[None, :, None] - r[None, None, :])
    bucket = jnp.where(jnp.abs(rel) <= WINDOW, _t5_bucket(rel), -1).astype(jnp.int32)
    return pl.pallas_call(
        _bias_kernel,
        out_shape=jax.ShapeDtypeStruct((N_HEADS, _KEYS_B, _TQ_B), _F32),
        in_specs=[pl.BlockSpec(memory_space=pltpu.VMEM),
                  pl.BlockSpec(memory_space=pltpu.VMEM)],
        out_specs=pl.BlockSpec(memory_space=pltpu.VMEM),
        compiler_params=pltpu.CompilerParams(vmem_limit_bytes=_VMEM_LIMIT_BYTES),
        name="t5_bias",
    )(jnp.broadcast_to(rel_table.T[:, :, None], (N_HEADS, N_BUCKETS, Q_BLOCK)), bucket)


def _norm_rope(xh, gain, cos, sin):
    ms = jnp.mean(xh * xh, axis=0, keepdims=True)
    y = xh * lax.rsqrt(ms + EPS) * gain
    q = HEAD_DIM // 4
    rot = jnp.concatenate([-y[q:2 * q], y[0:q], -y[3 * q:4 * q], y[2 * q:3 * q]], axis=0)
    return y * cos + rot * sin


def _inproj_kernel(x_ref, shift_ref, scale_ref, gpre_ref, w_ref, cos_ref, sin_ref,
                   qn_ref, kn_ref,
                   qa_ref, ka_ref, va_ref, ga_ref, qb_ref, kb_ref, vb_ref, gb_ref):
    x = x_ref[0]
    ms = jnp.mean(x * x, axis=-1, keepdims=True)
    h = x * lax.rsqrt(ms + EPS) * gpre_ref[...]
    h = h * (1.0 + scale_ref[0]) + shift_ref[0]
    hb = h.astype(_BF16)

    def proj(lo, n):
        return lax.dot_general(w_ref[lo:lo + n, :], hb, _NT, preferred_element_type=_F32)

    cos = cos_ref[...]
    sin = sin_ref[...]
    qn = qn_ref[...]
    kn = kn_ref[...]

    qa = proj(_QA, WIDTH)
    for hd in range(N_HEADS):
        rows = slice(hd * HEAD_DIM, (hd + 1) * HEAD_DIM)
        qa_ref[0, rows, :] = (_norm_rope(qa[rows], qn, cos, sin) * Q_SCALE).astype(_BF16)

    ka = proj(_KA, KV_WIDTH)
    ka = jnp.concatenate(
        [_norm_rope(ka[hd * HEAD_DIM:(hd + 1) * HEAD_DIM], kn, cos, sin) for hd in range(N_KV)],
        axis=0)
    ka_ref[0] = ka.T.astype(_BF16)

    va_ref[0] = proj(_VA, KV_WIDTH).astype(_BF16)
    ga_ref[0] = _silu(proj(_GA, WIDTH)).astype(_BF16)
    qb_ref[0] = (proj(_QB, WIDTH) * Q_SCALE).astype(_BF16)
    kb_ref[0] = proj(_KB, KV_WIDTH).T.astype(_BF16)
    vb_ref[0] = proj(_VB, KV_WIDTH).astype(_BF16)
    gb_ref[0] = _silu(proj(_GB, WIDTH)).astype(_BF16)


def _rope_tables(seq):
    rows = seq // GRID_W
    row = jnp.repeat(jnp.arange(rows), GRID_W).astype(_F32)
    col = jnp.tile(jnp.arange(GRID_W), rows).astype(_F32)
    half = HEAD_DIM // 2
    freqs = ROPE_THETA ** (-jnp.arange(0, half, 2, dtype=_F32) / half)

    def ang2(pos):
        ang = pos[:, None] * freqs[None, :]
        return jnp.concatenate([ang, ang], axis=-1)

    ang = jnp.concatenate([ang2(row), ang2(col)], axis=-1)
    return jnp.cos(ang).T, jnp.sin(ang).T


def _in_proj(x, shift, scale, g_pre, w_in_t, cos_t, sin_t, qn, kn):
    bsz, seq, d = x.shape
    tm = _TM_IN
    tok_t = lambda n: pl.BlockSpec((1, n, tm), lambda b, i: (b, 0, i))
    tok_m = lambda n: pl.BlockSpec((1, tm, n), lambda b, i: (b, i, 0))
    per_b = pl.BlockSpec((1, 1, d), lambda b, i: (b, 0, 0))
    tab = pl.BlockSpec((HEAD_DIM, tm), lambda b, i: (0, i))
    gain = pl.BlockSpec((HEAD_DIM, tm), lambda b, i: (0, 0))
    shp_t = lambda n: jax.ShapeDtypeStruct((bsz, n, seq), _BF16)
    shp_m = lambda n: jax.ShapeDtypeStruct((bsz, seq, n), _BF16)
    return pl.pallas_call(
        _inproj_kernel,
        out_shape=(shp_t(WIDTH), shp_m(KV_WIDTH), shp_t(KV_WIDTH), shp_t(WIDTH),
                   shp_t(WIDTH), shp_m(KV_WIDTH), shp_t(KV_WIDTH), shp_t(WIDTH)),
        grid=(bsz, seq // tm),
        in_specs=[pl.BlockSpec((1, tm, d), lambda b, i: (b, i, 0)),
                  per_b, per_b,
                  pl.BlockSpec((1, d), lambda b, i: (0, 0)),
                  pl.BlockSpec((IN_COLS, d), lambda b, i: (0, 0)),
                  tab, tab, gain, gain],
        out_specs=(tok_t(WIDTH), tok_m(KV_WIDTH), tok_t(KV_WIDTH), tok_t(WIDTH),
                   tok_t(WIDTH), tok_m(KV_WIDTH), tok_t(KV_WIDTH), tok_t(WIDTH)),
        compiler_params=_params(2),
        name="in_proj",
    )(x, shift, scale, g_pre.reshape(1, d), w_in_t, cos_t, sin_t,
      jnp.broadcast_to(qn[:, None], (HEAD_DIM, tm)),
      jnp.broadcast_to(kn[:, None], (HEAD_DIM, tm)))


_PAIRS = N_HEADS // 2
_ONES_ROWS = 16


def _head_rows(hd):
    return slice(hd * HEAD_DIM, (hd + 1) * HEAD_DIM)


def _pair_q(q_ref, pair):
    def padded(hd):
        qh = q_ref[0, _head_rows(hd), :]
        z = jnp.zeros_like(qh)
        return jnp.concatenate([qh, z] if hd // GROUP == 0 else [z, qh], axis=0)
    return jnp.concatenate([padded(2 * pair), padded(2 * pair + 1)], axis=1)


def _pair_out(o, inv_l, g_ref, pair, tq):
    o = o * inv_l
    g0 = g_ref[0, _head_rows(2 * pair), :].astype(_F32)
    g1 = g_ref[0, _head_rows(2 * pair + 1), :].astype(_F32)
    return jnp.concatenate([o[:, :tq] * g0, o[:, tq:] * g1], axis=0).T.astype(_BF16)


def _v_with_ones(v):
    return jnp.concatenate([v, jnp.ones((_ONES_ROWS, v.shape[1]), v.dtype)], axis=0)


def _global_attn_kernel(q_ref, k_ref, v_ref, g_ref, o_ref, s0_ref, s1_ref):
    tq = q_ref.shape[2]
    seq = k_ref.shape[1]
    s_refs = (s0_ref, s1_ref)
    chunks = [slice(c * _TK_A, (c + 1) * _TK_A) for c in range(seq // _TK_A)]
    zero = jnp.minimum(pl.program_id(0), 0)

    def dyn(rows):
        return pl.ds(pl.multiple_of(zero + rows.start, _TK_A), _TK_A)

    def scores(pair, rows):
        s = jnp.dot(k_ref[0, rows, :], _pair_q(q_ref, pair), preferred_element_type=_F32)
        s_refs[pair % 2][dyn(rows), :] = s
        return jnp.max(s, axis=0, keepdims=True)

    def col_max(parts):
        return functools.reduce(jnp.maximum, parts)

    m = col_max([scores(0, rows) for rows in chunks])
    for pair in range(_PAIRS):
        nxt, o = [], None
        for rows in chunks:
            if pair + 1 < _PAIRS:
                nxt.append(scores(pair + 1, rows))
            e = jnp.exp2(s_refs[pair % 2][dyn(rows), :] - m)
            v = _v_with_ones(v_ref[0, _head_rows(pair // (GROUP // 2)), rows])
            pv = jnp.dot(v, e.astype(_BF16), preferred_element_type=_F32)
            o = pv if o is None else o + pv
        o_ref[0, :, pair * 2 * HEAD_DIM:(pair + 1) * 2 * HEAD_DIM] = _pair_out(
            o[:HEAD_DIM], 1.0 / o[HEAD_DIM:HEAD_DIM + 1], g_ref, pair, tq)
        if nxt:
            m = col_max(nxt)


def _global_attn(q_t, k, v_t, g_t):
    bsz, _, seq = q_t.shape
    tq = _TQ_A
    return pl.pallas_call(
        _global_attn_kernel,
        out_shape=jax.ShapeDtypeStruct((bsz, seq, WIDTH), _BF16),
        grid=(bsz, seq // tq),
        in_specs=[pl.BlockSpec((1, WIDTH, tq), lambda b, i: (b, 0, i)),
                  pl.BlockSpec((1, seq, KV_WIDTH), lambda b, i: (b, 0, 0)),
                  pl.BlockSpec((1, KV_WIDTH, seq), lambda b, i: (b, 0, 0)),
                  pl.BlockSpec((1, WIDTH, tq), lambda b, i: (b, 0, i))],
        out_specs=pl.BlockSpec((1, tq, WIDTH), lambda b, i: (b, i, 0)),
        scratch_shapes=[pltpu.VMEM((seq, 2 * tq), _F32), pltpu.VMEM((seq, 2 * tq), _F32)],
        compiler_params=_params(2),
        name="global_attn",
    )(q_t, k, v_t, g_t)


def _window_attn_kernel(sink_ref, q_ref, k0_ref, k1_ref, k2_ref, k3_ref,
                        v0_ref, v1_ref, v2_ref, v3_ref, g_ref, bias_ref, o_ref):
    tq = q_ref.shape[2]
    t = pl.program_id(1)
    last = pl.num_programs(1) - 1
    k = jnp.concatenate([k0_ref[0], k1_ref[0], k2_ref[0], k3_ref[0]], axis=0)
    v = jnp.concatenate([v0_ref[0], v1_ref[0], v2_ref[0], v3_ref[0]], axis=1)
    for pair in range(_PAIRS):
        bias = jnp.concatenate([bias_ref[2 * pair], bias_ref[2 * pair + 1]], axis=1)
        s = jnp.dot(k, _pair_q(q_ref, pair), preferred_element_type=_F32) + bias
        s = jnp.concatenate([jnp.where(t == 0, MASK_VALUE, s[:Q_BLOCK]),
                             s[Q_BLOCK:_KEYS_B - Q_BLOCK],
                             jnp.where(t == last, MASK_VALUE, s[_KEYS_B - Q_BLOCK:])], axis=0)
        sink = sink_ref[pair] * LOG2E
        m = jnp.maximum(jnp.max(s, axis=0, keepdims=True), sink)
        e = jnp.exp2(s - m)
        o = jnp.dot(_v_with_ones(v[_head_rows(pair // (GROUP // 2)), :]), e.astype(_BF16),
                    preferred_element_type=_F32)
        l = o[HEAD_DIM:HEAD_DIM + 1] + jnp.exp2(sink - m)
        o_ref[0, :, pair * 2 * HEAD_DIM:(pair + 1) * 2 * HEAD_DIM] = _pair_out(
            o[:HEAD_DIM], 1.0 / l, g_ref, pair, tq)


def _window_attn(sink, q_t, k, v_t, g_t, bias):
    bsz, _, seq = q_t.shape
    tq = _TQ_B
    nblk = seq // Q_BLOCK
    per_tile = tq // Q_BLOCK

    def blk(j):
        return lambda i: jnp.clip(per_tile * i - 1 + j, 0, nblk - 1)

    k_spec = lambda j: pl.BlockSpec((1, Q_BLOCK, KV_WIDTH), lambda b, i: (b, blk(j)(i), 0))
    v_spec = lambda j: pl.BlockSpec((1, KV_WIDTH, Q_BLOCK), lambda b, i: (b, 0, blk(j)(i)))
    q_spec = pl.BlockSpec((1, WIDTH, tq), lambda b, i: (b, 0, i))
    return pl.pallas_call(
        _window_attn_kernel,
        out_shape=jax.ShapeDtypeStruct((bsz, seq, WIDTH), _BF16),
        grid=(bsz, seq // tq),
        in_specs=[pl.BlockSpec((_PAIRS, 1, 2 * tq), lambda b, i: (0, 0, 0)), q_spec,
                  k_spec(0), k_spec(1), k_spec(2), k_spec(3),
                  v_spec(0), v_spec(1), v_spec(2), v_spec(3),
                  q_spec,
                  pl.BlockSpec((N_HEADS, _KEYS_B, tq), lambda b, i: (0, 0, 0))],
        out_specs=pl.BlockSpec((1, tq, WIDTH), lambda b, i: (b, i, 0)),
        compiler_params=_params(2),
        name="window_attn",
    )(jnp.broadcast_to(sink[:, None], (N_HEADS, tq)).reshape(_PAIRS, 1, 2 * tq),
      q_t, k, k, k, k, v_t, v_t, v_t, v_t, g_t, bias)


def _outproj_kernel(oa_ref, ob_ref, w_ref, x_ref, gate_ref, gpost_ref, o_ref):
    y = (jnp.dot(oa_ref[0], w_ref[0:WIDTH, :], preferred_element_type=_F32)
         + jnp.dot(ob_ref[0], w_ref[WIDTH:2 * WIDTH, :], preferred_element_type=_F32))
    ms = jnp.mean(y * y, axis=-1, keepdims=True)
    yn = y * lax.rsqrt(ms + EPS) * gpost_ref[...]
    o_ref[0] = x_ref[0] + gate_ref[0] * yn


def _out_proj(oa, ob, w_out, x, gate, g_post):
    bsz, seq, d = x.shape
    tm = _TM_OUT
    act = pl.BlockSpec((1, tm, WIDTH), lambda b, i: (b, i, 0))
    xo = pl.BlockSpec((1, tm, d), lambda b, i: (b, i, 0))
    return pl.pallas_call(
        _outproj_kernel,
        out_shape=jax.ShapeDtypeStruct((bsz, seq, d), _F32),
        grid=(bsz, seq // tm),
        in_specs=[act, act,
                  pl.BlockSpec((2 * WIDTH, d), lambda b, i: (0, 0)),
                  xo,
                  pl.BlockSpec((1, 1, d), lambda b, i: (b, 0, 0)),
                  pl.BlockSpec((1, d), lambda b, i: (0, 0))],
        out_specs=xo,
        compiler_params=_params(2),
        name="out_proj",
    )(oa, ob, w_out, x, gate, g_post.reshape(1, d))


def kernel(x, c, w_ada, b_ada, g_pre, g_post, w_in, qn_a, kn_a, sink_b, w_out, rel_table):
    bsz, seq, d = x.shape
    depth = w_ada.shape[0]
    cos_t, sin_t = _rope_tables(seq)
    bias = _t5_bias_tiles(rel_table)
    for l in range(depth):
        mod = _adaln_mod(c, w_ada[l], b_ada[l])
        shift, scale, gate = (mod[:, i * d:(i + 1) * d].reshape(bsz, 1, d) for i in range(3))
        w_in_t = w_in[l].T.astype(_BF16)
        qa, ka, va, ga, qb, kb, vb, gb = _in_proj(
            x, shift, scale, g_pre[l], w_in_t, cos_t, sin_t, qn_a[l], kn_a[l])
        oa = _global_attn(qa, ka, va, ga)
        ob = _window_attn(sink_b[l], qb, kb, vb, gb, bias)
        x = _out_proj(oa, ob, w_out[l].astype(_BF16), x, gate, g_post[l])
    return x
```

```python
import functools

import jax
import jax.numpy as jnp
import numpy as np
from jax import lax
from jax.experimental import pallas as pl
from jax.experimental.pallas import tpu as pltpu

D_MODEL = 1024
HEAD_DIM = 64
N_HEADS = 8
N_KV = 2
GROUP = N_HEADS // N_KV
WIDTH = N_HEADS * HEAD_DIM
KV_WIDTH = N_KV * HEAD_DIM
IN_COLS = 2 * (2 * WIDTH + 2 * KV_WIDTH)
Q_BLOCK = 128
WINDOW = 128
GRID_W = 64
ROPE_THETA = 10000.0
N_BUCKETS = 32
MAX_DISTANCE = 128
EPS = 1e-6
MASK_VALUE = -1e30
LOG2E = 1.4426950408889634
Q_SCALE = HEAD_DIM ** -0.5 * LOG2E

_QA, _KA, _VA, _GA = 0, WIDTH, WIDTH + KV_WIDTH, WIDTH + 2 * KV_WIDTH
_HALF = 2 * WIDTH + 2 * KV_WIDTH
_QB, _KB, _VB, _GB = _HALF + _QA, _HALF + _KA, _HALF + _VA, _HALF + _GA
_W_ROW_ORDER = ((_QA, WIDTH), (_QB, WIDTH), (_KA, KV_WIDTH), (_KB, KV_WIDTH),
                (_GA, WIDTH), (_GB, WIDTH), (_VA, KV_WIDTH), (_VB, KV_WIDTH))
_QK_ROWS = 2 * WIDTH + 2 * KV_WIDTH
_GATE_ROWS = 2 * WIDTH
_V_ROWS = 2 * KV_WIDTH

_F32 = jnp.float32
_BF16 = jnp.bfloat16

_VMEM_LIMIT_BYTES = 48 * 1024 * 1024

_TM_IN = 512
_TQ_A = 256
_TK_A = 512
_TQ_B = 256
_KEYS_B = _TQ_B + 2 * WINDOW
_TM_OUT = 1024
_TN_MOD = 512

_NT = (((1,), (1,)), ((), ()))


def _silu(v):
    return 0.5 * v * (1.0 + jnp.tanh(0.5 * v))


def _params(n_grid, flags=None):
    return pltpu.CompilerParams(
        dimension_semantics=("arbitrary",) * n_grid,
        vmem_limit_bytes=_VMEM_LIMIT_BYTES, flags=flags)


def _mod_kernel(c_ref, w_ref, b_ref, o_ref):
    ca = _silu(c_ref[...])
    o_ref[...] = jnp.dot(ca.astype(_BF16), w_ref[...].astype(_BF16),
                         preferred_element_type=_F32) + b_ref[...]


def _adaln_mod(c, w, b):
    bsz, d = c.shape
    n = w.shape[1]
    return pl.pallas_call(
        _mod_kernel,
        out_shape=jax.ShapeDtypeStruct((bsz, n), _F32),
        grid=(n // _TN_MOD,),
        in_specs=[pl.BlockSpec((bsz, d), lambda j: (0, 0)),
                  pl.BlockSpec((d, _TN_MOD), lambda j: (0, j)),
                  pl.BlockSpec((1, _TN_MOD), lambda j: (0, j))],
        out_specs=pl.BlockSpec((bsz, _TN_MOD), lambda j: (0, j)),
        compiler_params=_params(1),
        name="adaln_mod",
    )(c, w, b.reshape(1, n))


def _bias_kernel(table_ref, bucket_ref, o_ref):
    masked = jnp.full((Q_BLOCK, Q_BLOCK), MASK_VALUE, _F32)
    for h in range(N_HEADS):
        table = table_ref[h] * LOG2E
        tiles = []
        for d in range(3):
            bucket = bucket_ref[d]
            acc = masked
            for b in range(N_BUCKETS):
                acc = jnp.where(bucket == b, table[b:b + 1, :], acc)
            tiles.append(acc)
        lo, mid, hi = tiles
        for jb, (left, right) in enumerate(((lo, masked), (mid, lo), (hi, mid), (masked, hi))):
            rows = slice(jb * Q_BLOCK, (jb + 1) * Q_BLOCK)
            o_ref[h, rows, 0:Q_BLOCK] = left
            o_ref[h, rows, Q_BLOCK:2 * Q_BLOCK] = right


def _t5_bucket(rel):
    nb = N_BUCKETS // 2
    max_exact = nb // 2
    ret = (rel > 0).astype(jnp.int32) * nb
    n = jnp.abs(rel)
    assert MAX_DISTANCE == 2 * max_exact ** 2 and nb - max_exact == 8
    n2 = n * n
    large = max_exact + sum((n2 >= 2 ** k).astype(jnp.int32) for k in range(7, 31))
    large = jnp.minimum(large, nb - 1)
    return ret + jnp.where(n < max_exact, n, large)


def _t5_bias_tiles(rel_table):
    r = jnp.arange(Q_BLOCK)
    rel = ((jnp.arange(3) - 1)[:, None, None] * Q_BLOCK
           + r[None, :, None] - r[None, None, :])
    bucket = jnp.where(jnp.abs(rel) <= WINDOW, _t5_bucket(rel), -1).astype(jnp.int32)
    return pl.pallas_call(
        _bias_kernel,
        out_shape=jax.ShapeDtypeStruct((N_HEADS, _KEYS_B, _TQ_B), _F32),
        in_specs=[pl.BlockSpec(memory_space=pltpu.VMEM),
                  pl.BlockSpec(memory_space=pltpu.VMEM)],
        out_specs=pl.BlockSpec(memory_space=pltpu.VMEM),
        compiler_params=pltpu.CompilerParams(vmem_limit_bytes=_VMEM_LIMIT_BYTES),
        name="t5_bias",
    )(jnp.broadcast_to(rel_table.T[:, :, None], (N_HEADS, N_BUCKETS, Q_BLOCK)), bucket)


def _norm_rope(xh, gain, cos, sin):
    ms = jnp.mean(xh * xh, axis=0, keepdims=True)
    y = xh * lax.rsqrt(ms + EPS) * gain
    q = HEAD_DIM // 4
    rot = jnp.concatenate([-y[q:2 * q], y[0:q], -y[3 * q:4 * q], y[2 * q:3 * q]], axis=0)
    return y * cos + rot * sin


def _inproj_kernel(x_ref, shift_ref, scale_ref, gpre_ref, w_ref, cos_ref, sin_ref,
                   qn_ref, kn_ref,
                   qa_ref, ka_ref, va_ref, ga_ref, qb_ref, kb_ref, vb_ref, gb_ref):
    x = x_ref[0]
    ms = jnp.mean(x * x, axis=-1, keepdims=True)
    h = x * lax.rsqrt(ms + EPS) * gpre_ref[...]
    h = h * (1.0 + scale_ref[0]) + shift_ref[0]
    hb = h.astype(_BF16)

    def proj(lo, n):
        return lax.dot_general(w_ref[lo:lo + n, :], hb, _NT, preferred_element_type=_F32)

    cos = cos_ref[...]
    sin = sin_ref[...]
    qn = qn_ref[...]
    kn = kn_ref[...]

    qk = proj(0, _QK_ROWS)
    for hd in range(N_HEADS):
        rows = slice(hd * HEAD_DIM, (hd + 1) * HEAD_DIM)
        qa_ref[0, rows, :] = (_norm_rope(qk[rows], qn, cos, sin) * Q_SCALE).astype(_BF16)
    qb_ref[0] = (qk[WIDTH:2 * WIDTH] * Q_SCALE).astype(_BF16)
    ka = qk[2 * WIDTH:2 * WIDTH + KV_WIDTH]
    ka = jnp.concatenate(
        [_norm_rope(ka[hd * HEAD_DIM:(hd + 1) * HEAD_DIM], kn, cos, sin) for hd in range(N_KV)],
        axis=0)
    ka_ref[0] = ka.T.astype(_BF16)
    kb_ref[0] = qk[2 * WIDTH + KV_WIDTH:].T.astype(_BF16)

    gates = _silu(proj(_QK_ROWS, _GATE_ROWS))
    ga_ref[0] = gates[:WIDTH].astype(_BF16)
    gb_ref[0] = gates[WIDTH:].astype(_BF16)

    v = proj(_QK_ROWS + _GATE_ROWS, _V_ROWS)
    va_ref[0] = v[:KV_WIDTH].astype(_BF16)
    vb_ref[0] = v[KV_WIDTH:].astype(_BF16)


def _rope_tables(seq):
    rows = seq // GRID_W
    row = jnp.repeat(jnp.arange(rows), GRID_W).astype(_F32)
    col = jnp.tile(jnp.arange(GRID_W), rows).astype(_F32)
    half = HEAD_DIM // 2
    freqs = ROPE_THETA ** (-jnp.arange(0, half, 2, dtype=_F32) / half)

    def ang2(pos):
        ang = pos[:, None] * freqs[None, :]
        return jnp.concatenate([ang, ang], axis=-1)

    ang = jnp.concatenate([ang2(row), ang2(col)], axis=-1)
    return jnp.cos(ang).T, jnp.sin(ang).T


def _in_proj(x, shift, scale, g_pre, w_in_t, cos_t, sin_t, qn, kn):
    bsz, seq, d = x.shape
    tm = _TM_IN
    tok_t = lambda n: pl.BlockSpec((1, n, tm), lambda b, i: (b, 0, i))
    tok_m = lambda n: pl.BlockSpec((1, tm, n), lambda b, i: (b, i, 0))
    per_b = pl.BlockSpec((1, 1, d), lambda b, i: (b, 0, 0))
    tab = pl.BlockSpec((HEAD_DIM, tm), lambda b, i: (0, i))
    gain = pl.BlockSpec((HEAD_DIM, tm), lambda b, i: (0, 0))
    shp_t = lambda n: jax.ShapeDtypeStruct((bsz, n, seq), _BF16)
    shp_m = lambda n: jax.ShapeDtypeStruct((bsz, seq, n), _BF16)
    return pl.pallas_call(
        _inproj_kernel,
        out_shape=(shp_t(WIDTH), shp_m(KV_WIDTH), shp_t(KV_WIDTH), shp_t(WIDTH),
                   shp_t(WIDTH), shp_m(KV_WIDTH), shp_t(KV_WIDTH), shp_t(WIDTH)),
        grid=(bsz, seq // tm),
        in_specs=[pl.BlockSpec((1, tm, d), lambda b, i: (b, i, 0)),
                  per_b, per_b,
                  pl.BlockSpec((1, d), lambda b, i: (0, 0)),
                  pl.BlockSpec((IN_COLS, d), lambda b, i: (0, 0)),
                  tab, tab, gain, gain],
        out_specs=(tok_t(WIDTH), tok_m(KV_WIDTH), tok_t(KV_WIDTH), tok_t(WIDTH),
                   tok_t(WIDTH), tok_m(KV_WIDTH), tok_t(KV_WIDTH), tok_t(WIDTH)),
        compiler_params=_params(2),
        name="in_proj",
    )(x, shift, scale, g_pre.reshape(1, d), w_in_t, cos_t, sin_t,
      jnp.broadcast_to(qn[:, None], (HEAD_DIM, tm)),
      jnp.broadcast_to(kn[:, None], (HEAD_DIM, tm)))


_PAIRS = N_HEADS // 2
_ONES_ROWS = 16


def _head_rows(hd):
    return slice(hd * HEAD_DIM, (hd + 1) * HEAD_DIM)


def _pair_q(q_ref, pair):
    def padded(hd):
        qh = q_ref[0, _head_rows(hd), :]
        z = jnp.zeros_like(qh)
        return jnp.concatenate([qh, z] if hd // GROUP == 0 else [z, qh], axis=0)
    return jnp.concatenate([padded(2 * pair), padded(2 * pair + 1)], axis=1)


def _pair_out(o, inv_l, g_ref, pair, tq):
    o = o * inv_l
    g0 = g_ref[0, _head_rows(2 * pair), :].astype(_F32)
    g1 = g_ref[0, _head_rows(2 * pair + 1), :].astype(_F32)
    return jnp.concatenate([o[:, :tq] * g0, o[:, tq:] * g1], axis=0).T.astype(_BF16)


def _v_with_ones(v):
    return jnp.concatenate([v, jnp.ones((_ONES_ROWS, v.shape[1]), v.dtype)], axis=0)


def _global_attn_kernel(q_ref, k_ref, v_ref, g_ref, o_ref, s0_ref, s1_ref):
    tq = q_ref.shape[2]
    seq = k_ref.shape[1]
    s_refs = (s0_ref, s1_ref)
    chunks = [slice(c * _TK_A, (c + 1) * _TK_A) for c in range(seq // _TK_A)]
    zero = jnp.minimum(pl.program_id(0), 0)

    def dyn(rows):
        return pl.ds(pl.multiple_of(zero + rows.start, _TK_A), _TK_A)

    def scores(pair, rows):
        s = jnp.dot(k_ref[0, rows, :], _pair_q(q_ref, pair), preferred_element_type=_F32)
        s_refs[pair % 2][dyn(rows), :] = s
        return jnp.max(s, axis=0, keepdims=True)

    def col_max(parts):
        return functools.reduce(jnp.maximum, parts)

    m = col_max([scores(0, rows) for rows in chunks])
    for pair in range(_PAIRS):
        nxt, o = [], None
        for rows in chunks:
            if pair + 1 < _PAIRS:
                nxt.append(scores(pair + 1, rows))
            e = jnp.exp2(s_refs[pair % 2][dyn(rows), :] - m)
            v = _v_with_ones(v_ref[0, _head_rows(pair // (GROUP // 2)), rows])
            pv = jnp.dot(v, e.astype(_BF16), preferred_element_type=_F32)
            o = pv if o is None else o + pv
        o_ref[0, :, pair * 2 * HEAD_DIM:(pair + 1) * 2 * HEAD_DIM] = _pair_out(
            o[:HEAD_DIM], 1.0 / o[HEAD_DIM:HEAD_DIM + 1], g_ref, pair, tq)
        if nxt:
            m = col_max(nxt)


def _global_attn(q_t, k, v_t, g_t):
    bsz, _, seq = q_t.shape
    tq = _TQ_A
    return pl.pallas_call(
        _global_attn_kernel,
        out_shape=jax.ShapeDtypeStruct((bsz, seq, WIDTH), _BF16),
        grid=(bsz, seq // tq),
        in_specs=[pl.BlockSpec((1, WIDTH, tq), lambda b, i: (b, 0, i)),
                  pl.BlockSpec((1, seq, KV_WIDTH), lambda b, i: (b, 0, 0)),
                  pl.BlockSpec((1, KV_WIDTH, seq), lambda b, i: (b, 0, 0)),
                  pl.BlockSpec((1, WIDTH, tq), lambda b, i: (b, 0, i))],
        out_specs=pl.BlockSpec((1, tq, WIDTH), lambda b, i: (b, i, 0)),
        scratch_shapes=[pltpu.VMEM((seq, 2 * tq), _F32), pltpu.VMEM((seq, 2 * tq), _F32)],
        compiler_params=_params(2),
        name="global_attn",
    )(q_t, k, v_t, g_t)


def _window_attn_kernel(sink_ref, q_ref, k0_ref, k1_ref, k2_ref, k3_ref,
                        v0_ref, v1_ref, v2_ref, v3_ref, g_ref, bias_ref, o_ref, s0_ref, s1_ref):
    tq = q_ref.shape[2]
    t = pl.program_id(1)
    last = pl.num_programs(1) - 1
    k = jnp.concatenate([k0_ref[0], k1_ref[0], k2_ref[0], k3_ref[0]], axis=0)
    v = jnp.concatenate([v0_ref[0], v1_ref[0], v2_ref[0], v3_ref[0]], axis=1)
    s_refs = (s0_ref, s1_ref)
    all_keys = pl.ds(pl.multiple_of(jnp.minimum(pl.program_id(0), 0), _KEYS_B), _KEYS_B)

    def scores(pair):
        bias = jnp.concatenate([bias_ref[2 * pair], bias_ref[2 * pair + 1]], axis=1)
        s = jnp.dot(k, _pair_q(q_ref, pair), preferred_element_type=_F32) + bias
        s = jnp.concatenate([jnp.where(t == 0, MASK_VALUE, s[:Q_BLOCK]),
                             s[Q_BLOCK:_KEYS_B - Q_BLOCK],
                             jnp.where(t == last, MASK_VALUE, s[_KEYS_B - Q_BLOCK:])], axis=0)
        s_refs[pair % 2][all_keys, :] = s
        return jnp.max(s, axis=0, keepdims=True)

    col_max = scores(0)
    for pair in range(_PAIRS):
        nxt = scores(pair + 1) if pair + 1 < _PAIRS else None
        sink = sink_ref[pair] * LOG2E
        m = jnp.maximum(col_max, sink)
        e = jnp.exp2(s_refs[pair % 2][all_keys, :] - m)
        o = jnp.dot(_v_with_ones(v[_head_rows(pair // (GROUP // 2)), :]), e.astype(_BF16),
                    preferred_element_type=_F32)
        l = o[HEAD_DIM:HEAD_DIM + 1] + jnp.exp2(sink - m)
        o_ref[0, :, pair * 2 * HEAD_DIM:(pair + 1) * 2 * HEAD_DIM] = _pair_out(
            o[:HEAD_DIM], 1.0 / l, g_ref, pair, tq)
        col_max = nxt


def _window_attn(sink, q_t, k, v_t, g_t, bias):
    bsz, _, seq = q_t.shape
    tq = _TQ_B
    nblk = seq // Q_BLOCK
    per_tile = tq // Q_BLOCK

    def blk(j):
        return lambda i: jnp.clip(per_tile * i - 1 + j, 0, nblk - 1)

    k_spec = lambda j: pl.BlockSpec((1, Q_BLOCK, KV_WIDTH), lambda b, i: (b, blk(j)(i), 0))
    v_spec = lambda j: pl.BlockSpec((1, KV_WIDTH, Q_BLOCK), lambda b, i: (b, 0, blk(j)(i)))
    q_spec = pl.BlockSpec((1, WIDTH, tq), lambda b, i: (b, 0, i))
    return pl.pallas_call(
        _window_attn_kernel,
        out_shape=jax.ShapeDtypeStruct((bsz, seq, WIDTH), _BF16),
        grid=(bsz, seq // tq),
        in_specs=[pl.BlockSpec((_PAIRS, 1, 2 * tq), lambda b, i: (0, 0, 0)), q_spec,
                  k_spec(0), k_spec(1), k_spec(2), k_spec(3),
                  v_spec(0), v_spec(1), v_spec(2), v_spec(3),
                  q_spec,
                  pl.BlockSpec((N_HEADS, _KEYS_B, tq), lambda b, i: (0, 0, 0))],
        out_specs=pl.BlockSpec((1, tq, WIDTH), lambda b, i: (b, i, 0)),
        scratch_shapes=[pltpu.VMEM((_KEYS_B, 2 * tq), _F32), pltpu.VMEM((_KEYS_B, 2 * tq), _F32)],
        compiler_params=_params(2),
        name="window_attn",
    )(jnp.broadcast_to(sink[:, None], (N_HEADS, tq)).reshape(_PAIRS, 1, 2 * tq),
      q_t, k, k, k, k, v_t, v_t, v_t, v_t, g_t, bias)


def _outproj_kernel(oa_ref, ob_ref, w_ref, x_ref, gate_ref, gpost_ref, o_ref):
    y = (jnp.dot(oa_ref[0], w_ref[0:WIDTH, :], preferred_element_type=_F32)
         + jnp.dot(ob_ref[0], w_ref[WIDTH:2 * WIDTH, :], preferred_element_type=_F32))
    ms = jnp.mean(y * y, axis=-1, keepdims=True)
    yn = y * lax.rsqrt(ms + EPS) * gpost_ref[...]
    o_ref[0] = x_ref[0] + gate_ref[0] * yn


def _out_proj(oa, ob, w_out, x, gate, g_post):
    bsz, seq, d = x.shape
    tm = _TM_OUT
    act = pl.BlockSpec((1, tm, WIDTH), lambda b, i: (b, i, 0))
    xo = pl.BlockSpec((1, tm, d), lambda b, i: (b, i, 0))
    return pl.pallas_call(
        _outproj_kernel,
        out_shape=jax.ShapeDtypeStruct((bsz, seq, d), _F32),
        grid=(bsz, seq // tm),
        in_specs=[act, act,
                  pl.BlockSpec((2 * WIDTH, d), lambda b, i: (0, 0)),
                  xo,
                  pl.BlockSpec((1, 1, d), lambda b, i: (b, 0, 0)),
                  pl.BlockSpec((1, d), lambda b, i: (0, 0))],
        out_specs=xo,
        compiler_params=_params(2),
        name="out_proj",
    )(oa, ob, w_out, x, gate, g_post.reshape(1, d))


def kernel(x, c, w_ada, b_ada, g_pre, g_post, w_in, qn_a, kn_a, sink_b, w_out, rel_table):
    bsz, seq, d = x.shape
    depth = w_ada.shape[0]
    cos_t, sin_t = _rope_tables(seq)
    bias = _t5_bias_tiles(rel_table)
    for l in range(depth):
        mod = _adaln_mod(c, w_ada[l], b_ada[l])
        shift, scale, gate = (mod[:, i * d:(i + 1) * d].reshape(bsz, 1, d) for i in range(3))
        w_in_t = jnp.concatenate([w_in[l][:, lo:lo + n] for lo, n in _W_ROW_ORDER],
                                 axis=1).T.astype(_BF16)
        qa, ka, va, ga, qb, kb, vb, gb = _in_proj(
            x, shift, scale, g_pre[l], w_in_t, cos_t, sin_t, qn_a[l], kn_a[l])
        oa = _global_attn(qa, ka, va, ga)
        ob = _window_attn(sink_b[l], qb, kb, vb, gb, bias)
        x = _out_proj(oa, ob, w_out[l].astype(_BF16), x, gate, g_post[l])
    return x
```

```python
import functools

import jax
import jax.numpy as jnp
import numpy as np
from jax import lax
from jax.experimental import pallas as pl
from jax.experimental.pallas import tpu as pltpu

D_MODEL = 1024
HEAD_DIM = 64
N_HEADS = 8
N_KV = 2
GROUP = N_HEADS // N_KV
WIDTH = N_HEADS * HEAD_DIM
KV_WIDTH = N_KV * HEAD_DIM
IN_COLS = 2 * (2 * WIDTH + 2 * KV_WIDTH)
Q_BLOCK = 128
WINDOW = 128
GRID_W = 64
ROPE_THETA = 10000.0
N_BUCKETS = 32
MAX_DISTANCE = 128
EPS = 1e-6
MASK_VALUE = -1e30
LOG2E = 1.4426950408889634
Q_SCALE = HEAD_DIM ** -0.5 * LOG2E

_QA, _KA, _VA, _GA = 0, WIDTH, WIDTH + KV_WIDTH, WIDTH + 2 * KV_WIDTH
_HALF = 2 * WIDTH + 2 * KV_WIDTH
_QB, _KB, _VB, _GB = _HALF + _QA, _HALF + _KA, _HALF + _VA, _HALF + _GA
_W_ROW_ORDER = ((_QA, WIDTH), (_QB, WIDTH), (_KA, KV_WIDTH), (_KB, KV_WIDTH),
                (_GA, WIDTH), (_GB, WIDTH), (_VA, KV_WIDTH), (_VB, KV_WIDTH))
_QK_ROWS = 2 * WIDTH + 2 * KV_WIDTH
_GATE_ROWS = 2 * WIDTH
_V_ROWS = 2 * KV_WIDTH

_F32 = jnp.float32
_BF16 = jnp.bfloat16

_VMEM_LIMIT_BYTES = 48 * 1024 * 1024

_TM_IN = 512
_TQ_A = 256
_TK_A = 512
_TQ_B = 256
_KEYS_B = _TQ_B + 2 * WINDOW
_TM_OUT = 1024
_TN_MOD = 512

_NT = (((1,), (1,)), ((), ()))


def _silu(v):
    return 0.5 * v * (1.0 + jnp.tanh(0.5 * v))


def _params(n_grid, flags=None):
    return pltpu.CompilerParams(
        dimension_semantics=("arbitrary",) * n_grid,
        vmem_limit_bytes=_VMEM_LIMIT_BYTES, flags=flags)


def _mod_kernel(c_ref, w_ref, b_ref, o_ref):
    ca = _silu(c_ref[...])
    o_ref[...] = jnp.dot(ca.astype(_BF16), w_ref[...].astype(_BF16),
                         preferred_element_type=_F32) + b_ref[...]


def _adaln_mod(c, w, b):
    bsz, d = c.shape
    n = w.shape[1]
    return pl.pallas_call(
        _mod_kernel,
        out_shape=jax.ShapeDtypeStruct((bsz, n), _F32),
        grid=(n // _TN_MOD,),
        in_specs=[pl.BlockSpec((bsz, d), lambda j: (0, 0)),
                  pl.BlockSpec((d, _TN_MOD), lambda j: (0, j)),
                  pl.BlockSpec((1, _TN_MOD), lambda j: (0, j))],
        out_specs=pl.BlockSpec((bsz, _TN_MOD), lambda j: (0, j)),
        compiler_params=_params(1),
        name="adaln_mod",
    )(c, w, b.reshape(1, n))


def _bias_kernel(table_ref, bucket_ref, o_ref):
    masked = jnp.full((Q_BLOCK, Q_BLOCK), MASK_VALUE, _F32)
    for h in range(N_HEADS):
        table = table_ref[h] * LOG2E
        tiles = []
        for d in range(3):
            bucket = bucket_ref[d]
            acc = masked
            for b in range(N_BUCKETS):
                acc = jnp.where(bucket == b, table[b:b + 1, :], acc)
            tiles.append(acc)
        lo, mid, hi = tiles
        for jb, (left, right) in enumerate(((lo, masked), (mid, lo), (hi, mid), (masked, hi))):
            rows = slice(jb * Q_BLOCK, (jb + 1) * Q_BLOCK)
            o_ref[h, rows, 0:Q_BLOCK] = left
            o_ref[h, rows, Q_BLOCK:2 * Q_BLOCK] = right


def _t5_bucket(rel):
    nb = N_BUCKETS // 2
    max_exact = nb // 2
    ret = (rel > 0).astype(jnp.int32) * nb
    n = jnp.abs(rel)
    assert MAX_DISTANCE == 2 * max_exact ** 2 and nb - max_exact == 8
    n2 = n * n
    large = max_exact + sum((n2 >= 2 ** k).astype(jnp.int32) for k in range(7, 31))
    large = jnp.minimum(large, nb - 1)
    return ret + jnp.where(n < max_exact, n, large)


def _t5_bias_tiles(rel_table):
    r = jnp.arange(Q_BLOCK)
    rel = ((jnp.arange(3) - 1)[:, None, None] * Q_BLOCK
           + r[None, :, None] - r[None, None, :])
    bucket = jnp.where(jnp.abs(rel) <= WINDOW, _t5_bucket(rel), -1).astype(jnp.int32)
    return pl.pallas_call(
        _bias_kernel,
        out_shape=jax.ShapeDtypeStruct((N_HEADS, _KEYS_B, _TQ_B), _F32),
        in_specs=[pl.BlockSpec(memory_space=pltpu.VMEM),
                  pl.BlockSpec(memory_space=pltpu.VMEM)],
        out_specs=pl.BlockSpec(memory_space=pltpu.VMEM),
        compiler_params=pltpu.CompilerParams(vmem_limit_bytes=_VMEM_LIMIT_BYTES),
        name="t5_bias",
    )(jnp.broadcast_to(rel_table.T[:, :, None], (N_HEADS, N_BUCKETS, Q_BLOCK)), bucket)


def _norm_rope(xh, gain, cos, sin):
    ms = jnp.mean(xh * xh, axis=0, keepdims=True)
    y = xh * lax.rsqrt(ms + EPS) * gain
    q = HEAD_DIM // 4
    rot = jnp.concatenate([-y[q:2 * q], y[0:q], -y[3 * q:4 * q], y[2 * q:3 * q]], axis=0)
    return y * cos + rot * sin


def _inproj_kernel(x_ref, shift_ref, scale_ref, gpre_ref, w_ref, cos_ref, sin_ref,
                   qn_ref, kn_ref,
                   qa_ref, ka_ref, va_ref, ga_ref, qb_ref, kb_ref, vb_ref, gb_ref):
    x = x_ref[0]
    ms = jnp.mean(x * x, axis=-1, keepdims=True)
    h = x * lax.rsqrt(ms + EPS) * gpre_ref[...]
    h = h * (1.0 + scale_ref[0]) + shift_ref[0]
    hb = h.astype(_BF16)

    def proj(lo, n):
        return lax.dot_general(w_ref[lo:lo + n, :], hb, _NT, preferred_element_type=_F32)

    cos = cos_ref[...]
    sin = sin_ref[...]
    qn = qn_ref[...]
    kn = kn_ref[...]

    def store_tiled(ref, rows, val):
        tq = ref.shape[3]
        for j in range(ref.shape[1]):
            ref[0, j, rows, :] = val[:, j * tq:(j + 1) * tq]

    qk = proj(0, _QK_ROWS)
    for hd in range(N_HEADS):
        rows = slice(hd * HEAD_DIM, (hd + 1) * HEAD_DIM)
        store_tiled(qa_ref, rows, (_norm_rope(qk[rows], qn, cos, sin) * Q_SCALE).astype(_BF16))
    qb_ref[0] = (qk[WIDTH:2 * WIDTH] * Q_SCALE).astype(_BF16)
    ka = qk[2 * WIDTH:2 * WIDTH + KV_WIDTH]
    ka = jnp.concatenate(
        [_norm_rope(ka[hd * HEAD_DIM:(hd + 1) * HEAD_DIM], kn, cos, sin) for hd in range(N_KV)],
        axis=0)
    ka_ref[0] = ka.T.astype(_BF16)
    kb_ref[0] = qk[2 * WIDTH + KV_WIDTH:].T.astype(_BF16)

    gates = _silu(proj(_QK_ROWS, _GATE_ROWS))
    store_tiled(ga_ref, slice(0, WIDTH), gates[:WIDTH].astype(_BF16))
    gb_ref[0] = gates[WIDTH:].astype(_BF16)

    v = proj(_QK_ROWS + _GATE_ROWS, _V_ROWS)
    va_ref[0] = v[:KV_WIDTH].astype(_BF16)
    vb_ref[0] = v[KV_WIDTH:].astype(_BF16)


def _rope_tables(seq):
    rows = seq // GRID_W
    row = jnp.repeat(jnp.arange(rows), GRID_W).astype(_F32)
    col = jnp.tile(jnp.arange(GRID_W), rows).astype(_F32)
    half = HEAD_DIM // 2
    freqs = ROPE_THETA ** (-jnp.arange(0, half, 2, dtype=_F32) / half)

    def ang2(pos):
        ang = pos[:, None] * freqs[None, :]
        return jnp.concatenate([ang, ang], axis=-1)

    ang = jnp.concatenate([ang2(row), ang2(col)], axis=-1)
    return jnp.cos(ang).T, jnp.sin(ang).T


def _in_proj(x, shift, scale, g_pre, w_in_t, cos_t, sin_t, qn, kn):
    bsz, seq, d = x.shape
    tm = _TM_IN
    tok_t = lambda n: pl.BlockSpec((1, n, tm), lambda b, i: (b, 0, i))
    tok_m = lambda n: pl.BlockSpec((1, tm, n), lambda b, i: (b, i, 0))
    per_b = pl.BlockSpec((1, 1, d), lambda b, i: (b, 0, 0))
    tab = pl.BlockSpec((HEAD_DIM, tm), lambda b, i: (0, i))
    gain = pl.BlockSpec((HEAD_DIM, tm), lambda b, i: (0, 0))
    shp_t = lambda n: jax.ShapeDtypeStruct((bsz, n, seq), _BF16)
    shp_m = lambda n: jax.ShapeDtypeStruct((bsz, seq, n), _BF16)
    tiled = pl.BlockSpec((1, tm // _TQ_A, WIDTH, _TQ_A), lambda b, i: (b, i, 0, 0))
    shp_tiled = jax.ShapeDtypeStruct((bsz, seq // _TQ_A, WIDTH, _TQ_A), _BF16)
    return pl.pallas_call(
        _inproj_kernel,
        out_shape=(shp_tiled, shp_m(KV_WIDTH), shp_t(KV_WIDTH), shp_tiled,
                   shp_t(WIDTH), shp_m(KV_WIDTH), shp_t(KV_WIDTH), shp_t(WIDTH)),
        grid=(bsz, seq // tm),
        in_specs=[pl.BlockSpec((1, tm, d), lambda b, i: (b, i, 0)),
                  per_b, per_b,
                  pl.BlockSpec((1, d), lambda b, i: (0, 0)),
                  pl.BlockSpec((IN_COLS, d), lambda b, i: (0, 0)),
                  tab, tab, gain, gain],
        out_specs=(tiled, tok_m(KV_WIDTH), tok_t(KV_WIDTH), tiled,
                   tok_t(WIDTH), tok_m(KV_WIDTH), tok_t(KV_WIDTH), tok_t(WIDTH)),
        compiler_params=_params(2),
        name="in_proj",
    )(x, shift, scale, g_pre.reshape(1, d), w_in_t, cos_t, sin_t,
      jnp.broadcast_to(qn[:, None], (HEAD_DIM, tm)),
      jnp.broadcast_to(kn[:, None], (HEAD_DIM, tm)))


_PAIRS = N_HEADS // 2
_ONES_ROWS = 16


def _head_rows(hd):
    return slice(hd * HEAD_DIM, (hd + 1) * HEAD_DIM)


def _pair_q(head, pair):
    def padded(hd):
        qh = head(hd)
        z = jnp.zeros_like(qh)
        return jnp.concatenate([qh, z] if hd // GROUP == 0 else [z, qh], axis=0)
    return jnp.concatenate([padded(2 * pair), padded(2 * pair + 1)], axis=1)


def _pair_out(o, inv_l, gate, pair, tq):
    o = o * inv_l
    g0 = gate(2 * pair).astype(_F32)
    g1 = gate(2 * pair + 1).astype(_F32)
    return jnp.concatenate([o[:, :tq] * g0, o[:, tq:] * g1], axis=0).T.astype(_BF16)


def _pair_lanes(pair):
    return slice(pair * 2 * HEAD_DIM, (pair + 1) * 2 * HEAD_DIM)


def _v_with_ones(v):
    return jnp.concatenate([v, jnp.ones((_ONES_ROWS, v.shape[1]), v.dtype)], axis=0)


def _global_attn_kernel(q_ref, k_ref, v_ref, g_ref, o_ref, s0_ref, s1_ref):
    n_tiles, tq = q_ref.shape[1], q_ref.shape[3]
    seq = k_ref.shape[1]
    s_refs = (s0_ref, s1_ref)
    chunks = [slice(c * _TK_A, (c + 1) * _TK_A) for c in range(seq // _TK_A)]

    def staged(rows, zero):
        return pl.ds(pl.multiple_of(zero + rows.start, _TK_A), _TK_A)

    def scores(tile, pair, rows, zero):
        s = jnp.dot(k_ref[0, rows, :], _pair_q(lambda hd: q_ref[0, tile, _head_rows(hd), :], pair),
                    preferred_element_type=_F32)
        s_refs[pair % 2][staged(rows, zero), :] = s
        return jnp.max(s, axis=0, keepdims=True)

    def col_max(parts):
        return functools.reduce(jnp.maximum, parts)

    def tile_body(tile, m):
        zero = jnp.minimum(tile, 0)
        next_tile = jnp.minimum(tile + 1, n_tiles - 1)
        for pair in range(_PAIRS):
            nxt, o = [], None
            for rows in chunks:
                if pair + 1 < _PAIRS:
                    nxt.append(scores(tile, pair + 1, rows, zero))
                else:
                    nxt.append(scores(next_tile, 0, rows, zero))
                e = jnp.exp2(s_refs[pair % 2][staged(rows, zero), :] - m)
                v = _v_with_ones(v_ref[0, _head_rows(pair // (GROUP // 2)), rows])
                pv = jnp.dot(v, e.astype(_BF16), preferred_element_type=_F32)
                o = pv if o is None else o + pv
            o_ref[0, pl.ds(pl.multiple_of(tile * tq, tq), tq), _pair_lanes(pair)] = _pair_out(
                o[:HEAD_DIM], 1.0 / o[HEAD_DIM:HEAD_DIM + 1],
                lambda hd: g_ref[0, tile, _head_rows(hd), :], pair, tq)
            m = col_max(nxt)
        return m

    first = jnp.minimum(pl.program_id(0), 0)
    m0 = col_max([scores(first, 0, rows, first) for rows in chunks])
    lax.fori_loop(0, n_tiles, tile_body, m0)


def _global_attn(q_t, k, v_t, g_t):
    bsz, n_tiles, _, tq = q_t.shape
    seq = k.shape[1]
    tiled = pl.BlockSpec((1, n_tiles, WIDTH, tq), lambda b: (b, 0, 0, 0))
    return pl.pallas_call(
        _global_attn_kernel,
        out_shape=jax.ShapeDtypeStruct((bsz, seq, WIDTH), _BF16),
        grid=(bsz,),
        in_specs=[tiled,
                  pl.BlockSpec((1, seq, KV_WIDTH), lambda b: (b, 0, 0)),
                  pl.BlockSpec((1, KV_WIDTH, seq), lambda b: (b, 0, 0)),
                  tiled],
        out_specs=pl.BlockSpec((1, seq, WIDTH), lambda b: (b, 0, 0)),
        scratch_shapes=[pltpu.VMEM((seq, 2 * tq), _F32), pltpu.VMEM((seq, 2 * tq), _F32)],
        compiler_params=_params(1),
        name="global_attn",
    )(q_t, k, v_t, g_t)


def _window_attn_kernel(sink_ref, q_ref, k0_ref, k1_ref, k2_ref, k3_ref,
                        v0_ref, v1_ref, v2_ref, v3_ref, g_ref, bias_ref, o_ref, s0_ref, s1_ref):
    tq = q_ref.shape[2]
    t = pl.program_id(1)
    last = pl.num_programs(1) - 1
    k = jnp.concatenate([k0_ref[0], k1_ref[0], k2_ref[0], k3_ref[0]], axis=0)
    v = jnp.concatenate([v0_ref[0], v1_ref[0], v2_ref[0], v3_ref[0]], axis=1)
    s_refs = (s0_ref, s1_ref)
    all_keys = pl.ds(pl.multiple_of(jnp.minimum(pl.program_id(0), 0), _KEYS_B), _KEYS_B)

    def scores(pair):
        bias = jnp.concatenate([bias_ref[2 * pair], bias_ref[2 * pair + 1]], axis=1)
        s = jnp.dot(k, _pair_q(lambda hd: q_ref[0, _head_rows(hd), :], pair),
                    preferred_element_type=_F32) + bias
        s = jnp.concatenate([jnp.where(t == 0, MASK_VALUE, s[:Q_BLOCK]),
                             s[Q_BLOCK:_KEYS_B - Q_BLOCK],
                             jnp.where(t == last, MASK_VALUE, s[_KEYS_B - Q_BLOCK:])], axis=0)
        s_refs[pair % 2][all_keys, :] = s
        return jnp.max(s, axis=0, keepdims=True)

    col_max = scores(0)
    for pair in range(_PAIRS):
        nxt = scores(pair + 1) if pair + 1 < _PAIRS else None
        sink = sink_ref[pair] * LOG2E
        m = jnp.maximum(col_max, sink)
        e = jnp.exp2(s_refs[pair % 2][all_keys, :] - m)
        o = jnp.dot(_v_with_ones(v[_head_rows(pair // (GROUP // 2)), :]), e.astype(_BF16),
                    preferred_element_type=_F32)
        l = o[HEAD_DIM:HEAD_DIM + 1] + jnp.exp2(sink - m)
        o_ref[0, :, _pair_lanes(pair)] = _pair_out(
            o[:HEAD_DIM], 1.0 / l, lambda hd: g_ref[0, _head_rows(hd), :], pair, tq)
        col_max = nxt


def _window_attn(sink, q_t, k, v_t, g_t, bias):
    bsz, _, seq = q_t.shape
    tq = _TQ_B
    nblk = seq // Q_BLOCK
    per_tile = tq // Q_BLOCK

    def blk(j):
        return lambda i: jnp.clip(per_tile * i - 1 + j, 0, nblk - 1)

    k_spec = lambda j: pl.BlockSpec((1, Q_BLOCK, KV_WIDTH), lambda b, i: (b, blk(j)(i), 0))
    v_spec = lambda j: pl.BlockSpec((1, KV_WIDTH, Q_BLOCK), lambda b, i: (b, 0, blk(j)(i)))
    q_spec = pl.BlockSpec((1, WIDTH, tq), lambda b, i: (b, 0, i))
    return pl.pallas_call(
        _window_attn_kernel,
        out_shape=jax.ShapeDtypeStruct((bsz, seq, WIDTH), _BF16),
        grid=(bsz, seq // tq),
        in_specs=[pl.BlockSpec((_PAIRS, 1, 2 * tq), lambda b, i: (0, 0, 0)), q_spec,
                  k_spec(0), k_spec(1), k_spec(2), k_spec(3),
                  v_spec(0), v_spec(1), v_spec(2), v_spec(3),
                  q_spec,
                  pl.BlockSpec((N_HEADS, _KEYS_B, tq), lambda b, i: (0, 0, 0))],
        out_specs=pl.BlockSpec((1, tq, WIDTH), lambda b, i: (b, i, 0)),
        scratch_shapes=[pltpu.VMEM((_KEYS_B, 2 * tq), _F32), pltpu.VMEM((_KEYS_B, 2 * tq), _F32)],
        compiler_params=_params(2),
        name="window_attn",
    )(jnp.broadcast_to(sink[:, None], (N_HEADS, tq)).reshape(_PAIRS, 1, 2 * tq),
      q_t, k, k, k, k, v_t, v_t, v_t, v_t, g_t, bias)


def _outproj_kernel(oa_ref, ob_ref, w_ref, x_ref, gate_ref, gpost_ref, o_ref):
    y = (jnp.dot(oa_ref[0], w_ref[0:WIDTH, :], preferred_element_type=_F32)
         + jnp.dot(ob_ref[0], w_ref[WIDTH:2 * WIDTH, :], preferred_element_type=_F32))
    ms = jnp.mean(y * y, axis=-1, keepdims=True)
    yn = y * lax.rsqrt(ms + EPS) * gpost_ref[...]
    o_ref[0] = x_ref[0] + gate_ref[0] * yn


def _out_proj(oa, ob, w_out, x, gate, g_post):
    bsz, seq, d = x.shape
    tm = _TM_OUT
    act = pl.BlockSpec((1, tm, WIDTH), lambda b, i: (b, i, 0))
    xo = pl.BlockSpec((1, tm, d), lambda b, i: (b, i, 0))
    return pl.pallas_call(
        _outproj_kernel,
        out_shape=jax.ShapeDtypeStruct((bsz, seq, d), _F32),
        grid=(bsz, seq // tm),
        in_specs=[act, act,
                  pl.BlockSpec((2 * WIDTH, d), lambda b, i: (0, 0)),
                  xo,
                  pl.BlockSpec((1, 1, d), lambda b, i: (b, 0, 0)),
                  pl.BlockSpec((1, d), lambda b, i: (0, 0))],
        out_specs=xo,
        compiler_params=_params(2),
        name="out_proj",
    )(oa, ob, w_out, x, gate, g_post.reshape(1, d))


def kernel(x, c, w_ada, b_ada, g_pre, g_post, w_in, qn_a, kn_a, sink_b, w_out, rel_table):
    bsz, seq, d = x.shape
    depth = w_ada.shape[0]
    cos_t, sin_t = _rope_tables(seq)
    bias = _t5_bias_tiles(rel_table)
    for l in range(depth):
        mod = _adaln_mod(c, w_ada[l], b_ada[l])
        shift, scale, gate = (mod[:, i * d:(i + 1) * d].reshape(bsz, 1, d) for i in range(3))
        w_in_t = jnp.concatenate([w_in[l][:, lo:lo + n] for lo, n in _W_ROW_ORDER],
                                 axis=1).T.astype(_BF16)
        qa, ka, va, ga, qb, kb, vb, gb = _in_proj(
            x, shift, scale, g_pre[l], w_in_t, cos_t, sin_t, qn_a[l], kn_a[l])
        oa = _global_attn(qa, ka, va, ga)
        ob = _window_attn(sink_b[l], qb, kb, vb, gb, bias)
        x = _out_proj(oa, ob, w_out[l].astype(_BF16), x, gate, g_post[l])
    return x
```

```python
import functools

import jax
import jax.numpy as jnp
import numpy as np
from jax import lax
from jax.experimental import pallas as pl
from jax.experimental.pallas import tpu as pltpu

D_MODEL = 1024
HEAD_DIM = 64
N_HEADS = 8
N_KV = 2
GROUP = N_HEADS // N_KV
WIDTH = N_HEADS * HEAD_DIM
KV_WIDTH = N_KV * HEAD_DIM
IN_COLS = 2 * (2 * WIDTH + 2 * KV_WIDTH)
Q_BLOCK = 128
WINDOW = 128
GRID_W = 64
ROPE_THETA = 10000.0
N_BUCKETS = 32
MAX_DISTANCE = 128
EPS = 1e-6
MASK_VALUE = -1e30
LOG2E = 1.4426950408889634
Q_SCALE = HEAD_DIM ** -0.5 * LOG2E

_QA, _KA, _VA, _GA = 0, WIDTH, WIDTH + KV_WIDTH, WIDTH + 2 * KV_WIDTH
_HALF = 2 * WIDTH + 2 * KV_WIDTH
_QB, _KB, _VB, _GB = _HALF + _QA, _HALF + _KA, _HALF + _VA, _HALF + _GA
_W_ROW_ORDER = ((_QA, WIDTH), (_QB, WIDTH), (_KA, KV_WIDTH), (_KB, KV_WIDTH),
                (_GA, WIDTH), (_GB, WIDTH), (_VA, KV_WIDTH), (_VB, KV_WIDTH))
_QK_ROWS = 2 * WIDTH + 2 * KV_WIDTH
_GATE_ROWS = 2 * WIDTH
_V_ROWS = 2 * KV_WIDTH

_F32 = jnp.float32
_BF16 = jnp.bfloat16

_VMEM_LIMIT_BYTES = 48 * 1024 * 1024

_TM_IN = 512
_TQ_A = 256
_TK_A = 512
_TQ_B = 256
_KEYS_B = _TQ_B + 2 * WINDOW
_TM_OUT = 1024
_TN_MOD = 512

_NT = (((1,), (1,)), ((), ()))


def _silu(v):
    return 0.5 * v * (1.0 + jnp.tanh(0.5 * v))


def _params(n_grid, flags=None):
    return pltpu.CompilerParams(
        dimension_semantics=("arbitrary",) * n_grid,
        vmem_limit_bytes=_VMEM_LIMIT_BYTES, flags=flags)


def _mod_kernel(c_ref, w_ref, b_ref, o_ref):
    ca = _silu(c_ref[...])
    o_ref[...] = jnp.dot(ca.astype(_BF16), w_ref[...].astype(_BF16),
                         preferred_element_type=_F32) + b_ref[...]


def _adaln_mod(c, w, b):
    bsz, d = c.shape
    n = w.shape[1]
    return pl.pallas_call(
        _mod_kernel,
        out_shape=jax.ShapeDtypeStruct((bsz, n), _F32),
        grid=(n // _TN_MOD,),
        in_specs=[pl.BlockSpec((bsz, d), lambda j: (0, 0)),
                  pl.BlockSpec((d, _TN_MOD), lambda j: (0, j)),
                  pl.BlockSpec((1, _TN_MOD), lambda j: (0, j))],
        out_specs=pl.BlockSpec((bsz, _TN_MOD), lambda j: (0, j)),
        compiler_params=_params(1),
        name="adaln_mod",
    )(c, w, b.reshape(1, n))


def _bias_kernel(table_ref, bucket_ref, o_ref):
    masked = jnp.full((Q_BLOCK, Q_BLOCK), MASK_VALUE, _F32)
    for h in range(N_HEADS):
        table = table_ref[h] * LOG2E
        tiles = []
        for d in range(3):
            bucket = bucket_ref[d]
            acc = masked
            for b in range(N_BUCKETS):
                acc = jnp.where(bucket == b, table[b:b + 1, :], acc)
            tiles.append(acc)
        lo, mid, hi = tiles
        for jb, (left, right) in enumerate(((lo, masked), (mid, lo), (hi, mid), (masked, hi))):
            rows = slice(jb * Q_BLOCK, (jb + 1) * Q_BLOCK)
            o_ref[h, rows, 0:Q_BLOCK] = left
            o_ref[h, rows, Q_BLOCK:2 * Q_BLOCK] = right


def _t5_bucket(rel):
    nb = N_BUCKETS // 2
    max_exact = nb // 2
    ret = (rel > 0).astype(jnp.int32) * nb
    n = jnp.abs(rel)
    assert MAX_DISTANCE == 2 * max_exact ** 2 and nb - max_exact == 8
    n2 = n * n
    large = max_exact + sum((n2 >= 2 ** k).astype(jnp.int32) for k in range(7, 31))
    large = jnp.minimum(large, nb - 1)
    return ret + jnp.where(n < max_exact, n, large)


def _t5_bias_tiles(rel_table):
    r = jnp.arange(Q_BLOCK)
    rel = ((jnp.arange(3) - 1)[:, None, None] * Q_BLOCK
           + r[None, :, None] - r[None, None, :])
    bucket = jnp.where(jnp.abs(rel) <= WINDOW, _t5_bucket(rel), -1).astype(jnp.int32)
    return pl.pallas_call(
        _bias_kernel,
        out_shape=jax.ShapeDtypeStruct((N_HEADS, _KEYS_B, _TQ_B), _F32),
        in_specs=[pl.BlockSpec(memory_space=pltpu.VMEM),
                  pl.BlockSpec(memory_space=pltpu.VMEM)],
        out_specs=pl.BlockSpec(memory_space=pltpu.VMEM),
        compiler_params=pltpu.CompilerParams(vmem_limit_bytes=_VMEM_LIMIT_BYTES),
        name="t5_bias",
    )(jnp.broadcast_to(rel_table.T[:, :, None], (N_HEADS, N_BUCKETS, Q_BLOCK)), bucket)


def _norm_rope(xh, gain, cos, sin):
    ms = jnp.mean(xh * xh, axis=0, keepdims=True)
    y = xh * lax.rsqrt(ms + EPS) * gain
    q = HEAD_DIM // 4
    rot = jnp.concatenate([-y[q:2 * q], y[0:q], -y[3 * q:4 * q], y[2 * q:3 * q]], axis=0)
    return y * cos + rot * sin


def _inproj_kernel(x_ref, shift_ref, scale_ref, gpre_ref, w_ref, cos_ref, sin_ref,
                   qn_ref, kn_ref,
                   qa_ref, ka_ref, va_ref, ga_ref, qb_ref, kb_ref, vb_ref, gb_ref):
    x = x_ref[0]
    ms = jnp.mean(x * x, axis=-1, keepdims=True)
    h = x * lax.rsqrt(ms + EPS) * gpre_ref[...]
    h = h * (1.0 + scale_ref[0]) + shift_ref[0]
    hb = h.astype(_BF16)

    def proj(lo, n):
        return lax.dot_general(w_ref[lo:lo + n, :], hb, _NT, preferred_element_type=_F32)

    cos = cos_ref[...]
    sin = sin_ref[...]
    qn = qn_ref[...]
    kn = kn_ref[...]

    def store_tiled(ref, rows, val):
        tq = ref.shape[3]
        for j in range(ref.shape[1]):
            ref[0, j, rows, :] = val[:, j * tq:(j + 1) * tq]

    qk = proj(0, _QK_ROWS)
    for hd in range(N_HEADS):
        rows = slice(hd * HEAD_DIM, (hd + 1) * HEAD_DIM)
        store_tiled(qa_ref, rows, (_norm_rope(qk[rows], qn, cos, sin) * Q_SCALE).astype(_BF16))
    store_tiled(qb_ref, slice(0, WIDTH), (qk[WIDTH:2 * WIDTH] * Q_SCALE).astype(_BF16))
    ka = qk[2 * WIDTH:2 * WIDTH + KV_WIDTH]
    ka = jnp.concatenate(
        [_norm_rope(ka[hd * HEAD_DIM:(hd + 1) * HEAD_DIM], kn, cos, sin) for hd in range(N_KV)],
        axis=0)
    ka_ref[0] = ka.T.astype(_BF16)
    kb_ref[0] = qk[2 * WIDTH + KV_WIDTH:].T.astype(_BF16)

    gates = _silu(proj(_QK_ROWS, _GATE_ROWS))
    store_tiled(ga_ref, slice(0, WIDTH), gates[:WIDTH].astype(_BF16))
    store_tiled(gb_ref, slice(0, WIDTH), gates[WIDTH:].astype(_BF16))

    v = proj(_QK_ROWS + _GATE_ROWS, _V_ROWS)
    va_ref[0] = v[:KV_WIDTH].astype(_BF16)
    store_tiled(vb_ref, slice(0, KV_WIDTH), v[KV_WIDTH:].astype(_BF16))


def _rope_tables(seq):
    rows = seq // GRID_W
    row = jnp.repeat(jnp.arange(rows), GRID_W).astype(_F32)
    col = jnp.tile(jnp.arange(GRID_W), rows).astype(_F32)
    half = HEAD_DIM // 2
    freqs = ROPE_THETA ** (-jnp.arange(0, half, 2, dtype=_F32) / half)

    def ang2(pos):
        ang = pos[:, None] * freqs[None, :]
        return jnp.concatenate([ang, ang], axis=-1)

    ang = jnp.concatenate([ang2(row), ang2(col)], axis=-1)
    return jnp.cos(ang).T, jnp.sin(ang).T


def _in_proj(x, shift, scale, g_pre, w_in_t, cos_t, sin_t, qn, kn):
    bsz, seq, d = x.shape
    tm = _TM_IN
    tok_t = lambda n: pl.BlockSpec((1, n, tm), lambda b, i: (b, 0, i))
    tok_m = lambda n: pl.BlockSpec((1, tm, n), lambda b, i: (b, i, 0))
    per_b = pl.BlockSpec((1, 1, d), lambda b, i: (b, 0, 0))
    tab = pl.BlockSpec((HEAD_DIM, tm), lambda b, i: (0, i))
    gain = pl.BlockSpec((HEAD_DIM, tm), lambda b, i: (0, 0))
    shp_t = lambda n: jax.ShapeDtypeStruct((bsz, n, seq), _BF16)
    shp_m = lambda n: jax.ShapeDtypeStruct((bsz, seq, n), _BF16)
    tiled = pl.BlockSpec((1, tm // _TQ_A, WIDTH, _TQ_A), lambda b, i: (b, i, 0, 0))
    shp_tiled = jax.ShapeDtypeStruct((bsz, seq // _TQ_A, WIDTH, _TQ_A), _BF16)
    v_blocks = pl.BlockSpec((1, tm // Q_BLOCK, KV_WIDTH, Q_BLOCK), lambda b, i: (b, i, 0, 0))
    shp_v_blocks = jax.ShapeDtypeStruct((bsz, seq // Q_BLOCK, KV_WIDTH, Q_BLOCK), _BF16)
    return pl.pallas_call(
        _inproj_kernel,
        out_shape=(shp_tiled, shp_m(KV_WIDTH), shp_t(KV_WIDTH), shp_tiled,
                   shp_tiled, shp_m(KV_WIDTH), shp_v_blocks, shp_tiled),
        grid=(bsz, seq // tm),
        in_specs=[pl.BlockSpec((1, tm, d), lambda b, i: (b, i, 0)),
                  per_b, per_b,
                  pl.BlockSpec((1, d), lambda b, i: (0, 0)),
                  pl.BlockSpec((IN_COLS, d), lambda b, i: (0, 0)),
                  tab, tab, gain, gain],
        out_specs=(tiled, tok_m(KV_WIDTH), tok_t(KV_WIDTH), tiled,
                   tiled, tok_m(KV_WIDTH), v_blocks, tiled),
        compiler_params=_params(2),
        name="in_proj",
    )(x, shift, scale, g_pre.reshape(1, d), w_in_t, cos_t, sin_t,
      jnp.broadcast_to(qn[:, None], (HEAD_DIM, tm)),
      jnp.broadcast_to(kn[:, None], (HEAD_DIM, tm)))


_PAIRS = N_HEADS // 2
_ONES_ROWS = 16


def _head_rows(hd):
    return slice(hd * HEAD_DIM, (hd + 1) * HEAD_DIM)


def _pair_q(head, pair):
    def padded(hd):
        qh = head(hd)
        z = jnp.zeros_like(qh)
        return jnp.concatenate([qh, z] if hd // GROUP == 0 else [z, qh], axis=0)
    return jnp.concatenate([padded(2 * pair), padded(2 * pair + 1)], axis=1)


def _pair_out(o, inv_l, gate, pair, tq):
    o = o * inv_l
    g0 = gate(2 * pair).astype(_F32)
    g1 = gate(2 * pair + 1).astype(_F32)
    return jnp.concatenate([o[:, :tq] * g0, o[:, tq:] * g1], axis=0).T.astype(_BF16)


def _pair_lanes(pair):
    return slice(pair * 2 * HEAD_DIM, (pair + 1) * 2 * HEAD_DIM)


def _v_with_ones(v):
    return jnp.concatenate([v, jnp.ones((_ONES_ROWS, v.shape[1]), v.dtype)], axis=0)


def _mixers_kernel(qa_ref, ka_ref, va_ref, ga_ref, qb_ref, kb_ref, vb_ref, gb_ref,
                   sink_ref, bias_ref, o_ref, s0_ref, s1_ref):
    n_tiles, tq = qa_ref.shape[1], qa_ref.shape[3]
    seq = ka_ref.shape[1]
    n_blocks = seq // Q_BLOCK
    s_refs = (s0_ref, s1_ref)
    a_chunks = [slice(c * _TK_A, (c + 1) * _TK_A) for c in range(seq // _TK_A)]
    items = [("a", p) for p in range(_PAIRS)] + [("b", p) for p in range(_PAIRS)]

    def staged(start, size, zero):
        return pl.ds(pl.multiple_of(zero + start, size), size)

    def window_blocks(tile):
        first = (tq // Q_BLOCK) * tile - 1
        return [jnp.clip(first + j, 0, n_blocks - 1) for j in range(_KEYS_B // Q_BLOCK)]

    def score_chunks(kind, pair, buf, tile, zero):
        q_ref = qa_ref if kind == "a" else qb_ref

        def weights():
            return _pair_q(lambda hd: q_ref[0, tile, _head_rows(hd), :], pair)

        def global_chunk(rows):
            s = jnp.dot(ka_ref[0, rows, :], weights(), preferred_element_type=_F32)
            s_refs[buf][staged(rows.start, _TK_A, zero), :] = s
            return jnp.max(s, axis=0, keepdims=True)

        def window():
            k = jnp.concatenate(
                [kb_ref[0, pl.ds(pl.multiple_of(blk * Q_BLOCK, Q_BLOCK), Q_BLOCK), :]
                 for blk in window_blocks(tile)], axis=0)
            bias = jnp.concatenate([bias_ref[2 * pair], bias_ref[2 * pair + 1]], axis=1)
            s = jnp.dot(k, weights(), preferred_element_type=_F32) + bias
            s = jnp.concatenate(
                [jnp.where(tile == 0, MASK_VALUE, s[:Q_BLOCK]),
                 s[Q_BLOCK:_KEYS_B - Q_BLOCK],
                 jnp.where(tile == n_tiles - 1, MASK_VALUE, s[_KEYS_B - Q_BLOCK:])], axis=0)
            s_refs[buf][staged(0, _KEYS_B, zero), :] = s
            return jnp.maximum(jnp.max(s, axis=0, keepdims=True), sink_ref[pair] * LOG2E)

        if kind == "a":
            return [functools.partial(global_chunk, rows) for rows in a_chunks]
        return [window]

    def pv_chunks(kind, pair, buf, tile, zero, m):
        kv = _head_rows(pair // (GROUP // 2))

        def global_chunk(rows):
            e = jnp.exp2(s_refs[buf][staged(rows.start, _TK_A, zero), :] - m)
            return jnp.dot(_v_with_ones(va_ref[0, kv, rows]), e.astype(_BF16),
                           preferred_element_type=_F32)

        def window():
            e = jnp.exp2(s_refs[buf][staged(0, _KEYS_B, zero), :] - m)
            v = jnp.concatenate([vb_ref[0, blk] for blk in window_blocks(tile)], axis=1)
            return jnp.dot(_v_with_ones(v[kv, :]), e.astype(_BF16), preferred_element_type=_F32)

        if kind == "a":
            return [functools.partial(global_chunk, rows) for rows in a_chunks]
        return [window]

    def finish(kind, pair, tile, o, m):
        l = o[HEAD_DIM:HEAD_DIM + 1]
        if kind == "b":
            l = l + jnp.exp2(sink_ref[pair] * LOG2E - m)
        g_ref = ga_ref if kind == "a" else gb_ref
        lane0 = (0 if kind == "a" else WIDTH) + pair * 2 * HEAD_DIM
        o_ref[0, pl.ds(pl.multiple_of(tile * tq, tq), tq), lane0:lane0 + 2 * HEAD_DIM] = _pair_out(
            o[:HEAD_DIM], 1.0 / l, lambda hd: g_ref[0, tile, _head_rows(hd), :], pair, tq)

    def col_max(parts):
        return functools.reduce(jnp.maximum, parts)

    def tile_body(tile, m):
        zero = jnp.minimum(tile, 0)
        next_tile = jnp.minimum(tile + 1, n_tiles - 1)
        for idx, (kind, pair) in enumerate(items):
            if idx + 1 < len(items):
                nxt = score_chunks(*items[idx + 1], (idx + 1) % 2, tile, zero)
            else:
                nxt = score_chunks(*items[0], 0, next_tile, zero)
            cur = pv_chunks(kind, pair, idx % 2, tile, zero, m)
            parts, o = [], None
            for i in range(max(len(nxt), len(cur))):
                if i < len(nxt):
                    parts.append(nxt[i]())
                if i < len(cur):
                    pv = cur[i]()
                    o = pv if o is None else o + pv
            finish(kind, pair, tile, o, m)
            m = col_max(parts)
        return m

    first = jnp.minimum(pl.program_id(0), 0)
    m0 = col_max([chunk() for chunk in score_chunks(*items[0], 0, first, first)])
    lax.fori_loop(0, n_tiles, tile_body, m0)


def _mixers(qa, ka, va, ga, qb, kb, vb, gb, sink, bias):
    bsz, n_tiles, _, tq = qa.shape
    seq = ka.shape[1]
    tiled = pl.BlockSpec((1, n_tiles, WIDTH, tq), lambda b: (b, 0, 0, 0))
    k_spec = pl.BlockSpec((1, seq, KV_WIDTH), lambda b: (b, 0, 0))
    const = lambda shape: pl.BlockSpec(shape, lambda b: (0,) * len(shape))
    return pl.pallas_call(
        _mixers_kernel,
        out_shape=jax.ShapeDtypeStruct((bsz, seq, 2 * WIDTH), _BF16),
        grid=(bsz,),
        in_specs=[tiled, k_spec, pl.BlockSpec((1, KV_WIDTH, seq), lambda b: (b, 0, 0)), tiled,
                  tiled, k_spec,
                  pl.BlockSpec((1, seq // Q_BLOCK, KV_WIDTH, Q_BLOCK), lambda b: (b, 0, 0, 0)),
                  tiled,
                  const((_PAIRS, 1, 2 * tq)), const((N_HEADS, _KEYS_B, tq))],
        out_specs=pl.BlockSpec((1, seq, 2 * WIDTH), lambda b: (b, 0, 0)),
        scratch_shapes=[pltpu.VMEM((seq, 2 * tq), _F32), pltpu.VMEM((seq, 2 * tq), _F32)],
        compiler_params=_params(1),
        name="mixers",
    )(qa, ka, va, ga, qb, kb, vb, gb,
      jnp.broadcast_to(sink[:, None], (N_HEADS, tq)).reshape(_PAIRS, 1, 2 * tq), bias)


def _outproj_kernel(o_ref_in, w_ref, x_ref, gate_ref, gpost_ref, o_ref):
    y = jnp.dot(o_ref_in[0], w_ref[...], preferred_element_type=_F32)
    ms = jnp.mean(y * y, axis=-1, keepdims=True)
    yn = y * lax.rsqrt(ms + EPS) * gpost_ref[...]
    o_ref[0] = x_ref[0] + gate_ref[0] * yn


def _out_proj(o, w_out, x, gate, g_post):
    bsz, seq, d = x.shape
    tm = _TM_OUT
    xo = pl.BlockSpec((1, tm, d), lambda b, i: (b, i, 0))
    return pl.pallas_call(
        _outproj_kernel,
        out_shape=jax.ShapeDtypeStruct((bsz, seq, d), _F32),
        grid=(bsz, seq // tm),
        in_specs=[pl.BlockSpec((1, tm, 2 * WIDTH), lambda b, i: (b, i, 0)),
                  pl.BlockSpec((2 * WIDTH, d), lambda b, i: (0, 0)),
                  xo,
                  pl.BlockSpec((1, 1, d), lambda b, i: (b, 0, 0)),
                  pl.BlockSpec((1, d), lambda b, i: (0, 0))],
        out_specs=xo,
        compiler_params=_params(2),
        name="out_proj",
    )(o, w_out, x, gate, g_post.reshape(1, d))


def kernel(x, c, w_ada, b_ada, g_pre, g_post, w_in, qn_a, kn_a, sink_b, w_out, rel_table):
    bsz, seq, d = x.shape
    depth = w_ada.shape[0]
    cos_t, sin_t = _rope_tables(seq)
    bias = _t5_bias_tiles(rel_table)
    for l in range(depth):
        mod = _adaln_mod(c, w_ada[l], b_ada[l])
        shift, scale, gate = (mod[:, i * d:(i + 1) * d].reshape(bsz, 1, d) for i in range(3))
        w_in_t = jnp.concatenate([w_in[l][:, lo:lo + n] for lo, n in _W_ROW_ORDER],
                                 axis=1).T.astype(_BF16)
        qa, ka, va, ga, qb, kb, vb, gb = _in_proj(
            x, shift, scale, g_pre[l], w_in_t, cos_t, sin_t, qn_a[l], kn_a[l])
        o = _mixers(qa, ka, va, ga, qb, kb, vb, gb, sink_b[l], bias)
        x = _out_proj(o, w_out[l].astype(_BF16), x, gate, g_post[l])
    return x
```

```python
import functools

import jax
import jax.numpy as jnp
import numpy as np
from jax import lax
from jax.experimental import pallas as pl
from jax.experimental.pallas import tpu as pltpu

D_MODEL = 1024
HEAD_DIM = 64
N_HEADS = 8
N_KV = 2
GROUP = N_HEADS // N_KV
WIDTH = N_HEADS * HEAD_DIM
KV_WIDTH = N_KV * HEAD_DIM
IN_COLS = 2 * (2 * WIDTH + 2 * KV_WIDTH)
Q_BLOCK = 128
WINDOW = 128
GRID_W = 64
ROPE_THETA = 10000.0
N_BUCKETS = 32
MAX_DISTANCE = 128
EPS = 1e-6
MASK_VALUE = -1e30
LOG2E = 1.4426950408889634
Q_SCALE = HEAD_DIM ** -0.5 * LOG2E

_QA, _KA, _VA, _GA = 0, WIDTH, WIDTH + KV_WIDTH, WIDTH + 2 * KV_WIDTH
_HALF = 2 * WIDTH + 2 * KV_WIDTH
_QB, _KB, _VB, _GB = _HALF + _QA, _HALF + _KA, _HALF + _VA, _HALF + _GA
_W_ROW_ORDER = ((_QA, WIDTH), (_QB, WIDTH), (_KA, KV_WIDTH), (_KB, KV_WIDTH),
                (_GA, WIDTH), (_GB, WIDTH), (_VA, KV_WIDTH), (_VB, KV_WIDTH))
_QK_ROWS = 2 * WIDTH + 2 * KV_WIDTH
_GATE_ROWS = 2 * WIDTH
_V_ROWS = 2 * KV_WIDTH

_F32 = jnp.float32
_BF16 = jnp.bfloat16

_VMEM_LIMIT_BYTES = 48 * 1024 * 1024

_TM_IN = 1024
_TQ_A = 256
_TK_A = 256
_TQ_B = 256
_KEYS_B = _TQ_B + 2 * WINDOW
_TM_OUT = 1024
_TM_OUT_CHUNK = 256
_TN_MOD = 512

_NT = (((1,), (1,)), ((), ()))


def _silu(v):
    return 0.5 * v * (1.0 + jnp.tanh(0.5 * v))


def _params(n_grid, flags=None):
    return pltpu.CompilerParams(
        dimension_semantics=("arbitrary",) * n_grid,
        vmem_limit_bytes=_VMEM_LIMIT_BYTES, flags=flags)


def _mod_kernel(c_ref, w_ref, b_ref, o_ref):
    ca = _silu(c_ref[...])
    o_ref[...] = jnp.dot(ca.astype(_BF16), w_ref[...].astype(_BF16),
                         preferred_element_type=_F32) + b_ref[...]


def _adaln_mod(c, w, b):
    bsz, d = c.shape
    n = w.shape[1]
    return pl.pallas_call(
        _mod_kernel,
        out_shape=jax.ShapeDtypeStruct((bsz, n), _F32),
        grid=(n // _TN_MOD,),
        in_specs=[pl.BlockSpec((bsz, d), lambda j: (0, 0)),
                  pl.BlockSpec((d, _TN_MOD), lambda j: (0, j)),
                  pl.BlockSpec((1, _TN_MOD), lambda j: (0, j))],
        out_specs=pl.BlockSpec((bsz, _TN_MOD), lambda j: (0, j)),
        compiler_params=_params(1),
        name="adaln_mod",
    )(c, w, b.reshape(1, n))


def _bias_kernel(table_ref, bucket_ref, o_ref):
    masked = jnp.full((Q_BLOCK, Q_BLOCK), MASK_VALUE, _F32)
    for h in range(N_HEADS):
        table = table_ref[h] * LOG2E
        tiles = []
        for d in range(3):
            bucket = bucket_ref[d]
            acc = masked
            for b in range(N_BUCKETS):
                acc = jnp.where(bucket == b, table[b:b + 1, :], acc)
            tiles.append(acc)
        lo, mid, hi = tiles
        for jb, (left, right) in enumerate(((lo, masked), (mid, lo), (hi, mid), (masked, hi))):
            rows = slice(jb * Q_BLOCK, (jb + 1) * Q_BLOCK)
            o_ref[h, rows, 0:Q_BLOCK] = left
            o_ref[h, rows, Q_BLOCK:2 * Q_BLOCK] = right


def _t5_bucket(rel):
    nb = N_BUCKETS // 2
    max_exact = nb // 2
    ret = (rel > 0).astype(jnp.int32) * nb
    n = jnp.abs(rel)
    assert MAX_DISTANCE == 2 * max_exact ** 2 and nb - max_exact == 8
    n2 = n * n
    large = max_exact + sum((n2 >= 2 ** k).astype(jnp.int32) for k in range(7, 31))
    large = jnp.minimum(large, nb - 1)
    return ret + jnp.where(n < max_exact, n, large)


def _t5_bias_tiles(rel_table):
    r = jnp.arange(Q_BLOCK)
    rel = ((jnp.arange(3) - 1)[:, None, None] * Q_BLOCK
           + r[None, :, None] - r[None, None, :])
    bucket = jnp.where(jnp.abs(rel) <= WINDOW, _t5_bucket(rel), -1).astype(jnp.int32)
    return pl.pallas_call(
        _bias_kernel,
        out_shape=jax.ShapeDtypeStruct((N_HEADS, _KEYS_B, _TQ_B), _F32),
        in_specs=[pl.BlockSpec(memory_space=pltpu.VMEM),
                  pl.BlockSpec(memory_space=pltpu.VMEM)],
        out_specs=pl.BlockSpec(memory_space=pltpu.VMEM),
        compiler_params=pltpu.CompilerParams(vmem_limit_bytes=_VMEM_LIMIT_BYTES),
        name="t5_bias",
    )(jnp.broadcast_to(rel_table.T[:, :, None], (N_HEADS, N_BUCKETS, Q_BLOCK)), bucket)


def _norm_rope(xh, gain, cos, sin):
    ms = jnp.mean(xh * xh, axis=0, keepdims=True)
    y = xh * lax.rsqrt(ms + EPS) * gain
    q = HEAD_DIM // 4
    rot = jnp.concatenate([-y[q:2 * q], y[0:q], -y[3 * q:4 * q], y[2 * q:3 * q]], axis=0)
    return y * cos + rot * sin


def _inproj_kernel(x_ref, shift_ref, scale_ref, gpre_ref, w_ref, cos_ref, sin_ref,
                   qn_ref, kn_ref,
                   qa_ref, ka_ref, va_ref, ga_ref, qb_ref, kb_ref, vb_ref, gb_ref):
    x = x_ref[0]
    ms = jnp.mean(x * x, axis=-1, keepdims=True)
    h = x * lax.rsqrt(ms + EPS) * gpre_ref[...]
    h = h * (1.0 + scale_ref[0]) + shift_ref[0]
    hb = h.astype(_BF16)

    def proj(lo, n):
        return lax.dot_general(w_ref[lo:lo + n, :], hb, _NT, preferred_element_type=_F32)

    cos = cos_ref[...]
    sin = sin_ref[...]
    qn = qn_ref[...]
    kn = kn_ref[...]

    def store_tiled(ref, rows, val):
        tq = ref.shape[3]
        for j in range(ref.shape[1]):
            ref[0, j, rows, :] = val[:, j * tq:(j + 1) * tq]

    qk = proj(0, _QK_ROWS)
    for hd in range(N_HEADS):
        rows = slice(hd * HEAD_DIM, (hd + 1) * HEAD_DIM)
        store_tiled(qa_ref, rows, (_norm_rope(qk[rows], qn, cos, sin) * Q_SCALE).astype(_BF16))
    store_tiled(qb_ref, slice(0, WIDTH), (qk[WIDTH:2 * WIDTH] * Q_SCALE).astype(_BF16))
    ka = qk[2 * WIDTH:2 * WIDTH + KV_WIDTH]
    ka = jnp.concatenate(
        [_norm_rope(ka[hd * HEAD_DIM:(hd + 1) * HEAD_DIM], kn, cos, sin) for hd in range(N_KV)],
        axis=0)
    ka_ref[0] = ka.T.astype(_BF16)
    kb_ref[0] = qk[2 * WIDTH + KV_WIDTH:].T.astype(_BF16)

    gates = _silu(proj(_QK_ROWS, _GATE_ROWS))
    store_tiled(ga_ref, slice(0, WIDTH), gates[:WIDTH].astype(_BF16))
    store_tiled(gb_ref, slice(0, WIDTH), gates[WIDTH:].astype(_BF16))

    v = proj(_QK_ROWS + _GATE_ROWS, _V_ROWS)
    va_ref[0] = v[:KV_WIDTH].astype(_BF16)
    store_tiled(vb_ref, slice(0, KV_WIDTH), v[KV_WIDTH:].astype(_BF16))


def _rope_tables(seq):
    rows = seq // GRID_W
    row = jnp.repeat(jnp.arange(rows), GRID_W).astype(_F32)
    col = jnp.tile(jnp.arange(GRID_W), rows).astype(_F32)
    half = HEAD_DIM // 2
    freqs = ROPE_THETA ** (-jnp.arange(0, half, 2, dtype=_F32) / half)

    def ang2(pos):
        ang = pos[:, None] * freqs[None, :]
        return jnp.concatenate([ang, ang], axis=-1)

    ang = jnp.concatenate([ang2(row), ang2(col)], axis=-1)
    return jnp.cos(ang).T, jnp.sin(ang).T


def _in_proj(x, shift, scale, g_pre, w_in_t, cos_t, sin_t, qn, kn):
    bsz, seq, d = x.shape
    tm = _TM_IN
    tok_t = lambda n: pl.BlockSpec((1, n, tm), lambda b, i: (b, 0, i))
    tok_m = lambda n: pl.BlockSpec((1, tm, n), lambda b, i: (b, i, 0))
    per_b = pl.BlockSpec((1, 1, d), lambda b, i: (b, 0, 0))
    tab = pl.BlockSpec((HEAD_DIM, tm), lambda b, i: (0, i))
    gain = pl.BlockSpec((HEAD_DIM, tm), lambda b, i: (0, 0))
    shp_t = lambda n: jax.ShapeDtypeStruct((bsz, n, seq), _BF16)
    shp_m = lambda n: jax.ShapeDtypeStruct((bsz, seq, n), _BF16)
    tiled = pl.BlockSpec((1, tm // _TQ_A, WIDTH, _TQ_A), lambda b, i: (b, i, 0, 0))
    shp_tiled = jax.ShapeDtypeStruct((bsz, seq // _TQ_A, WIDTH, _TQ_A), _BF16)
    v_blocks = pl.BlockSpec((1, tm // Q_BLOCK, KV_WIDTH, Q_BLOCK), lambda b, i: (b, i, 0, 0))
    shp_v_blocks = jax.ShapeDtypeStruct((bsz, seq // Q_BLOCK, KV_WIDTH, Q_BLOCK), _BF16)
    return pl.pallas_call(
        _inproj_kernel,
        out_shape=(shp_tiled, shp_m(KV_WIDTH), shp_t(KV_WIDTH), shp_tiled,
                   shp_tiled, shp_m(KV_WIDTH), shp_v_blocks, shp_tiled),
        grid=(bsz, seq // tm),
        in_specs=[pl.BlockSpec((1, tm, d), lambda b, i: (b, i, 0)),
                  per_b, per_b,
                  pl.BlockSpec((1, d), lambda b, i: (0, 0)),
                  pl.BlockSpec((IN_COLS, d), lambda b, i: (0, 0)),
                  tab, tab, gain, gain],
        out_specs=(tiled, tok_m(KV_WIDTH), tok_t(KV_WIDTH), tiled,
                   tiled, tok_m(KV_WIDTH), v_blocks, tiled),
        compiler_params=_params(2),
        name="in_proj",
    )(x, shift, scale, g_pre.reshape(1, d), w_in_t, cos_t, sin_t,
      jnp.broadcast_to(qn[:, None], (HEAD_DIM, tm)),
      jnp.broadcast_to(kn[:, None], (HEAD_DIM, tm)))


_PAIRS = N_HEADS // 2
_ONES_ROWS = 16


def _head_rows(hd):
    return slice(hd * HEAD_DIM, (hd + 1) * HEAD_DIM)


def _pair_q(head, pair):
    def padded(hd):
        qh = head(hd)
        z = jnp.zeros_like(qh)
        return jnp.concatenate([qh, z] if hd // GROUP == 0 else [z, qh], axis=0)
    return jnp.concatenate([padded(2 * pair), padded(2 * pair + 1)], axis=1)


def _pair_out(o, inv_l, gate, pair, tq):
    o = o * inv_l
    g0 = gate(2 * pair).astype(_F32)
    g1 = gate(2 * pair + 1).astype(_F32)
    return jnp.concatenate([o[:, :tq] * g0, o[:, tq:] * g1], axis=0).T.astype(_BF16)


def _pair_lanes(pair):
    return slice(pair * 2 * HEAD_DIM, (pair + 1) * 2 * HEAD_DIM)


def _v_with_ones(v):
    return jnp.concatenate([v, jnp.ones((_ONES_ROWS, v.shape[1]), v.dtype)], axis=0)


def _mixers_kernel(qa_ref, ka_ref, va_ref, ga_ref, qb_ref, kb_ref, vb_ref, gb_ref,
                   sink_ref, bias_ref, o_ref, s0_ref, s1_ref):
    n_tiles, tq = qa_ref.shape[1], qa_ref.shape[3]
    seq = ka_ref.shape[1]
    n_blocks = seq // Q_BLOCK
    s_refs = (s0_ref, s1_ref)
    a_chunks = [slice(c * _TK_A, (c + 1) * _TK_A) for c in range(seq // _TK_A)]
    items = [("a", p) for p in range(_PAIRS)] + [("b", p) for p in range(_PAIRS)]

    def staged(start, size, zero):
        return pl.ds(pl.multiple_of(zero + start, size), size)

    def window_blocks(tile):
        first = (tq // Q_BLOCK) * tile - 1
        return [jnp.clip(first + j, 0, n_blocks - 1) for j in range(_KEYS_B // Q_BLOCK)]

    def score_chunks(kind, pair, buf, tile, zero):
        q_ref = qa_ref if kind == "a" else qb_ref

        def weights():
            return _pair_q(lambda hd: q_ref[0, tile, _head_rows(hd), :], pair)

        def global_chunk(rows):
            s = jnp.dot(ka_ref[0, rows, :], weights(), preferred_element_type=_F32)
            s_refs[buf][staged(rows.start, _TK_A, zero), :] = s
            return jnp.max(s, axis=0, keepdims=True)

        def window():
            k = jnp.concatenate(
                [kb_ref[0, pl.ds(pl.multiple_of(blk * Q_BLOCK, Q_BLOCK), Q_BLOCK), :]
                 for blk in window_blocks(tile)], axis=0)
            bias = jnp.concatenate([bias_ref[2 * pair], bias_ref[2 * pair + 1]], axis=1)
            s = jnp.dot(k, weights(), preferred_element_type=_F32) + bias
            s = jnp.concatenate(
                [jnp.where(tile == 0, MASK_VALUE, s[:Q_BLOCK]),
                 s[Q_BLOCK:_KEYS_B - Q_BLOCK],
                 jnp.where(tile == n_tiles - 1, MASK_VALUE, s[_KEYS_B - Q_BLOCK:])], axis=0)
            s_refs[buf][staged(0, _KEYS_B, zero), :] = s
            return jnp.maximum(jnp.max(s, axis=0, keepdims=True), sink_ref[pair] * LOG2E)

        if kind == "a":
            return [functools.partial(global_chunk, rows) for rows in a_chunks]
        return [window]

    def pv_chunks(kind, pair, buf, tile, zero, m):
        kv = _head_rows(pair // (GROUP // 2))

        def global_chunk(rows):
            e = jnp.exp2(s_refs[buf][staged(rows.start, _TK_A, zero), :] - m)
            return jnp.dot(_v_with_ones(va_ref[0, kv, rows]), e.astype(_BF16),
                           preferred_element_type=_F32)

        def window():
            e = jnp.exp2(s_refs[buf][staged(0, _KEYS_B, zero), :] - m)
            v = jnp.concatenate([vb_ref[0, blk] for blk in window_blocks(tile)], axis=1)
            return jnp.dot(_v_with_ones(v[kv, :]), e.astype(_BF16), preferred_element_type=_F32)

        if kind == "a":
            return [functools.partial(global_chunk, rows) for rows in a_chunks]
        return [window]

    def finish(kind, pair, tile, o, m):
        l = o[HEAD_DIM:HEAD_DIM + 1]
        if kind == "b":
            l = l + jnp.exp2(sink_ref[pair] * LOG2E - m)
        g_ref = ga_ref if kind == "a" else gb_ref
        lane0 = (0 if kind == "a" else WIDTH) + pair * 2 * HEAD_DIM
        o_ref[0, pl.ds(pl.multiple_of(tile * tq, tq), tq), lane0:lane0 + 2 * HEAD_DIM] = _pair_out(
            o[:HEAD_DIM], 1.0 / l, lambda hd: g_ref[0, tile, _head_rows(hd), :], pair, tq)

    def col_max(parts):
        return functools.reduce(jnp.maximum, parts)

    def tile_body(tile, m):
        zero = jnp.minimum(tile, 0)
        next_tile = jnp.minimum(tile + 1, n_tiles - 1)
        for idx, (kind, pair) in enumerate(items):
            if idx + 1 < len(items):
                nxt = score_chunks(*items[idx + 1], (idx + 1) % 2, tile, zero)
            else:
                nxt = score_chunks(*items[0], 0, next_tile, zero)
            cur = pv_chunks(kind, pair, idx % 2, tile, zero, m)
            parts, o = [], None
            for i in range(max(len(nxt), len(cur))):
                if i < len(nxt):
                    parts.append(nxt[i]())
                if i < len(cur):
                    pv = cur[i]()
                    o = pv if o is None else o + pv
            finish(kind, pair, tile, o, m)
            m = col_max(parts)
        return m

    first = jnp.minimum(pl.program_id(0), 0)
    m0 = col_max([chunk() for chunk in score_chunks(*items[0], 0, first, first)])
    lax.fori_loop(0, n_tiles, tile_body, m0)


def _mixers(qa, ka, va, ga, qb, kb, vb, gb, sink, bias):
    bsz, n_tiles, _, tq = qa.shape
    seq = ka.shape[1]
    tiled = pl.BlockSpec((1, n_tiles, WIDTH, tq), lambda b: (b, 0, 0, 0))
    k_spec = pl.BlockSpec((1, seq, KV_WIDTH), lambda b: (b, 0, 0))
    const = lambda shape: pl.BlockSpec(shape, lambda b: (0,) * len(shape))
    return pl.pallas_call(
        _mixers_kernel,
        out_shape=jax.ShapeDtypeStruct((bsz, seq, 2 * WIDTH), _BF16),
        grid=(bsz,),
        in_specs=[tiled, k_spec, pl.BlockSpec((1, KV_WIDTH, seq), lambda b: (b, 0, 0)), tiled,
                  tiled, k_spec,
                  pl.BlockSpec((1, seq // Q_BLOCK, KV_WIDTH, Q_BLOCK), lambda b: (b, 0, 0, 0)),
                  tiled,
                  const((_PAIRS, 1, 2 * tq)), const((N_HEADS, _KEYS_B, tq))],
        out_specs=pl.BlockSpec((1, seq, 2 * WIDTH), lambda b: (b, 0, 0)),
        scratch_shapes=[pltpu.VMEM((seq, 2 * tq), _F32), pltpu.VMEM((seq, 2 * tq), _F32)],
        compiler_params=_params(1),
        name="mixers",
    )(qa, ka, va, ga, qb, kb, vb, gb,
      jnp.broadcast_to(sink[:, None], (N_HEADS, tq)).reshape(_PAIRS, 1, 2 * tq), bias)


def _outproj_kernel(a_ref, w_ref, x_ref, gate_ref, gpost_ref, o_ref):
    w = w_ref[...].astype(_BF16)
    for c in range(a_ref.shape[1] // _TM_OUT_CHUNK):
        rows = slice(c * _TM_OUT_CHUNK, (c + 1) * _TM_OUT_CHUNK)
        y = jnp.dot(a_ref[0, rows, :], w, preferred_element_type=_F32)
        ms = jnp.mean(y * y, axis=-1, keepdims=True)
        yn = y * lax.rsqrt(ms + EPS) * gpost_ref[...]
        o_ref[0, rows, :] = x_ref[0, rows, :] + gate_ref[0] * yn


def _out_proj(o, w_out, x, gate, g_post):
    bsz, seq, d = x.shape
    tm = _TM_OUT
    xo = pl.BlockSpec((1, tm, d), lambda b, i: (b, i, 0))
    return pl.pallas_call(
        _outproj_kernel,
        out_shape=jax.ShapeDtypeStruct((bsz, seq, d), _F32),
        grid=(bsz, seq // tm),
        in_specs=[pl.BlockSpec((1, tm, 2 * WIDTH), lambda b, i: (b, i, 0)),
                  pl.BlockSpec((2 * WIDTH, d), lambda b, i: (0, 0)),
                  xo,
                  pl.BlockSpec((1, 1, d), lambda b, i: (b, 0, 0)),
                  pl.BlockSpec((1, d), lambda b, i: (0, 0))],
        out_specs=xo,
        compiler_params=_params(2),
        name="out_proj",
    )(o, w_out, x, gate, g_post.reshape(1, d))


def kernel(x, c, w_ada, b_ada, g_pre, g_post, w_in, qn_a, kn_a, sink_b, w_out, rel_table):
    bsz, seq, d = x.shape
    depth = w_ada.shape[0]
    cos_t, sin_t = _rope_tables(seq)
    bias = _t5_bias_tiles(rel_table)
    for l in range(depth):
        mod = _adaln_mod(c, w_ada[l], b_ada[l])
        shift, scale, gate = (mod[:, i * d:(i + 1) * d].reshape(bsz, 1, d) for i in range(3))
        w_in_t = jnp.concatenate([w_in[l][:, lo:lo + n] for lo, n in _W_ROW_ORDER],
                                 axis=1).T.astype(_BF16)
        qa, ka, va, ga, qb, kb, vb, gb = _in_proj(
            x, shift, scale, g_pre[l], w_in_t, cos_t, sin_t, qn_a[l], kn_a[l])
        o = _mixers(qa, ka, va, ga, qb, kb, vb, gb, sink_b[l], bias)
        x = _out_proj(o, w_out[l], x, gate, g_post[l])
    return x
```

```python
import functools

import jax
import jax.numpy as jnp
import numpy as np
from jax import lax
from jax.experimental import pallas as pl
from jax.experimental.pallas import tpu as pltpu

D_MODEL = 1024
HEAD_DIM = 64
N_HEADS = 8
N_KV = 2
GROUP = N_HEADS // N_KV
WIDTH = N_HEADS * HEAD_DIM
KV_WIDTH = N_KV * HEAD_DIM
IN_COLS = 2 * (2 * WIDTH + 2 * KV_WIDTH)
Q_BLOCK = 128
WINDOW = 128
GRID_W = 64
ROPE_THETA = 10000.0
N_BUCKETS = 32
MAX_DISTANCE = 128
EPS = 1e-6
MASK_VALUE = -1e30
LOG2E = 1.4426950408889634
Q_SCALE = HEAD_DIM ** -0.5 * LOG2E

_QA, _KA, _VA, _GA = 0, WIDTH, WIDTH + KV_WIDTH, WIDTH + 2 * KV_WIDTH
_HALF = 2 * WIDTH + 2 * KV_WIDTH
_QB, _KB, _VB, _GB = _HALF + _QA, _HALF + _KA, _HALF + _VA, _HALF + _GA
_QK_GROUP = ((_QA, WIDTH), (_QB, WIDTH), (_KA, KV_WIDTH), (_KB, KV_WIDTH))
_GATE_GROUP = ((_GA, WIDTH), (_GB, WIDTH))
_V_GROUP = ((_VA, KV_WIDTH), (_VB, KV_WIDTH))

_F32 = jnp.float32
_BF16 = jnp.bfloat16

_VMEM_LIMIT_BYTES = 48 * 1024 * 1024

_TM_IN = 1024
_TQ_A = 256
_TK_A = 256
_TQ_B = 256
_KEYS_B = _TQ_B + 2 * WINDOW
_TM_OUT = 1024
_TM_OUT_CHUNK = 256
_TN_MOD = 512

_NT = (((1,), (1,)), ((), ()))


def _silu(v):
    return 0.5 * v * (1.0 + jnp.tanh(0.5 * v))


def _params(n_grid, flags=None):
    return pltpu.CompilerParams(
        dimension_semantics=("arbitrary",) * n_grid,
        vmem_limit_bytes=_VMEM_LIMIT_BYTES, flags=flags)


def _mod_kernel(c_ref, w_ref, b_ref, o_ref):
    ca = _silu(c_ref[...])
    o_ref[...] = jnp.dot(ca.astype(_BF16), w_ref[...].astype(_BF16),
                         preferred_element_type=_F32) + b_ref[...]


def _adaln_mod(c, w, b):
    bsz, d = c.shape
    n = w.shape[1]
    return pl.pallas_call(
        _mod_kernel,
        out_shape=jax.ShapeDtypeStruct((bsz, n), _F32),
        grid=(n // _TN_MOD,),
        in_specs=[pl.BlockSpec((bsz, d), lambda j: (0, 0)),
                  pl.BlockSpec((d, _TN_MOD), lambda j: (0, j)),
                  pl.BlockSpec((1, _TN_MOD), lambda j: (0, j))],
        out_specs=pl.BlockSpec((bsz, _TN_MOD), lambda j: (0, j)),
        compiler_params=_params(1),
        name="adaln_mod",
    )(c, w, b.reshape(1, n))


def _bias_kernel(table_ref, bucket_ref, o_ref):
    masked = jnp.full((Q_BLOCK, Q_BLOCK), MASK_VALUE, _F32)
    for h in range(N_HEADS):
        table = table_ref[h] * LOG2E
        tiles = []
        for d in range(3):
            bucket = bucket_ref[d]
            acc = masked
            for b in range(N_BUCKETS):
                acc = jnp.where(bucket == b, table[b:b + 1, :], acc)
            tiles.append(acc)
        lo, mid, hi = tiles
        for jb, (left, right) in enumerate(((lo, masked), (mid, lo), (hi, mid), (masked, hi))):
            rows = slice(jb * Q_BLOCK, (jb + 1) * Q_BLOCK)
            o_ref[h, rows, 0:Q_BLOCK] = left
            o_ref[h, rows, Q_BLOCK:2 * Q_BLOCK] = right


def _t5_bucket(rel):
    nb = N_BUCKETS // 2
    max_exact = nb // 2
    ret = (rel > 0).astype(jnp.int32) * nb
    n = jnp.abs(rel)
    assert MAX_DISTANCE == 2 * max_exact ** 2 and nb - max_exact == 8
    n2 = n * n
    large = max_exact + sum((n2 >= 2 ** k).astype(jnp.int32) for k in range(7, 31))
    large = jnp.minimum(large, nb - 1)
    return ret + jnp.where(n < max_exact, n, large)


def _t5_bias_tiles(rel_table):
    r = jnp.arange(Q_BLOCK)
    rel = ((jnp.arange(3) - 1)[:, None, None] * Q_BLOCK
           + r[None, :, None] - r[None, None, :])
    bucket = jnp.where(jnp.abs(rel) <= WINDOW, _t5_bucket(rel), -1).astype(jnp.int32)
    return pl.pallas_call(
        _bias_kernel,
        out_shape=jax.ShapeDtypeStruct((N_HEADS, _KEYS_B, _TQ_B), _F32),
        in_specs=[pl.BlockSpec(memory_space=pltpu.VMEM),
                  pl.BlockSpec(memory_space=pltpu.VMEM)],
        out_specs=pl.BlockSpec(memory_space=pltpu.VMEM),
        compiler_params=pltpu.CompilerParams(vmem_limit_bytes=_VMEM_LIMIT_BYTES),
        name="t5_bias",
    )(jnp.broadcast_to(rel_table.T[:, :, None], (N_HEADS, N_BUCKETS, Q_BLOCK)), bucket)


def _norm_rope(xh, gain, cos, sin):
    ms = jnp.mean(xh * xh, axis=0, keepdims=True)
    y = xh * lax.rsqrt(ms + EPS) * gain
    q = HEAD_DIM // 4
    rot = jnp.concatenate([-y[q:2 * q], y[0:q], -y[3 * q:4 * q], y[2 * q:3 * q]], axis=0)
    return y * cos + rot * sin


def _inproj_kernel(x_ref, shift_ref, scale_ref, gpre_ref, w_ref, cos_ref, sin_ref,
                   qn_ref, kn_ref,
                   qa_ref, ka_ref, va_ref, ga_ref, qb_ref, kb_ref, vb_ref, gb_ref):
    x = x_ref[0]
    ms = jnp.mean(x * x, axis=-1, keepdims=True)
    h = x * lax.rsqrt(ms + EPS) * gpre_ref[...]
    h = h * (1.0 + scale_ref[0]) + shift_ref[0]
    hb = h.astype(_BF16)

    def proj(group):
        w = jnp.concatenate([w_ref[lo:lo + n, :] for lo, n in group], axis=0)
        return lax.dot_general(w, hb, _NT, preferred_element_type=_F32)

    cos = cos_ref[...]
    sin = sin_ref[...]
    qn = qn_ref[...]
    kn = kn_ref[...]

    def store_tiled(ref, rows, val):
        tq = ref.shape[3]
        for j in range(ref.shape[1]):
            ref[0, j, rows, :] = val[:, j * tq:(j + 1) * tq]

    qk = proj(_QK_GROUP)
    for hd in range(N_HEADS):
        rows = slice(hd * HEAD_DIM, (hd + 1) * HEAD_DIM)
        store_tiled(qa_ref, rows, (_norm_rope(qk[rows], qn, cos, sin) * Q_SCALE).astype(_BF16))
    store_tiled(qb_ref, slice(0, WIDTH), (qk[WIDTH:2 * WIDTH] * Q_SCALE).astype(_BF16))
    ka = qk[2 * WIDTH:2 * WIDTH + KV_WIDTH]
    ka = jnp.concatenate(
        [_norm_rope(ka[hd * HEAD_DIM:(hd + 1) * HEAD_DIM], kn, cos, sin) for hd in range(N_KV)],
        axis=0)
    ka_ref[0] = ka.T.astype(_BF16)
    kb_ref[0] = qk[2 * WIDTH + KV_WIDTH:].T.astype(_BF16)

    gates = _silu(proj(_GATE_GROUP))
    store_tiled(ga_ref, slice(0, WIDTH), gates[:WIDTH].astype(_BF16))
    store_tiled(gb_ref, slice(0, WIDTH), gates[WIDTH:].astype(_BF16))

    v = proj(_V_GROUP)
    va_ref[0] = v[:KV_WIDTH].astype(_BF16)
    store_tiled(vb_ref, slice(0, KV_WIDTH), v[KV_WIDTH:].astype(_BF16))


def _rope_tables(seq):
    rows = seq // GRID_W
    row = jnp.repeat(jnp.arange(rows), GRID_W).astype(_F32)
    col = jnp.tile(jnp.arange(GRID_W), rows).astype(_F32)
    half = HEAD_DIM // 2
    freqs = ROPE_THETA ** (-jnp.arange(0, half, 2, dtype=_F32) / half)

    def ang2(pos):
        ang = pos[:, None] * freqs[None, :]
        return jnp.concatenate([ang, ang], axis=-1)

    ang = jnp.concatenate([ang2(row), ang2(col)], axis=-1)
    return jnp.cos(ang).T, jnp.sin(ang).T


def _in_proj(x, shift, scale, g_pre, w_in_t, cos_t, sin_t, qn, kn):
    bsz, seq, d = x.shape
    tm = _TM_IN
    tok_t = lambda n: pl.BlockSpec((1, n, tm), lambda b, i: (b, 0, i))
    tok_m = lambda n: pl.BlockSpec((1, tm, n), lambda b, i: (b, i, 0))
    per_b = pl.BlockSpec((1, 1, d), lambda b, i: (b, 0, 0))
    tab = pl.BlockSpec((HEAD_DIM, tm), lambda b, i: (0, i))
    gain = pl.BlockSpec((HEAD_DIM, tm), lambda b, i: (0, 0))
    shp_t = lambda n: jax.ShapeDtypeStruct((bsz, n, seq), _BF16)
    shp_m = lambda n: jax.ShapeDtypeStruct((bsz, seq, n), _BF16)
    tiled = pl.BlockSpec((1, tm // _TQ_A, WIDTH, _TQ_A), lambda b, i: (b, i, 0, 0))
    shp_tiled = jax.ShapeDtypeStruct((bsz, seq // _TQ_A, WIDTH, _TQ_A), _BF16)
    v_blocks = pl.BlockSpec((1, tm // Q_BLOCK, KV_WIDTH, Q_BLOCK), lambda b, i: (b, i, 0, 0))
    shp_v_blocks = jax.ShapeDtypeStruct((bsz, seq // Q_BLOCK, KV_WIDTH, Q_BLOCK), _BF16)
    return pl.pallas_call(
        _inproj_kernel,
        out_shape=(shp_tiled, shp_m(KV_WIDTH), shp_t(KV_WIDTH), shp_tiled,
                   shp_tiled, shp_m(KV_WIDTH), shp_v_blocks, shp_tiled),
        grid=(bsz, seq // tm),
        in_specs=[pl.BlockSpec((1, tm, d), lambda b, i: (b, i, 0)),
                  per_b, per_b,
                  pl.BlockSpec((1, d), lambda b, i: (0, 0)),
                  pl.BlockSpec((IN_COLS, d), lambda b, i: (0, 0)),
                  tab, tab, gain, gain],
        out_specs=(tiled, tok_m(KV_WIDTH), tok_t(KV_WIDTH), tiled,
                   tiled, tok_m(KV_WIDTH), v_blocks, tiled),
        compiler_params=_params(2),
        name="in_proj",
    )(x, shift, scale, g_pre.reshape(1, d), w_in_t, cos_t, sin_t,
      jnp.broadcast_to(qn[:, None], (HEAD_DIM, tm)),
      jnp.broadcast_to(kn[:, None], (HEAD_DIM, tm)))


_PAIRS = N_HEADS // 2
_ONES_ROWS = 16


def _head_rows(hd):
    return slice(hd * HEAD_DIM, (hd + 1) * HEAD_DIM)


def _pair_q(head, pair):
    def padded(hd):
        qh = head(hd)
        z = jnp.zeros_like(qh)
        return jnp.concatenate([qh, z] if hd // GROUP == 0 else [z, qh], axis=0)
    return jnp.concatenate([padded(2 * pair), padded(2 * pair + 1)], axis=1)


def _pair_out(o, inv_l, gate, pair, tq):
    o = o * inv_l
    g0 = gate(2 * pair).astype(_F32)
    g1 = gate(2 * pair + 1).astype(_F32)
    return jnp.concatenate([o[:, :tq] * g0, o[:, tq:] * g1], axis=0).T.astype(_BF16)


def _pair_lanes(pair):
    return slice(pair * 2 * HEAD_DIM, (pair + 1) * 2 * HEAD_DIM)


def _v_with_ones(v):
    return jnp.concatenate([v, jnp.ones((_ONES_ROWS, v.shape[1]), v.dtype)], axis=0)


def _mixers_kernel(qa_ref, ka_ref, va_ref, ga_ref, qb_ref, kb_ref, vb_ref, gb_ref,
                   sink_ref, bias_ref, o_ref, s0_ref, s1_ref):
    n_tiles, tq = qa_ref.shape[1], qa_ref.shape[3]
    seq = ka_ref.shape[1]
    n_blocks = seq // Q_BLOCK
    s_refs = (s0_ref, s1_ref)
    a_chunks = [slice(c * _TK_A, (c + 1) * _TK_A) for c in range(seq // _TK_A)]
    items = [("a", p) for p in range(_PAIRS)] + [("b", p) for p in range(_PAIRS)]

    def staged(start, size, zero):
        return pl.ds(pl.multiple_of(zero + start, size), size)

    def window_blocks(tile):
        first = (tq // Q_BLOCK) * tile - 1
        return [jnp.clip(first + j, 0, n_blocks - 1) for j in range(_KEYS_B // Q_BLOCK)]

    def score_chunks(kind, pair, buf, tile, zero):
        q_ref = qa_ref if kind == "a" else qb_ref

        def weights():
            return _pair_q(lambda hd: q_ref[0, tile, _head_rows(hd), :], pair)

        def global_chunk(rows):
            s = jnp.dot(ka_ref[0, rows, :], weights(), preferred_element_type=_F32)
            s_refs[buf][staged(rows.start, _TK_A, zero), :] = s
            return jnp.max(s, axis=0, keepdims=True)

        def window():
            k = jnp.concatenate(
                [kb_ref[0, pl.ds(pl.multiple_of(blk * Q_BLOCK, Q_BLOCK), Q_BLOCK), :]
                 for blk in window_blocks(tile)], axis=0)
            bias = jnp.concatenate([bias_ref[2 * pair], bias_ref[2 * pair + 1]], axis=1)
            s = jnp.dot(k, weights(), preferred_element_type=_F32) + bias
            s = jnp.concatenate(
                [jnp.where(tile == 0, MASK_VALUE, s[:Q_BLOCK]),
                 s[Q_BLOCK:_KEYS_B - Q_BLOCK],
                 jnp.where(tile == n_tiles - 1, MASK_VALUE, s[_KEYS_B - Q_BLOCK:])], axis=0)
            s_refs[buf][staged(0, _KEYS_B, zero), :] = s
            return jnp.maximum(jnp.max(s, axis=0, keepdims=True), sink_ref[pair] * LOG2E)

        if kind == "a":
            return [functools.partial(global_chunk, rows) for rows in a_chunks]
        return [window]

    def pv_chunks(kind, pair, buf, tile, zero, m):
        kv = _head_rows(pair // (GROUP // 2))

        def global_chunk(rows):
            e = jnp.exp2(s_refs[buf][staged(rows.start, _TK_A, zero), :] - m)
            return jnp.dot(_v_with_ones(va_ref[0, kv, rows]), e.astype(_BF16),
                           preferred_element_type=_F32)

        def window():
            e = jnp.exp2(s_refs[buf][staged(0, _KEYS_B, zero), :] - m)
            v = jnp.concatenate([vb_ref[0, blk] for blk in window_blocks(tile)], axis=1)
            return jnp.dot(_v_with_ones(v[kv, :]), e.astype(_BF16), preferred_element_type=_F32)

        if kind == "a":
            return [functools.partial(global_chunk, rows) for rows in a_chunks]
        return [window]

    def finish(kind, pair, tile, o, m):
        l = o[HEAD_DIM:HEAD_DIM + 1]
        if kind == "b":
            l = l + jnp.exp2(sink_ref[pair] * LOG2E - m)
        g_ref = ga_ref if kind == "a" else gb_ref
        lane0 = (0 if kind == "a" else WIDTH) + pair * 2 * HEAD_DIM
        o_ref[0, pl.ds(pl.multiple_of(tile * tq, tq), tq), lane0:lane0 + 2 * HEAD_DIM] = _pair_out(
            o[:HEAD_DIM], 1.0 / l, lambda hd: g_ref[0, tile, _head_rows(hd), :], pair, tq)

    def col_max(parts):
        return functools.reduce(jnp.maximum, parts)

    def tile_body(tile, m):
        zero = jnp.minimum(tile, 0)
        next_tile = jnp.minimum(tile + 1, n_tiles - 1)
        for idx, (kind, pair) in enumerate(items):
            if idx + 1 < len(items):
                nxt = score_chunks(*items[idx + 1], (idx + 1) % 2, tile, zero)
            else:
                nxt = score_chunks(*items[0], 0, next_tile, zero)
            cur = pv_chunks(kind, pair, idx % 2, tile, zero, m)
            parts, o = [], None
            for i in range(max(len(nxt), len(cur))):
                if i < len(nxt):
                    parts.append(nxt[i]())
                if i < len(cur):
                    pv = cur[i]()
                    o = pv if o is None else o + pv
            finish(kind, pair, tile, o, m)
            m = col_max(parts)
        return m

    first = jnp.minimum(pl.program_id(0), 0)
    m0 = col_max([chunk() for chunk in score_chunks(*items[0], 0, first, first)])
    lax.fori_loop(0, n_tiles, tile_body, m0)


def _mixers(qa, ka, va, ga, qb, kb, vb, gb, sink, bias):
    bsz, n_tiles, _, tq = qa.shape
    seq = ka.shape[1]
    tiled = pl.BlockSpec((1, n_tiles, WIDTH, tq), lambda b: (b, 0, 0, 0))
    k_spec = pl.BlockSpec((1, seq, KV_WIDTH), lambda b: (b, 0, 0))
    const = lambda shape: pl.BlockSpec(shape, lambda b: (0,) * len(shape))
    return pl.pallas_call(
        _mixers_kernel,
        out_shape=jax.ShapeDtypeStruct((bsz, seq, 2 * WIDTH), _BF16),
        grid=(bsz,),
        in_specs=[tiled, k_spec, pl.BlockSpec((1, KV_WIDTH, seq), lambda b: (b, 0, 0)), tiled,
                  tiled, k_spec,
                  pl.BlockSpec((1, seq // Q_BLOCK, KV_WIDTH, Q_BLOCK), lambda b: (b, 0, 0, 0)),
                  tiled,
                  const((_PAIRS, 1, 2 * tq)), const((N_HEADS, _KEYS_B, tq))],
        out_specs=pl.BlockSpec((1, seq, 2 * WIDTH), lambda b: (b, 0, 0)),
        scratch_shapes=[pltpu.VMEM((seq, 2 * tq), _F32), pltpu.VMEM((seq, 2 * tq), _F32)],
        compiler_params=_params(1),
        name="mixers",
    )(qa, ka, va, ga, qb, kb, vb, gb,
      jnp.broadcast_to(sink[:, None], (N_HEADS, tq)).reshape(_PAIRS, 1, 2 * tq), bias)


def _outproj_kernel(a_ref, w_ref, x_ref, gate_ref, gpost_ref, o_ref):
    w = w_ref[...].astype(_BF16)
    for c in range(a_ref.shape[1] // _TM_OUT_CHUNK):
        rows = slice(c * _TM_OUT_CHUNK, (c + 1) * _TM_OUT_CHUNK)
        y = jnp.dot(a_ref[0, rows, :], w, preferred_element_type=_F32)
        ms = jnp.mean(y * y, axis=-1, keepdims=True)
        yn = y * lax.rsqrt(ms + EPS) * gpost_ref[...]
        o_ref[0, rows, :] = x_ref[0, rows, :] + gate_ref[0] * yn


def _out_proj(o, w_out, x, gate, g_post):
    bsz, seq, d = x.shape
    tm = _TM_OUT
    xo = pl.BlockSpec((1, tm, d), lambda b, i: (b, i, 0))
    return pl.pallas_call(
        _outproj_kernel,
        out_shape=jax.ShapeDtypeStruct((bsz, seq, d), _F32),
        grid=(bsz, seq // tm),
        in_specs=[pl.BlockSpec((1, tm, 2 * WIDTH), lambda b, i: (b, i, 0)),
                  pl.BlockSpec((2 * WIDTH, d), lambda b, i: (0, 0)),
                  xo,
                  pl.BlockSpec((1, 1, d), lambda b, i: (b, 0, 0)),
                  pl.BlockSpec((1, d), lambda b, i: (0, 0))],
        out_specs=xo,
        compiler_params=_params(2),
        name="out_proj",
    )(o, w_out, x, gate, g_post.reshape(1, d))


def kernel(x, c, w_ada, b_ada, g_pre, g_post, w_in, qn_a, kn_a, sink_b, w_out, rel_table):
    bsz, seq, d = x.shape
    depth = w_ada.shape[0]
    cos_t, sin_t = _rope_tables(seq)
    bias = _t5_bias_tiles(rel_table)
    for l in range(depth):
        mod = _adaln_mod(c, w_ada[l], b_ada[l])
        shift, scale, gate = (mod[:, i * d:(i + 1) * d].reshape(bsz, 1, d) for i in range(3))
        w_in_t = w_in[l].T.astype(_BF16)
        qa, ka, va, ga, qb, kb, vb, gb = _in_proj(
            x, shift, scale, g_pre[l], w_in_t, cos_t, sin_t, qn_a[l], kn_a[l])
        o = _mixers(qa, ka, va, ga, qb, kb, vb, gb, sink_b[l], bias)
        x = _out_proj(o, w_out[l], x, gate, g_post[l])
    return x
```

```python
import functools

import jax
import jax.numpy as jnp
from jax import lax
from jax.experimental import pallas as pl
from jax.experimental.pallas import tpu as pltpu

HEAD_DIM = 64
N_HEADS = 8
N_KV = 2
GROUP = N_HEADS // N_KV
WIDTH = N_HEADS * HEAD_DIM
KV_WIDTH = N_KV * HEAD_DIM
IN_COLS = 2 * (2 * WIDTH + 2 * KV_WIDTH)
Q_BLOCK = 128
WINDOW = 128
GRID_W = 64
ROPE_THETA = 10000.0
N_BUCKETS = 32
MAX_DISTANCE = 128
EPS = 1e-6
MASK_VALUE = -1e30
LOG2E = 1.4426950408889634
Q_SCALE = HEAD_DIM ** -0.5 * LOG2E

_QA, _KA, _VA, _GA = 0, WIDTH, WIDTH + KV_WIDTH, WIDTH + 2 * KV_WIDTH
_HALF = 2 * WIDTH + 2 * KV_WIDTH
_QB, _KB, _VB, _GB = _HALF + _QA, _HALF + _KA, _HALF + _VA, _HALF + _GA
_QK_GROUP = ((_QA, WIDTH), (_QB, WIDTH), (_KA, KV_WIDTH), (_KB, KV_WIDTH))
_GATE_GROUP = ((_GA, WIDTH), (_GB, WIDTH))
_V_GROUP = ((_VA, KV_WIDTH), (_VB, KV_WIDTH))

_F32 = jnp.float32
_BF16 = jnp.bfloat16

_VMEM_LIMIT_BYTES = 48 * 1024 * 1024

_TM_IN = 1024
_TQ_A = 256
_TK_A = 256
_KEYS_B = _TQ_A + 2 * WINDOW
_TM_OUT = 1024
_TM_OUT_CHUNK = 256
_TN_MOD = 512

_NT = (((1,), (1,)), ((), ()))


def _silu(v):
    return 0.5 * v * (1.0 + jnp.tanh(0.5 * v))


def _params(n_grid):
    return pltpu.CompilerParams(
        dimension_semantics=("arbitrary",) * n_grid,
        vmem_limit_bytes=_VMEM_LIMIT_BYTES)


def _mod_kernel(c_ref, w_ref, b_ref, o_ref):
    ca = _silu(c_ref[...])
    o_ref[...] = jnp.dot(ca.astype(_BF16), w_ref[...].astype(_BF16),
                         preferred_element_type=_F32) + b_ref[...]


def _adaln_mod(c, w, b):
    bsz, d = c.shape
    n = w.shape[1]
    return pl.pallas_call(
        _mod_kernel,
        out_shape=jax.ShapeDtypeStruct((bsz, n), _F32),
        grid=(n // _TN_MOD,),
        in_specs=[pl.BlockSpec((bsz, d), lambda j: (0, 0)),
                  pl.BlockSpec((d, _TN_MOD), lambda j: (0, j)),
                  pl.BlockSpec((1, _TN_MOD), lambda j: (0, j))],
        out_specs=pl.BlockSpec((bsz, _TN_MOD), lambda j: (0, j)),
        compiler_params=_params(1),
        name="adaln_mod",
    )(c, w, b.reshape(1, n))


def _bias_kernel(table_ref, bucket_ref, o_ref):
    masked = jnp.full((Q_BLOCK, Q_BLOCK), MASK_VALUE, _F32)
    for h in range(N_HEADS):
        table = table_ref[h] * LOG2E
        tiles = []
        for d in range(3):
            bucket = bucket_ref[d]
            acc = masked
            for b in range(N_BUCKETS):
                acc = jnp.where(bucket == b, table[b:b + 1, :], acc)
            tiles.append(acc)
        lo, mid, hi = tiles
        for jb, (left, right) in enumerate(((lo, masked), (mid, lo), (hi, mid), (masked, hi))):
            rows = slice(jb * Q_BLOCK, (jb + 1) * Q_BLOCK)
            o_ref[h, rows, 0:Q_BLOCK] = left
            o_ref[h, rows, Q_BLOCK:2 * Q_BLOCK] = right


def _t5_bucket(rel):
    nb = N_BUCKETS // 2
    max_exact = nb // 2
    ret = (rel > 0).astype(jnp.int32) * nb
    n = jnp.abs(rel)
    assert MAX_DISTANCE == 2 * max_exact ** 2 and nb - max_exact == 8
    n2 = n * n
    large = max_exact + sum((n2 >= 2 ** k).astype(jnp.int32) for k in range(7, 31))
    large = jnp.minimum(large, nb - 1)
    return ret + jnp.where(n < max_exact, n, large)


def _t5_bias_tiles(rel_table):
    r = jnp.arange(Q_BLOCK)
    rel = ((jnp.arange(3) - 1)[:, None, None] * Q_BLOCK
           + r[None, :, None] - r[None, None, :])
    bucket = jnp.where(jnp.abs(rel) <= WINDOW, _t5_bucket(rel), -1).astype(jnp.int32)
    return pl.pallas_call(
        _bias_kernel,
        out_shape=jax.ShapeDtypeStruct((N_HEADS, _KEYS_B, _TQ_A), _F32),
        in_specs=[pl.BlockSpec(memory_space=pltpu.VMEM),
                  pl.BlockSpec(memory_space=pltpu.VMEM)],
        out_specs=pl.BlockSpec(memory_space=pltpu.VMEM),
        compiler_params=pltpu.CompilerParams(vmem_limit_bytes=_VMEM_LIMIT_BYTES),
        name="t5_bias",
    )(jnp.broadcast_to(rel_table.T[:, :, None], (N_HEADS, N_BUCKETS, Q_BLOCK)), bucket)


def _norm_rope(xh, gain, cos, sin):
    ms = jnp.mean(xh * xh, axis=0, keepdims=True)
    y = xh * lax.rsqrt(ms + EPS) * gain
    q = HEAD_DIM // 4
    rot = jnp.concatenate([-y[q:2 * q], y[0:q], -y[3 * q:4 * q], y[2 * q:3 * q]], axis=0)
    return y * cos + rot * sin


def _inproj_kernel(x_ref, shift_ref, scale_ref, gpre_ref, w_ref, cos_ref, sin_ref,
                   qn_ref, kn_ref,
                   qa_ref, ka_ref, va_ref, ga_ref, qb_ref, kb_ref, vb_ref, gb_ref):
    x = x_ref[0]
    ms = jnp.mean(x * x, axis=-1, keepdims=True)
    h = x * lax.rsqrt(ms + EPS) * gpre_ref[...]
    h = h * (1.0 + scale_ref[0]) + shift_ref[0]
    hb = h.astype(_BF16)

    def proj(group):
        w = jnp.concatenate([w_ref[lo:lo + n, :] for lo, n in group], axis=0)
        return lax.dot_general(w, hb, _NT, preferred_element_type=_F32)

    cos = cos_ref[...]
    sin = sin_ref[...]
    qn = qn_ref[...]
    kn = kn_ref[...]

    def store_tiled(ref, rows, val):
        tq = ref.shape[3]
        for j in range(ref.shape[1]):
            ref[0, j, rows, :] = val[:, j * tq:(j + 1) * tq]

    qk = proj(_QK_GROUP)
    for hd in range(N_HEADS):
        rows = slice(hd * HEAD_DIM, (hd + 1) * HEAD_DIM)
        store_tiled(qa_ref, rows, (_norm_rope(qk[rows], qn, cos, sin) * Q_SCALE).astype(_BF16))
    store_tiled(qb_ref, slice(0, WIDTH), (qk[WIDTH:2 * WIDTH] * Q_SCALE).astype(_BF16))
    ka = qk[2 * WIDTH:2 * WIDTH + KV_WIDTH]
    ka = jnp.concatenate(
        [_norm_rope(ka[hd * HEAD_DIM:(hd + 1) * HEAD_DIM], kn, cos, sin) for hd in range(N_KV)],
        axis=0)
    ka_ref[0] = ka.T.astype(_BF16)
    kb_ref[0] = qk[2 * WIDTH + KV_WIDTH:].T.astype(_BF16)

    gates = _silu(proj(_GATE_GROUP))
    store_tiled(ga_ref, slice(0, WIDTH), gates[:WIDTH].astype(_BF16))
    store_tiled(gb_ref, slice(0, WIDTH), gates[WIDTH:].astype(_BF16))

    v = proj(_V_GROUP)
    va_ref[0] = v[:KV_WIDTH].astype(_BF16)
    store_tiled(vb_ref, slice(0, KV_WIDTH), v[KV_WIDTH:].astype(_BF16))


def _rope_tables(seq):
    rows = seq // GRID_W
    row = jnp.repeat(jnp.arange(rows), GRID_W).astype(_F32)
    col = jnp.tile(jnp.arange(GRID_W), rows).astype(_F32)
    half = HEAD_DIM // 2
    freqs = ROPE_THETA ** (-jnp.arange(0, half, 2, dtype=_F32) / half)

    def ang2(pos):
        ang = pos[:, None] * freqs[None, :]
        return jnp.concatenate([ang, ang], axis=-1)

    ang = jnp.concatenate([ang2(row), ang2(col)], axis=-1)
    return jnp.cos(ang).T, jnp.sin(ang).T


def _in_proj(x, shift, scale, g_pre, w_in_t, cos_t, sin_t, qn, kn):
    bsz, seq, d = x.shape
    tm = _TM_IN
    tok_t = lambda n: pl.BlockSpec((1, n, tm), lambda b, i: (b, 0, i))
    tok_m = lambda n: pl.BlockSpec((1, tm, n), lambda b, i: (b, i, 0))
    per_b = pl.BlockSpec((1, 1, d), lambda b, i: (b, 0, 0))
    tab = pl.BlockSpec((HEAD_DIM, tm), lambda b, i: (0, i))
    gain = pl.BlockSpec((HEAD_DIM, tm), lambda b, i: (0, 0))
    shp_t = lambda n: jax.ShapeDtypeStruct((bsz, n, seq), _BF16)
    shp_m = lambda n: jax.ShapeDtypeStruct((bsz, seq, n), _BF16)
    tiled = pl.BlockSpec((1, tm // _TQ_A, WIDTH, _TQ_A), lambda b, i: (b, i, 0, 0))
    shp_tiled = jax.ShapeDtypeStruct((bsz, seq // _TQ_A, WIDTH, _TQ_A), _BF16)
    v_blocks = pl.BlockSpec((1, tm // Q_BLOCK, KV_WIDTH, Q_BLOCK), lambda b, i: (b, i, 0, 0))
    shp_v_blocks = jax.ShapeDtypeStruct((bsz, seq // Q_BLOCK, KV_WIDTH, Q_BLOCK), _BF16)
    return pl.pallas_call(
        _inproj_kernel,
        out_shape=(shp_tiled, shp_m(KV_WIDTH), shp_t(KV_WIDTH), shp_tiled,
                   shp_tiled, shp_m(KV_WIDTH), shp_v_blocks, shp_tiled),
        grid=(bsz, seq // tm),
        in_specs=[pl.BlockSpec((1, tm, d), lambda b, i: (b, i, 0)),
                  per_b, per_b,
                  pl.BlockSpec((1, d), lambda b, i: (0, 0)),
                  pl.BlockSpec((IN_COLS, d), lambda b, i: (0, 0)),
                  tab, tab, gain, gain],
        out_specs=(tiled, tok_m(KV_WIDTH), tok_t(KV_WIDTH), tiled,
                   tiled, tok_m(KV_WIDTH), v_blocks, tiled),
        compiler_params=_params(2),
        name="in_proj",
    )(x, shift, scale, g_pre.reshape(1, d), w_in_t, cos_t, sin_t,
      jnp.broadcast_to(qn[:, None], (HEAD_DIM, tm)),
      jnp.broadcast_to(kn[:, None], (HEAD_DIM, tm)))


_PAIRS = N_HEADS // 2
_ONES_ROWS = 16
_LOOKAHEAD = 2


def _head_rows(hd):
    return slice(hd * HEAD_DIM, (hd + 1) * HEAD_DIM)


def _pair_q(head, pair):
    def padded(hd):
        qh = head(hd)
        z = jnp.zeros_like(qh)
        return jnp.concatenate([qh, z] if hd // GROUP == 0 else [z, qh], axis=0)
    return jnp.concatenate([padded(2 * pair), padded(2 * pair + 1)], axis=1)


def _pair_out(o, inv_l, gate, pair, tq):
    o = o * inv_l
    g0 = gate(2 * pair).astype(_F32)
    g1 = gate(2 * pair + 1).astype(_F32)
    return jnp.concatenate([o[:, :tq] * g0, o[:, tq:] * g1], axis=0).T.astype(_BF16)


def _v_with_ones(v):
    return jnp.concatenate([v, jnp.ones((_ONES_ROWS, v.shape[1]), v.dtype)], axis=0)


def _mixers_kernel(qa_ref, ka_ref, va_ref, ga_ref, qb_ref, kb_ref, vb_ref, gb_ref,
                   sink_ref, bias_ref, o_ref, sa0_ref, sa1_ref, sb0_ref, sb1_ref):
    n_tiles, tq = qa_ref.shape[1], qa_ref.shape[3]
    seq = ka_ref.shape[1]
    n_blocks = seq // Q_BLOCK
    a_chunks = [slice(c * _TK_A, (c + 1) * _TK_A) for c in range(seq // _TK_A)]
    items = [(kind, p) for p in range(_PAIRS) for kind in ("a", "b")]
    bufs = (sa0_ref, sb0_ref, sa1_ref, sb1_ref) * (_PAIRS // 2)

    def staged(start, size, zero):
        return pl.ds(pl.multiple_of(zero + start, size), size)

    def window_blocks(tile):
        first = (tq // Q_BLOCK) * tile - 1
        return [jnp.clip(first + j, 0, n_blocks - 1) for j in range(_KEYS_B // Q_BLOCK)]

    def score_chunks(kind, pair, buf, tile, zero):
        q_ref = qa_ref if kind == "a" else qb_ref

        def weights():
            return _pair_q(lambda hd: q_ref[0, tile, _head_rows(hd), :], pair)

        def global_chunk(rows):
            s = jnp.dot(ka_ref[0, rows, :], weights(), preferred_element_type=_F32)
            buf[staged(rows.start, _TK_A, zero), :] = s
            return jnp.max(s, axis=0, keepdims=True)

        def window():
            k = jnp.concatenate(
                [kb_ref[0, pl.ds(pl.multiple_of(blk * Q_BLOCK, Q_BLOCK), Q_BLOCK), :]
                 for blk in window_blocks(tile)], axis=0)
            bias = jnp.concatenate([bias_ref[2 * pair], bias_ref[2 * pair + 1]], axis=1)
            s = jnp.dot(k, weights(), preferred_element_type=_F32) + bias
            s = jnp.concatenate(
                [jnp.where(tile == 0, MASK_VALUE, s[:Q_BLOCK]),
                 s[Q_BLOCK:_KEYS_B - Q_BLOCK],
                 jnp.where(tile == n_tiles - 1, MASK_VALUE, s[_KEYS_B - Q_BLOCK:])], axis=0)
            buf[staged(0, _KEYS_B, zero), :] = s
            return jnp.maximum(jnp.max(s, axis=0, keepdims=True), sink_ref[pair] * LOG2E)

        if kind == "a":
            return [functools.partial(global_chunk, rows) for rows in a_chunks]
        return [window]

    def pv_chunks(kind, pair, buf, tile, zero, m):
        kv = _head_rows(pair // (GROUP // 2))

        def global_chunk(rows):
            e = jnp.exp2(buf[staged(rows.start, _TK_A, zero), :] - m)
            return jnp.dot(_v_with_ones(va_ref[0, kv, rows]), e.astype(_BF16),
                           preferred_element_type=_F32)

        def window():
            e = jnp.exp2(buf[staged(0, _KEYS_B, zero), :] - m)
            v = jnp.concatenate([vb_ref[0, blk] for blk in window_blocks(tile)], axis=1)
            return jnp.dot(_v_with_ones(v[kv, :]), e.astype(_BF16), preferred_element_type=_F32)

        if kind == "a":
            return [functools.partial(global_chunk, rows) for rows in a_chunks]
        return [window]

    def finish(kind, pair, tile, o, m):
        l = o[HEAD_DIM:HEAD_DIM + 1]
        if kind == "b":
            l = l + jnp.exp2(sink_ref[pair] * LOG2E - m)
        g_ref = ga_ref if kind == "a" else gb_ref
        lane0 = (0 if kind == "a" else WIDTH) + pair * 2 * HEAD_DIM
        o_ref[0, pl.ds(pl.multiple_of(tile * tq, tq), tq), lane0:lane0 + 2 * HEAD_DIM] = _pair_out(
            o[:HEAD_DIM], 1.0 / l, lambda hd: g_ref[0, tile, _head_rows(hd), :], pair, tq)

    def col_max(parts):
        return functools.reduce(jnp.maximum, parts)

    def tile_body(tile, carry):
        zero = jnp.minimum(tile, 0)
        next_tile = jnp.minimum(tile + 1, n_tiles - 1)
        col_maxes = list(carry)
        for idx, (kind, pair) in enumerate(items):
            ahead = (idx + _LOOKAHEAD) % len(items)
            ahead_tile = tile if idx + _LOOKAHEAD < len(items) else next_tile
            nxt = score_chunks(*items[ahead], bufs[ahead], ahead_tile, zero)
            m = col_maxes[idx]
            cur = pv_chunks(kind, pair, bufs[idx], tile, zero, m)
            parts, o = [], None
            for i in range(max(len(nxt), len(cur))):
                if i < len(nxt):
                    parts.append(nxt[i]())
                if i < len(cur):
                    pv = cur[i]()
                    o = pv if o is None else o + pv
            finish(kind, pair, tile, o, m)
            col_maxes.append(col_max(parts))
        return tuple(col_maxes[len(items):])

    first = jnp.minimum(pl.program_id(0), 0)
    carry0 = tuple(
        col_max([chunk() for chunk in score_chunks(*items[i], bufs[i], first, first)])
        for i in range(_LOOKAHEAD))
    lax.fori_loop(0, n_tiles, tile_body, carry0)


def _mixers(qa, ka, va, ga, qb, kb, vb, gb, sink, bias):
    bsz, n_tiles, _, tq = qa.shape
    seq = ka.shape[1]
    tiled = pl.BlockSpec((1, n_tiles, WIDTH, tq), lambda b: (b, 0, 0, 0))
    k_spec = pl.BlockSpec((1, seq, KV_WIDTH), lambda b: (b, 0, 0))
    const = lambda shape: pl.BlockSpec(shape, lambda b: (0,) * len(shape))
    return pl.pallas_call(
        _mixers_kernel,
        out_shape=jax.ShapeDtypeStruct((bsz, seq, 2 * WIDTH), _BF16),
        grid=(bsz,),
        in_specs=[tiled, k_spec, pl.BlockSpec((1, KV_WIDTH, seq), lambda b: (b, 0, 0)), tiled,
                  tiled, k_spec,
                  pl.BlockSpec((1, seq // Q_BLOCK, KV_WIDTH, Q_BLOCK), lambda b: (b, 0, 0, 0)),
                  tiled,
                  const((_PAIRS, 1, 2 * tq)), const((N_HEADS, _KEYS_B, tq))],
        out_specs=pl.BlockSpec((1, seq, 2 * WIDTH), lambda b: (b, 0, 0)),
        scratch_shapes=([pltpu.VMEM((seq, 2 * tq), _F32)] * 2
                        + [pltpu.VMEM((_KEYS_B, 2 * tq), _F32)] * 2),
        compiler_params=_params(1),
        name="mixers",
    )(qa, ka, va, ga, qb, kb, vb, gb,
      jnp.broadcast_to(sink[:, None], (N_HEADS, tq)).reshape(_PAIRS, 1, 2 * tq), bias)


def _outproj_kernel(a_ref, w_ref, x_ref, gate_ref, gpost_ref, o_ref):
    w = w_ref[...].astype(_BF16)
    for c in range(a_ref.shape[1] // _TM_OUT_CHUNK):
        rows = slice(c * _TM_OUT_CHUNK, (c + 1) * _TM_OUT_CHUNK)
        y = jnp.dot(a_ref[0, rows, :], w, preferred_element_type=_F32)
        ms = jnp.mean(y * y, axis=-1, keepdims=True)
        yn = y * lax.rsqrt(ms + EPS) * gpost_ref[...]
        o_ref[0, rows, :] = x_ref[0, rows, :] + gate_ref[0] * yn


def _out_proj(o, w_out, x, gate, g_post):
    bsz, seq, d = x.shape
    tm = _TM_OUT
    xo = pl.BlockSpec((1, tm, d), lambda b, i: (b, i, 0))
    return pl.pallas_call(
        _outproj_kernel,
        out_shape=jax.ShapeDtypeStruct((bsz, seq, d), _F32),
        grid=(bsz, seq // tm),
        in_specs=[pl.BlockSpec((1, tm, 2 * WIDTH), lambda b, i: (b, i, 0)),
                  pl.BlockSpec((2 * WIDTH, d), lambda b, i: (0, 0)),
                  xo,
                  pl.BlockSpec((1, 1, d), lambda b, i: (b, 0, 0)),
                  pl.BlockSpec((1, d), lambda b, i: (0, 0))],
        out_specs=xo,
        compiler_params=_params(2),
        name="out_proj",
    )(o, w_out, x, gate, g_post.reshape(1, d))


def kernel(x, c, w_ada, b_ada, g_pre, g_post, w_in, qn_a, kn_a, sink_b, w_out, rel_table):
    bsz, seq, d = x.shape
    depth = w_ada.shape[0]
    cos_t, sin_t = _rope_tables(seq)
    bias = _t5_bias_tiles(rel_table)
    for l in range(depth):
        mod = _adaln_mod(c, w_ada[l], b_ada[l])
        shift, scale, gate = (mod[:, i * d:(i + 1) * d].reshape(bsz, 1, d) for i in range(3))
        w_in_t = w_in[l].T.astype(_BF16)
        qa, ka, va, ga, qb, kb, vb, gb = _in_proj(
            x, shift, scale, g_pre[l], w_in_t, cos_t, sin_t, qn_a[l], kn_a[l])
        o = _mixers(qa, ka, va, ga, qb, kb, vb, gb, sink_b[l], bias)
        x = _out_proj(o, w_out[l], x, gate, g_post[l])
    return x
```

```python
import functools

import jax
import jax.numpy as jnp
from jax import lax
from jax.experimental import pallas as pl
from jax.experimental.pallas import tpu as pltpu

HEAD_DIM = 64
N_HEADS = 8
N_KV = 2
GROUP = N_HEADS // N_KV
WIDTH = N_HEADS * HEAD_DIM
KV_WIDTH = N_KV * HEAD_DIM
IN_COLS = 2 * (2 * WIDTH + 2 * KV_WIDTH)
Q_BLOCK = 128
WINDOW = 128
GRID_W = 64
ROPE_THETA = 10000.0
N_BUCKETS = 32
MAX_DISTANCE = 128
EPS = 1e-6
MASK_VALUE = -1e30
LOG2E = 1.4426950408889634
Q_SCALE = HEAD_DIM ** -0.5 * LOG2E

_QA, _KA, _VA, _GA = 0, WIDTH, WIDTH + KV_WIDTH, WIDTH + 2 * KV_WIDTH
_HALF = 2 * WIDTH + 2 * KV_WIDTH
_QB, _KB, _VB, _GB = _HALF + _QA, _HALF + _KA, _HALF + _VA, _HALF + _GA
_QK_GROUP = ((_QA, WIDTH), (_QB, WIDTH), (_KA, KV_WIDTH), (_KB, KV_WIDTH))
_GATE_GROUP = ((_GA, WIDTH), (_GB, WIDTH))
_V_GROUP = ((_VA, KV_WIDTH), (_VB, KV_WIDTH))

_F32 = jnp.float32
_BF16 = jnp.bfloat16

_VMEM_LIMIT_BYTES = 48 * 1024 * 1024

_TM_IN = 1024
_TQ_A = 256
_TK_A = 256
_KEYS_B = _TQ_A + 2 * WINDOW
_TM_OUT = 1024
_TM_OUT_CHUNK = 256
_TN_MOD = 512

_TNT = (((0,), (1,)), ((), ()))


def _silu(v):
    return 0.5 * v * (1.0 + jnp.tanh(0.5 * v))


def _params(n_grid):
    return pltpu.CompilerParams(
        dimension_semantics=("arbitrary",) * n_grid,
        vmem_limit_bytes=_VMEM_LIMIT_BYTES)


def _mod_kernel(c_ref, w_ref, b_ref, o_ref):
    ca = _silu(c_ref[...])
    o_ref[...] = jnp.dot(ca.astype(_BF16), w_ref[...].astype(_BF16),
                         preferred_element_type=_F32) + b_ref[...]


def _adaln_mod(c, w, b):
    bsz, d = c.shape
    n = w.shape[1]
    return pl.pallas_call(
        _mod_kernel,
        out_shape=jax.ShapeDtypeStruct((bsz, n), _F32),
        grid=(n // _TN_MOD,),
        in_specs=[pl.BlockSpec((bsz, d), lambda j: (0, 0)),
                  pl.BlockSpec((d, _TN_MOD), lambda j: (0, j)),
                  pl.BlockSpec((1, _TN_MOD), lambda j: (0, j))],
        out_specs=pl.BlockSpec((bsz, _TN_MOD), lambda j: (0, j)),
        compiler_params=_params(1),
        name="adaln_mod",
    )(c, w, b.reshape(1, n))


def _bias_kernel(table_ref, bucket_ref, o_ref):
    masked = jnp.full((Q_BLOCK, Q_BLOCK), MASK_VALUE, _F32)
    for h in range(N_HEADS):
        table = table_ref[h] * LOG2E
        tiles = []
        for d in range(3):
            bucket = bucket_ref[d]
            acc = masked
            for b in range(N_BUCKETS):
                acc = jnp.where(bucket == b, table[b:b + 1, :], acc)
            tiles.append(acc)
        lo, mid, hi = tiles
        for jb, (left, right) in enumerate(((lo, masked), (mid, lo), (hi, mid), (masked, hi))):
            rows = slice(jb * Q_BLOCK, (jb + 1) * Q_BLOCK)
            o_ref[h, rows, 0:Q_BLOCK] = left
            o_ref[h, rows, Q_BLOCK:2 * Q_BLOCK] = right


def _t5_bucket(rel):
    nb = N_BUCKETS // 2
    max_exact = nb // 2
    ret = (rel > 0).astype(jnp.int32) * nb
    n = jnp.abs(rel)
    assert MAX_DISTANCE == 2 * max_exact ** 2 and nb - max_exact == 8
    n2 = n * n
    large = max_exact + sum((n2 >= 2 ** k).astype(jnp.int32) for k in range(7, 31))
    large = jnp.minimum(large, nb - 1)
    return ret + jnp.where(n < max_exact, n, large)


def _t5_bias_tiles(rel_table):
    r = jnp.arange(Q_BLOCK)
    rel = ((jnp.arange(3) - 1)[:, None, None] * Q_BLOCK
           + r[None, :, None] - r[None, None, :])
    bucket = jnp.where(jnp.abs(rel) <= WINDOW, _t5_bucket(rel), -1).astype(jnp.int32)
    return pl.pallas_call(
        _bias_kernel,
        out_shape=jax.ShapeDtypeStruct((N_HEADS, _KEYS_B, _TQ_A), _F32),
        in_specs=[pl.BlockSpec(memory_space=pltpu.VMEM),
                  pl.BlockSpec(memory_space=pltpu.VMEM)],
        out_specs=pl.BlockSpec(memory_space=pltpu.VMEM),
        compiler_params=pltpu.CompilerParams(vmem_limit_bytes=_VMEM_LIMIT_BYTES),
        name="t5_bias",
    )(jnp.broadcast_to(rel_table.T[:, :, None], (N_HEADS, N_BUCKETS, Q_BLOCK)), bucket)


def _norm_rope(xh, gain, cos, sin):
    ms = jnp.mean(xh * xh, axis=0, keepdims=True)
    y = xh * lax.rsqrt(ms + EPS) * gain
    q = HEAD_DIM // 4
    rot = jnp.concatenate([-y[q:2 * q], y[0:q], -y[3 * q:4 * q], y[2 * q:3 * q]], axis=0)
    return y * cos + rot * sin


def _inproj_kernel(x_ref, shift_ref, scale_ref, gpre_ref, w_ref, cos_ref, sin_ref,
                   qn_ref, kn_ref,
                   qa_ref, ka_ref, va_ref, ga_ref, qb_ref, kb_ref, vb_ref, gb_ref):
    x = x_ref[0]
    ms = jnp.mean(x * x, axis=-1, keepdims=True)
    h = x * lax.rsqrt(ms + EPS) * gpre_ref[...]
    h = h * (1.0 + scale_ref[0]) + shift_ref[0]
    hb = h.astype(_BF16)

    def proj(group):
        w = jnp.concatenate([w_ref[:, lo:lo + n] for lo, n in group], axis=1).astype(_BF16)
        return lax.dot_general(w, hb, _TNT, preferred_element_type=_F32)

    cos = cos_ref[...]
    sin = sin_ref[...]
    qn = qn_ref[...]
    kn = kn_ref[...]

    def store_tiled(ref, rows, val):
        tq = ref.shape[3]
        for j in range(ref.shape[1]):
            ref[0, j, rows, :] = val[:, j * tq:(j + 1) * tq]

    qk = proj(_QK_GROUP)
    for hd in range(N_HEADS):
        rows = slice(hd * HEAD_DIM, (hd + 1) * HEAD_DIM)
        store_tiled(qa_ref, rows, (_norm_rope(qk[rows], qn, cos, sin) * Q_SCALE).astype(_BF16))
    store_tiled(qb_ref, slice(0, WIDTH), (qk[WIDTH:2 * WIDTH] * Q_SCALE).astype(_BF16))
    ka = qk[2 * WIDTH:2 * WIDTH + KV_WIDTH]
    ka = jnp.concatenate(
        [_norm_rope(ka[hd * HEAD_DIM:(hd + 1) * HEAD_DIM], kn, cos, sin) for hd in range(N_KV)],
        axis=0)
    ka_ref[0] = ka.T.astype(_BF16)
    kb_ref[0] = qk[2 * WIDTH + KV_WIDTH:].T.astype(_BF16)

    gates = _silu(proj(_GATE_GROUP))
    store_tiled(ga_ref, slice(0, WIDTH), gates[:WIDTH].astype(_BF16))
    store_tiled(gb_ref, slice(0, WIDTH), gates[WIDTH:].astype(_BF16))

    v = proj(_V_GROUP)
    va_ref[0] = v[:KV_WIDTH].astype(_BF16)
    store_tiled(vb_ref, slice(0, KV_WIDTH), v[KV_WIDTH:].astype(_BF16))


def _rope_tables(seq):
    rows = seq // GRID_W
    row = jnp.repeat(jnp.arange(rows), GRID_W).astype(_F32)
    col = jnp.tile(jnp.arange(GRID_W), rows).astype(_F32)
    half = HEAD_DIM // 2
    freqs = ROPE_THETA ** (-jnp.arange(0, half, 2, dtype=_F32) / half)

    def ang2(pos):
        ang = pos[:, None] * freqs[None, :]
        return jnp.concatenate([ang, ang], axis=-1)

    ang = jnp.concatenate([ang2(row), ang2(col)], axis=-1)
    return jnp.cos(ang).T, jnp.sin(ang).T


def _in_proj(x, shift, scale, g_pre, w_in, cos_t, sin_t, qn, kn):
    bsz, seq, d = x.shape
    tm = _TM_IN
    tok_t = lambda n: pl.BlockSpec((1, n, tm), lambda b, i: (b, 0, i))
    tok_m = lambda n: pl.BlockSpec((1, tm, n), lambda b, i: (b, i, 0))
    per_b = pl.BlockSpec((1, 1, d), lambda b, i: (b, 0, 0))
    tab = pl.BlockSpec((HEAD_DIM, tm), lambda b, i: (0, i))
    gain = pl.BlockSpec((HEAD_DIM, tm), lambda b, i: (0, 0))
    shp_t = lambda n: jax.ShapeDtypeStruct((bsz, n, seq), _BF16)
    shp_m = lambda n: jax.ShapeDtypeStruct((bsz, seq, n), _BF16)
    tiled = pl.BlockSpec((1, tm // _TQ_A, WIDTH, _TQ_A), lambda b, i: (b, i, 0, 0))
    shp_tiled = jax.ShapeDtypeStruct((bsz, seq // _TQ_A, WIDTH, _TQ_A), _BF16)
    v_blocks = pl.BlockSpec((1, tm // Q_BLOCK, KV_WIDTH, Q_BLOCK), lambda b, i: (b, i, 0, 0))
    shp_v_blocks = jax.ShapeDtypeStruct((bsz, seq // Q_BLOCK, KV_WIDTH, Q_BLOCK), _BF16)
    return pl.pallas_call(
        _inproj_kernel,
        out_shape=(shp_tiled, shp_m(KV_WIDTH), shp_t(KV_WIDTH), shp_tiled,
                   shp_tiled, shp_m(KV_WIDTH), shp_v_blocks, shp_tiled),
        grid=(bsz, seq // tm),
        in_specs=[pl.BlockSpec((1, tm, d), lambda b, i: (b, i, 0)),
                  per_b, per_b,
                  pl.BlockSpec((1, d), lambda b, i: (0, 0)),
                  pl.BlockSpec((d, IN_COLS), lambda b, i: (0, 0)),
                  tab, tab, gain, gain],
        out_specs=(tiled, tok_m(KV_WIDTH), tok_t(KV_WIDTH), tiled,
                   tiled, tok_m(KV_WIDTH), v_blocks, tiled),
        compiler_params=_params(2),
        name="in_proj",
    )(x, shift, scale, g_pre.reshape(1, d), w_in, cos_t, sin_t,
      jnp.broadcast_to(qn[:, None], (HEAD_DIM, tm)),
      jnp.broadcast_to(kn[:, None], (HEAD_DIM, tm)))


_PAIRS = N_HEADS // 2
_ONES_ROWS = 16
_LOOKAHEAD = 2


def _head_rows(hd):
    return slice(hd * HEAD_DIM, (hd + 1) * HEAD_DIM)


def _pair_q(head, pair):
    def padded(hd):
        qh = head(hd)
        z = jnp.zeros_like(qh)
        return jnp.concatenate([qh, z] if hd // GROUP == 0 else [z, qh], axis=0)
    return jnp.concatenate([padded(2 * pair), padded(2 * pair + 1)], axis=1)


def _pair_out(o, inv_l, gate, pair, tq):
    o = o * inv_l
    g0 = gate(2 * pair).astype(_F32)
    g1 = gate(2 * pair + 1).astype(_F32)
    return jnp.concatenate([o[:, :tq] * g0, o[:, tq:] * g1], axis=0).T.astype(_BF16)


def _v_with_ones(v):
    return jnp.concatenate([v, jnp.ones((_ONES_ROWS, v.shape[1]), v.dtype)], axis=0)


def _mixers_kernel(qa_ref, ka_ref, va_ref, ga_ref, qb_ref, kb_ref, vb_ref, gb_ref,
                   sink_ref, bias_ref, o_ref, sa0_ref, sa1_ref, sb0_ref, sb1_ref):
    n_tiles, tq = qa_ref.shape[1], qa_ref.shape[3]
    seq = ka_ref.shape[1]
    n_blocks = seq // Q_BLOCK
    a_chunks = [slice(c * _TK_A, (c + 1) * _TK_A) for c in range(seq // _TK_A)]
    items = [(kind, p) for p in range(_PAIRS) for kind in ("a", "b")]
    bufs = (sa0_ref, sb0_ref, sa1_ref, sb1_ref) * (_PAIRS // 2)

    def staged(start, size, zero):
        return pl.ds(pl.multiple_of(zero + start, size), size)

    def window_blocks(tile):
        first = (tq // Q_BLOCK) * tile - 1
        return [jnp.clip(first + j, 0, n_blocks - 1) for j in range(_KEYS_B // Q_BLOCK)]

    def score_chunks(kind, pair, buf, tile, zero):
        q_ref = qa_ref if kind == "a" else qb_ref

        def weights():
            return _pair_q(lambda hd: q_ref[0, tile, _head_rows(hd), :], pair)

        def global_chunk(rows):
            s = jnp.dot(ka_ref[0, rows, :], weights(), preferred_element_type=_F32)
            buf[staged(rows.start, _TK_A, zero), :] = s
            return jnp.max(s, axis=0, keepdims=True)

        def window():
            k = jnp.concatenate(
                [kb_ref[0, pl.ds(pl.multiple_of(blk * Q_BLOCK, Q_BLOCK), Q_BLOCK), :]
                 for blk in window_blocks(tile)], axis=0)
            bias = jnp.concatenate([bias_ref[2 * pair], bias_ref[2 * pair + 1]], axis=1)
            s = jnp.dot(k, weights(), preferred_element_type=_F32) + bias
            s = jnp.concatenate(
                [jnp.where(tile == 0, MASK_VALUE, s[:Q_BLOCK]),
                 s[Q_BLOCK:_KEYS_B - Q_BLOCK],
                 jnp.where(tile == n_tiles - 1, MASK_VALUE, s[_KEYS_B - Q_BLOCK:])], axis=0)
            buf[staged(0, _KEYS_B, zero), :] = s
            return jnp.maximum(jnp.max(s, axis=0, keepdims=True), sink_ref[pair] * LOG2E)

        if kind == "a":
            return [functools.partial(global_chunk, rows) for rows in a_chunks]
        return [window]

    def pv_chunks(kind, pair, buf, tile, zero, m):
        kv = _head_rows(pair // (GROUP // 2))

        def global_chunk(rows):
            e = jnp.exp2(buf[staged(rows.start, _TK_A, zero), :] - m)
            return jnp.dot(_v_with_ones(va_ref[0, kv, rows]), e.astype(_BF16),
                           preferred_element_type=_F32)

        def window():
            e = jnp.exp2(buf[staged(0, _KEYS_B, zero), :] - m)
            v = jnp.concatenate([vb_ref[0, blk] for blk in window_blocks(tile)], axis=1)
            return jnp.dot(_v_with_ones(v[kv, :]), e.astype(_BF16), preferred_element_type=_F32)

        if kind == "a":
            return [functools.partial(global_chunk, rows) for rows in a_chunks]
        return [window]

    def finish(kind, pair, tile, o, m):
        l = o[HEAD_DIM:HEAD_DIM + 1]
        if kind == "b":
            l = l + jnp.exp2(sink_ref[pair] * LOG2E - m)
        g_ref = ga_ref if kind == "a" else gb_ref
        lane0 = (0 if kind == "a" else WIDTH) + pair * 2 * HEAD_DIM
        o_ref[0, pl.ds(pl.multiple_of(tile * tq, tq), tq), lane0:lane0 + 2 * HEAD_DIM] = _pair_out(
            o[:HEAD_DIM], 1.0 / l, lambda hd: g_ref[0, tile, _head_rows(hd), :], pair, tq)

    def col_max(parts):
        return functools.reduce(jnp.maximum, parts)

    def tile_body(tile, carry):
        zero = jnp.minimum(tile, 0)
        next_tile = jnp.minimum(tile + 1, n_tiles - 1)
        col_maxes = list(carry)
        for idx, (kind, pair) in enumerate(items):
            ahead = (idx + _LOOKAHEAD) % len(items)
            ahead_tile = tile if idx + _LOOKAHEAD < len(items) else next_tile
            nxt = score_chunks(*items[ahead], bufs[ahead], ahead_tile, zero)
            m = col_maxes[idx]
            cur = pv_chunks(kind, pair, bufs[idx], tile, zero, m)
            parts, o = [], None
            for i in range(max(len(nxt), len(cur))):
                if i < len(nxt):
                    parts.append(nxt[i]())
                if i < len(cur):
                    pv = cur[i]()
                    o = pv if o is None else o + pv
            finish(kind, pair, tile, o, m)
            col_maxes.append(col_max(parts))
        return tuple(col_maxes[len(items):])

    first = jnp.minimum(pl.program_id(0), 0)
    carry0 = tuple(
        col_max([chunk() for chunk in score_chunks(*items[i], bufs[i], first, first)])
        for i in range(_LOOKAHEAD))
    def two_tiles(i, carry):
        return tile_body(2 * i + 1, tile_body(2 * i, carry))

    lax.fori_loop(0, n_tiles // 2, two_tiles, carry0)


def _mixers(qa, ka, va, ga, qb, kb, vb, gb, sink, bias):
    bsz, n_tiles, _, tq = qa.shape
    seq = ka.shape[1]
    tiled = pl.BlockSpec((1, n_tiles, WIDTH, tq), lambda b: (b, 0, 0, 0))
    k_spec = pl.BlockSpec((1, seq, KV_WIDTH), lambda b: (b, 0, 0))
    const = lambda shape: pl.BlockSpec(shape, lambda b: (0,) * len(shape))
    return pl.pallas_call(
        _mixers_kernel,
        out_shape=jax.ShapeDtypeStruct((bsz, seq, 2 * WIDTH), _BF16),
        grid=(bsz,),
        in_specs=[tiled, k_spec, pl.BlockSpec((1, KV_WIDTH, seq), lambda b: (b, 0, 0)), tiled,
                  tiled, k_spec,
                  pl.BlockSpec((1, seq // Q_BLOCK, KV_WIDTH, Q_BLOCK), lambda b: (b, 0, 0, 0)),
                  tiled,
                  const((_PAIRS, 1, 2 * tq)), const((N_HEADS, _KEYS_B, tq))],
        out_specs=pl.BlockSpec((1, seq, 2 * WIDTH), lambda b: (b, 0, 0)),
        scratch_shapes=([pltpu.VMEM((seq, 2 * tq), _F32)] * 2
                        + [pltpu.VMEM((_KEYS_B, 2 * tq), _F32)] * 2),
        compiler_params=_params(1),
        name="mixers",
    )(qa, ka, va, ga, qb, kb, vb, gb,
      jnp.broadcast_to(sink[:, None], (N_HEADS, tq)).reshape(_PAIRS, 1, 2 * tq), bias)


def _outproj_kernel(a_ref, w_ref, x_ref, gate_ref, gpost_ref, o_ref):
    w = w_ref[...].astype(_BF16)
    for c in range(a_ref.shape[1] // _TM_OUT_CHUNK):
        rows = slice(c * _TM_OUT_CHUNK, (c + 1) * _TM_OUT_CHUNK)
        y = jnp.dot(a_ref[0, rows, :], w, preferred_element_type=_F32)
        ms = jnp.mean(y * y, axis=-1, keepdims=True)
        yn = y * lax.rsqrt(ms + EPS) * gpost_ref[...]
        o_ref[0, rows, :] = x_ref[0, rows, :] + gate_ref[0] * yn


def _out_proj(o, w_out, x, gate, g_post):
    bsz, seq, d = x.shape
    tm = _TM_OUT
    xo = pl.BlockSpec((1, tm, d), lambda b, i: (b, i, 0))
    return pl.pallas_call(
        _outproj_kernel,
        out_shape=jax.ShapeDtypeStruct((bsz, seq, d), _F32),
        grid=(bsz, seq // tm),
        in_specs=[pl.BlockSpec((1, tm, 2 * WIDTH), lambda b, i: (b, i, 0)),
                  pl.BlockSpec((2 * WIDTH, d), lambda b, i: (0, 0)),
                  xo,
                  pl.BlockSpec((1, 1, d), lambda b, i: (b, 0, 0)),
                  pl.BlockSpec((1, d), lambda b, i: (0, 0))],
        out_specs=xo,
        compiler_params=_params(2),
        name="out_proj",
    )(o, w_out, x, gate, g_post.reshape(1, d))


def kernel(x, c, w_ada, b_ada, g_pre, g_post, w_in, qn_a, kn_a, sink_b, w_out, rel_table):
    bsz, seq, d = x.shape
    depth = w_ada.shape[0]
    cos_t, sin_t = _rope_tables(seq)
    bias = _t5_bias_tiles(rel_table)
    for l in range(depth):
        mod = _adaln_mod(c, w_ada[l], b_ada[l])
        shift, scale, gate = (mod[:, i * d:(i + 1) * d].reshape(bsz, 1, d) for i in range(3))
        qa, ka, va, ga, qb, kb, vb, gb = _in_proj(
            x, shift, scale, g_pre[l], w_in[l], cos_t, sin_t, qn_a[l], kn_a[l])
        o = _mixers(qa, ka, va, ga, qb, kb, vb, gb, sink_b[l], bias)
        x = _out_proj(o, w_out[l], x, gate, g_post[l])
    return x
```

```python
import functools

import jax
import jax.numpy as jnp
import numpy as np
from jax import lax
from jax.experimental import pallas as pl
from jax.experimental.pallas import tpu as pltpu

HEAD_DIM = 64
N_HEADS = 8
N_KV = 2
GROUP = N_HEADS // N_KV
WIDTH = N_HEADS * HEAD_DIM
KV_WIDTH = N_KV * HEAD_DIM
IN_COLS = 2 * (2 * WIDTH + 2 * KV_WIDTH)
Q_BLOCK = 128
WINDOW = 128
GRID_W = 64
ROPE_THETA = 10000.0
N_BUCKETS = 32
MAX_DISTANCE = 128
EPS = 1e-6
MASK_VALUE = -1e30
LOG2E = 1.4426950408889634
Q_SCALE = HEAD_DIM ** -0.5 * LOG2E

_QA, _KA, _VA, _GA = 0, WIDTH, WIDTH + KV_WIDTH, WIDTH + 2 * KV_WIDTH
_HALF = 2 * WIDTH + 2 * KV_WIDTH
_QB, _KB, _VB, _GB = _HALF + _QA, _HALF + _KA, _HALF + _VA, _HALF + _GA
_QK_GROUP = ((_QA, WIDTH), (_QB, WIDTH), (_KA, KV_WIDTH), (_KB, KV_WIDTH))
_GATE_GROUP = ((_GA, WIDTH), (_GB, WIDTH))
_V_GROUP = ((_VA, KV_WIDTH), (_VB, KV_WIDTH))

_F32 = jnp.float32
_BF16 = jnp.bfloat16

_VMEM_LIMIT_BYTES = 48 * 1024 * 1024

_TM_IN = 1024
_TQ_A = 256
_TK_A = 256
_KEYS_B = _TQ_A + 2 * WINDOW
_KEYS_Q = Q_BLOCK + 2 * WINDOW
_KEYS_Q0 = slice(0, _KEYS_Q)
_KEYS_Q1 = slice(Q_BLOCK, Q_BLOCK + _KEYS_Q)
_TM_OUT = 1024
_TM_OUT_CHUNKS = (256, 256, 256, 128, 128)
_TN_MOD = 512

_TNT = (((0,), (1,)), ((), ()))


def _silu(v):
    return 0.5 * v * (1.0 + jnp.tanh(0.5 * v))


def _params(n_grid):
    return pltpu.CompilerParams(
        dimension_semantics=("arbitrary",) * n_grid,
        vmem_limit_bytes=_VMEM_LIMIT_BYTES)


def _mod_kernel(c_ref, w_ref, b_ref, o_ref):
    ca = _silu(c_ref[...])
    o_ref[...] = jnp.dot(ca.astype(_BF16), w_ref[...].astype(_BF16),
                         preferred_element_type=_F32) + b_ref[...]


def _adaln_mod(c, w, b):
    bsz, d = c.shape
    n = w.shape[1]
    return pl.pallas_call(
        _mod_kernel,
        out_shape=jax.ShapeDtypeStruct((bsz, n), _F32),
        grid=(n // _TN_MOD,),
        in_specs=[pl.BlockSpec((bsz, d), lambda j: (0, 0)),
                  pl.BlockSpec((d, _TN_MOD), lambda j: (0, j)),
                  pl.BlockSpec((1, _TN_MOD), lambda j: (0, j))],
        out_specs=pl.BlockSpec((bsz, _TN_MOD), lambda j: (0, j)),
        compiler_params=_params(1),
        name="adaln_mod",
    )(c, w, b.reshape(1, n))


def _bias_kernel(table_ref, bucket_ref, o_ref):
    masked = jnp.full((Q_BLOCK, Q_BLOCK), MASK_VALUE, _F32)
    for h in range(N_HEADS):
        table = table_ref[h] * LOG2E
        tiles = []
        for d in range(3):
            bucket = bucket_ref[d]
            acc = masked
            for b in range(N_BUCKETS):
                acc = jnp.where(bucket == b, table[b:b + 1, :], acc)
            tiles.append(acc)
        lo, mid, hi = tiles
        for jb, (left, right) in enumerate(((lo, masked), (mid, lo), (hi, mid), (masked, hi))):
            rows = slice(jb * Q_BLOCK, (jb + 1) * Q_BLOCK)
            o_ref[h, rows, 0:Q_BLOCK] = left
            o_ref[h, rows, Q_BLOCK:2 * Q_BLOCK] = right


def _t5_bucket(rel):
    nb = N_BUCKETS // 2
    max_exact = nb // 2
    ret = (rel > 0).astype(jnp.int32) * nb
    n = jnp.abs(rel)
    assert MAX_DISTANCE == 2 * max_exact ** 2 and nb - max_exact == 8
    n2 = n * n
    large = max_exact + sum((n2 >= 2 ** k).astype(jnp.int32) for k in range(7, 31))
    large = jnp.minimum(large, nb - 1)
    return ret + jnp.where(n < max_exact, n, large)


def _t5_bias_tiles(rel_table):
    r = jnp.arange(Q_BLOCK)
    rel = ((jnp.arange(3) - 1)[:, None, None] * Q_BLOCK
           + r[None, :, None] - r[None, None, :])
    bucket = jnp.where(jnp.abs(rel) <= WINDOW, _t5_bucket(rel), -1).astype(jnp.int32)
    return pl.pallas_call(
        _bias_kernel,
        out_shape=jax.ShapeDtypeStruct((N_HEADS, _KEYS_B, _TQ_A), _F32),
        in_specs=[pl.BlockSpec(memory_space=pltpu.VMEM),
                  pl.BlockSpec(memory_space=pltpu.VMEM)],
        out_specs=pl.BlockSpec(memory_space=pltpu.VMEM),
        compiler_params=pltpu.CompilerParams(vmem_limit_bytes=_VMEM_LIMIT_BYTES),
        name="t5_bias",
    )(jnp.broadcast_to(rel_table.T[:, :, None], (N_HEADS, N_BUCKETS, Q_BLOCK)), bucket)


def _norm_rope(xh, gain, cos, sin):
    ms = jnp.mean(xh * xh, axis=0, keepdims=True)
    y = xh * lax.rsqrt(ms + EPS) * gain
    q = HEAD_DIM // 4
    rot = jnp.concatenate([-y[q:2 * q], y[0:q], -y[3 * q:4 * q], y[2 * q:3 * q]], axis=0)
    return y * cos + rot * sin


def _inproj_kernel(x_ref, shift_ref, scale_ref, gpre_ref, w_ref, cos_ref, sin_ref,
                   qn_ref, kn_ref,
                   qa_ref, ka_ref, va_ref, ga_ref, qb_ref, kb_ref, vb_ref, gb_ref):
    x = x_ref[0]
    ms = jnp.mean(x * x, axis=-1, keepdims=True)
    h = x * lax.rsqrt(ms + EPS) * gpre_ref[...]
    h = h * (1.0 + scale_ref[0]) + shift_ref[0]
    hb = h.astype(_BF16)

    def proj(group):
        w = jnp.concatenate([w_ref[:, lo:lo + n] for lo, n in group], axis=1).astype(_BF16)
        return lax.dot_general(w, hb, _TNT, preferred_element_type=_F32)

    cos = cos_ref[...]
    sin = sin_ref[...]
    qn = qn_ref[...]
    kn = kn_ref[...]

    def store_tiled(ref, rows, val):
        tq = ref.shape[3]
        for j in range(ref.shape[1]):
            ref[0, j, rows, :] = val[:, j * tq:(j + 1) * tq]

    qk = proj(_QK_GROUP)
    for hd in range(N_HEADS):
        rows = slice(hd * HEAD_DIM, (hd + 1) * HEAD_DIM)
        store_tiled(qa_ref, rows, (_norm_rope(qk[rows], qn, cos, sin) * Q_SCALE).astype(_BF16))
    store_tiled(qb_ref, slice(0, WIDTH), (qk[WIDTH:2 * WIDTH] * Q_SCALE).astype(_BF16))
    ka = qk[2 * WIDTH:2 * WIDTH + KV_WIDTH]
    ka = jnp.concatenate(
        [_norm_rope(ka[hd * HEAD_DIM:(hd + 1) * HEAD_DIM], kn, cos, sin) for hd in range(N_KV)],
        axis=0)
    ka_ref[0] = ka.T.astype(_BF16)
    kb_ref[0] = qk[2 * WIDTH + KV_WIDTH:].T.astype(_BF16)

    gates = _silu(proj(_GATE_GROUP))
    store_tiled(ga_ref, slice(0, WIDTH), gates[:WIDTH].astype(_BF16))
    store_tiled(gb_ref, slice(0, WIDTH), gates[WIDTH:].astype(_BF16))

    v = proj(_V_GROUP)
    va_ref[0] = v[:KV_WIDTH].astype(_BF16)
    store_tiled(vb_ref, slice(0, KV_WIDTH), v[KV_WIDTH:].astype(_BF16))


def _rope_tables(seq):
    rows = seq // GRID_W
    row = jnp.repeat(jnp.arange(rows), GRID_W).astype(_F32)
    col = jnp.tile(jnp.arange(GRID_W), rows).astype(_F32)
    half = HEAD_DIM // 2
    freqs = ROPE_THETA ** (-jnp.arange(0, half, 2, dtype=_F32) / half)

    def ang2(pos):
        ang = pos[:, None] * freqs[None, :]
        return jnp.concatenate([ang, ang], axis=-1)

    ang = jnp.concatenate([ang2(row), ang2(col)], axis=-1)
    return jnp.cos(ang).T, jnp.sin(ang).T


def _in_proj(x, shift, scale, g_pre, w_in, cos_t, sin_t, qn, kn):
    bsz, seq, d = x.shape
    tm = _TM_IN
    tok_t = lambda n: pl.BlockSpec((1, n, tm), lambda b, i: (b, 0, i))
    tok_m = lambda n: pl.BlockSpec((1, tm, n), lambda b, i: (b, i, 0))
    per_b = pl.BlockSpec((1, 1, d), lambda b, i: (b, 0, 0))
    tab = pl.BlockSpec((HEAD_DIM, tm), lambda b, i: (0, i))
    gain = pl.BlockSpec((HEAD_DIM, tm), lambda b, i: (0, 0))
    shp_t = lambda n: jax.ShapeDtypeStruct((bsz, n, seq), _BF16)
    shp_m = lambda n: jax.ShapeDtypeStruct((bsz, seq, n), _BF16)
    tiled = pl.BlockSpec((1, tm // _TQ_A, WIDTH, _TQ_A), lambda b, i: (b, i, 0, 0))
    shp_tiled = jax.ShapeDtypeStruct((bsz, seq // _TQ_A, WIDTH, _TQ_A), _BF16)
    v_blocks = pl.BlockSpec((1, tm // Q_BLOCK, KV_WIDTH, Q_BLOCK), lambda b, i: (b, i, 0, 0))
    shp_v_blocks = jax.ShapeDtypeStruct((bsz, seq // Q_BLOCK, KV_WIDTH, Q_BLOCK), _BF16)
    return pl.pallas_call(
        _inproj_kernel,
        out_shape=(shp_tiled, shp_m(KV_WIDTH), shp_t(KV_WIDTH), shp_tiled,
                   shp_tiled, shp_m(KV_WIDTH), shp_v_blocks, shp_tiled),
        grid=(bsz, seq // tm),
        in_specs=[pl.BlockSpec((1, tm, d), lambda b, i: (b, i, 0)),
                  per_b, per_b,
                  pl.BlockSpec((1, d), lambda b, i: (0, 0)),
                  pl.BlockSpec((d, IN_COLS), lambda b, i: (0, 0)),
                  tab, tab, gain, gain],
        out_specs=(tiled, tok_m(KV_WIDTH), tok_t(KV_WIDTH), tiled,
                   tiled, tok_m(KV_WIDTH), v_blocks, tiled),
        compiler_params=_params(2),
        name="in_proj",
    )(x, shift, scale, g_pre.reshape(1, d), w_in, cos_t, sin_t,
      jnp.broadcast_to(qn[:, None], (HEAD_DIM, tm)),
      jnp.broadcast_to(kn[:, None], (HEAD_DIM, tm)))


_PAIRS = N_HEADS // 2
_ONES_ROWS = 16
_LOOKAHEAD = 2


def _head_rows(hd):
    return slice(hd * HEAD_DIM, (hd + 1) * HEAD_DIM)


def _pair_q(head, pair):
    def padded(hd):
        qh = head(hd)
        z = jnp.zeros_like(qh)
        return jnp.concatenate([qh, z] if hd // GROUP == 0 else [z, qh], axis=0)
    return jnp.concatenate([padded(2 * pair), padded(2 * pair + 1)], axis=1)


def _pair_out(o, inv_l, gate, pair, tq):
    o = o * inv_l
    g0 = gate(2 * pair).astype(_F32)
    g1 = gate(2 * pair + 1).astype(_F32)
    return jnp.concatenate([o[:, :tq] * g0, o[:, tq:] * g1], axis=0).T.astype(_BF16)


def _swap_middle(a):
    b = Q_BLOCK
    return jnp.concatenate([a[:, :b], a[:, 2 * b:3 * b], a[:, b:2 * b], a[:, 3 * b:]], axis=1)


def _v_with_ones(v):
    return jnp.concatenate([v, jnp.ones((_ONES_ROWS, v.shape[1]), v.dtype)], axis=0)


def _mixers_kernel(qa_ref, ka_ref, va_ref, ga_ref, qb_ref, kb_ref, vb_ref, gb_ref,
                   sink_ref, bias_ref, o_ref, sa0_ref, sa1_ref, sb0_ref, sb1_ref):
    n_tiles, tq = qa_ref.shape[1], qa_ref.shape[3]
    seq = ka_ref.shape[1]
    n_blocks = seq // Q_BLOCK
    a_chunks = [slice(c * _TK_A, (c + 1) * _TK_A) for c in range(seq // _TK_A)]
    items = [(kind, p) for p in range(_PAIRS) for kind in ("a", "b")]
    bufs = (sa0_ref, sb0_ref, sa1_ref, sb1_ref) * (_PAIRS // 2)

    def staged(start, size, zero):
        return pl.ds(pl.multiple_of(zero + start, Q_BLOCK), size)

    def window_blocks(tile):
        first = (tq // Q_BLOCK) * tile - 1
        return [jnp.clip(first + j, 0, n_blocks - 1) for j in range(_KEYS_B // Q_BLOCK)]

    def score_chunks(kind, pair, buf, tile, zero):
        q_ref = qa_ref if kind == "a" else qb_ref

        def weights():
            return _pair_q(lambda hd: q_ref[0, tile, _head_rows(hd), :], pair)

        def global_chunk(rows):
            s = jnp.dot(ka_ref[0, rows, :], weights(), preferred_element_type=_F32)
            buf[staged(rows.start, _TK_A, zero), :] = s
            return jnp.max(s, axis=0, keepdims=True)

        def window():
            k = jnp.concatenate(
                [kb_ref[0, pl.ds(pl.multiple_of(blk * Q_BLOCK, Q_BLOCK), Q_BLOCK), :]
                 for blk in window_blocks(tile)], axis=0)
            s = jnp.dot(k, _swap_middle(weights()), preferred_element_type=_F32)

            def bias(keys, queries):
                return jnp.concatenate([bias_ref[2 * pair, keys, queries],
                                        bias_ref[2 * pair + 1, keys, queries]], axis=1)

            left = s[_KEYS_Q0, :tq] + bias(_KEYS_Q0, slice(0, Q_BLOCK))
            right = s[_KEYS_Q1, tq:] + bias(_KEYS_Q1, slice(Q_BLOCK, 2 * Q_BLOCK))
            left = jnp.concatenate(
                [jnp.where(tile == 0, MASK_VALUE, left[:Q_BLOCK]), left[Q_BLOCK:]], axis=0)
            right = jnp.concatenate(
                [right[:-Q_BLOCK], jnp.where(tile == n_tiles - 1, MASK_VALUE, right[-Q_BLOCK:])],
                axis=0)
            buf[staged(_KEYS_Q0.start, _KEYS_Q, zero), :tq] = left
            buf[staged(_KEYS_Q1.start, _KEYS_Q, zero), tq:] = right
            m = jnp.concatenate([jnp.max(left, axis=0, keepdims=True),
                                 jnp.max(right, axis=0, keepdims=True)], axis=1)
            return jnp.maximum(m, sink_ref[pair] * LOG2E)

        if kind == "a":
            return [functools.partial(global_chunk, rows) for rows in a_chunks]
        return [window]

    def pv_chunks(kind, pair, buf, tile, zero, m):
        kv = _head_rows(pair // (GROUP // 2))

        def global_chunk(rows):
            e = jnp.exp2(buf[staged(rows.start, _TK_A, zero), :] - m)
            return jnp.dot(_v_with_ones(va_ref[0, kv, rows]), e.astype(_BF16),
                           preferred_element_type=_F32)

        def window():
            e_left = jnp.exp2(buf[staged(_KEYS_Q0.start, _KEYS_Q, zero), :tq] - m[:, :tq])
            e_right = jnp.exp2(buf[staged(_KEYS_Q1.start, _KEYS_Q, zero), tq:] - m[:, tq:])
            v = jnp.concatenate([vb_ref[0, blk] for blk in window_blocks(tile)], axis=1)
            v = _v_with_ones(v[kv, :])
            return jnp.concatenate(
                [jnp.dot(v[:, _KEYS_Q0], e_left.astype(_BF16), preferred_element_type=_F32),
                 jnp.dot(v[:, _KEYS_Q1], e_right.astype(_BF16), preferred_element_type=_F32)],
                axis=1)

        if kind == "a":
            return [functools.partial(global_chunk, rows) for rows in a_chunks]
        return [window]

    def finish(kind, pair, tile, o, m):
        l = o[HEAD_DIM:HEAD_DIM + 1]
        if kind == "b":
            l = _swap_middle(l + jnp.exp2(sink_ref[pair] * LOG2E - m))
            o = _swap_middle(o)
        g_ref = ga_ref if kind == "a" else gb_ref
        lane0 = (0 if kind == "a" else WIDTH) + pair * 2 * HEAD_DIM
        o_ref[0, pl.ds(pl.multiple_of(tile * tq, tq), tq), lane0:lane0 + 2 * HEAD_DIM] = _pair_out(
            o[:HEAD_DIM], 1.0 / l, lambda hd: g_ref[0, tile, _head_rows(hd), :], pair, tq)

    def col_max(parts):
        return functools.reduce(jnp.maximum, parts)

    def tile_body(tile, carry):
        zero = jnp.minimum(tile, 0)
        next_tile = jnp.minimum(tile + 1, n_tiles - 1)
        col_maxes = list(carry)
        for idx, (kind, pair) in enumerate(items):
            ahead = (idx + _LOOKAHEAD) % len(items)
            ahead_tile = tile if idx + _LOOKAHEAD < len(items) else next_tile
            nxt = score_chunks(*items[ahead], bufs[ahead], ahead_tile, zero)
            m = col_maxes[idx]
            cur = pv_chunks(kind, pair, bufs[idx], tile, zero, m)
            parts, o = [], None
            for i in range(max(len(nxt), len(cur))):
                if i < len(nxt):
                    parts.append(nxt[i]())
                if i < len(cur):
                    pv = cur[i]()
                    o = pv if o is None else o + pv
            finish(kind, pair, tile, o, m)
            col_maxes.append(col_max(parts))
        return tuple(col_maxes[len(items):])

    first = jnp.minimum(pl.program_id(0), 0)
    carry0 = tuple(
        col_max([chunk() for chunk in score_chunks(*items[i], bufs[i], first, first)])
        for i in range(_LOOKAHEAD))
    def two_tiles(i, carry):
        return tile_body(2 * i + 1, tile_body(2 * i, carry))

    lax.fori_loop(0, n_tiles // 2, two_tiles, carry0)


def _mixers(qa, ka, va, ga, qb, kb, vb, gb, sink, bias):
    bsz, n_tiles, _, tq = qa.shape
    seq = ka.shape[1]
    tiled = pl.BlockSpec((1, n_tiles, WIDTH, tq), lambda b: (b, 0, 0, 0))
    k_spec = pl.BlockSpec((1, seq, KV_WIDTH), lambda b: (b, 0, 0))
    const = lambda shape: pl.BlockSpec(shape, lambda b: (0,) * len(shape))
    return pl.pallas_call(
        _mixers_kernel,
        out_shape=jax.ShapeDtypeStruct((bsz, seq, 2 * WIDTH), _BF16),
        grid=(bsz,),
        in_specs=[tiled, k_spec, pl.BlockSpec((1, KV_WIDTH, seq), lambda b: (b, 0, 0)), tiled,
                  tiled, k_spec,
                  pl.BlockSpec((1, seq // Q_BLOCK, KV_WIDTH, Q_BLOCK), lambda b: (b, 0, 0, 0)),
                  tiled,
                  const((_PAIRS, 1, 2 * tq)), const((N_HEADS, _KEYS_B, tq))],
        out_specs=pl.BlockSpec((1, seq, 2 * WIDTH), lambda b: (b, 0, 0)),
        scratch_shapes=([pltpu.VMEM((seq, 2 * tq), _F32)] * 2
                        + [pltpu.VMEM((_KEYS_B, 2 * tq), _F32)] * 2),
        compiler_params=_params(1),
        name="mixers",
    )(qa, ka, va, ga, qb, kb, vb, gb,
      jnp.tile(jnp.broadcast_to(sink[:, None], (N_HEADS, Q_BLOCK)).reshape(_PAIRS, 1, 2 * Q_BLOCK),
               (1, 1, tq // Q_BLOCK)),
      bias)


def _outproj_kernel(a_ref, w_ref, x_ref, gate_ref, gpost_ref, o_ref):
    w = w_ref[...].astype(_BF16)
    starts = np.cumsum((0,) + _TM_OUT_CHUNKS)
    assert starts[-1] == a_ref.shape[1]
    for lo, hi in zip(starts[:-1], starts[1:]):
        rows = slice(int(lo), int(hi))
        y = jnp.dot(a_ref[0, rows, :], w, preferred_element_type=_F32)
        ms = jnp.mean(y * y, axis=-1, keepdims=True)
        yn = y * lax.rsqrt(ms + EPS) * gpost_ref[...]
        o_ref[0, rows, :] = x_ref[0, rows, :] + gate_ref[0] * yn


def _out_proj(o, w_out, x, gate, g_post):
    bsz, seq, d = x.shape
    tm = _TM_OUT
    xo = pl.BlockSpec((1, tm, d), lambda b, i: (b, i, 0))
    return pl.pallas_call(
        _outproj_kernel,
        out_shape=jax.ShapeDtypeStruct((bsz, seq, d), _F32),
        grid=(bsz, seq // tm),
        in_specs=[pl.BlockSpec((1, tm, 2 * WIDTH), lambda b, i: (b, i, 0)),
                  pl.BlockSpec((2 * WIDTH, d), lambda b, i: (0, 0)),
                  xo,
                  pl.BlockSpec((1, 1, d), lambda b, i: (b, 0, 0)),
                  pl.BlockSpec((1, d), lambda b, i: (0, 0))],
        out_specs=xo,
        compiler_params=_params(2),
        name="out_proj",
    )(o, w_out, x, gate, g_post.reshape(1, d))


def kernel(x, c, w_ada, b_ada, g_pre, g_post, w_in, qn_a, kn_a, sink_b, w_out, rel_table):
    bsz, seq, d = x.shape
    depth = w_ada.shape[0]
    cos_t, sin_t = _rope_tables(seq)
    bias = _t5_bias_tiles(rel_table)
    for l in range(depth):
        mod = _adaln_mod(c, w_ada[l], b_ada[l])
        shift, scale, gate = (mod[:, i * d:(i + 1) * d].reshape(bsz, 1, d) for i in range(3))
        qa, ka, va, ga, qb, kb, vb, gb = _in_proj(
            x, shift, scale, g_pre[l], w_in[l], cos_t, sin_t, qn_a[l], kn_a[l])
        o = _mixers(qa, ka, va, ga, qb, kb, vb, gb, sink_b[l], bias)
        x = _out_proj(o, w_out[l], x, gate, g_post[l])
    return x
```

```python
import functools

import jax
import jax.numpy as jnp
import numpy as np
from jax import lax
from jax.experimental import pallas as pl
from jax.experimental.pallas import tpu as pltpu

HEAD_DIM = 64
N_HEADS = 8
N_KV = 2
GROUP = N_HEADS // N_KV
WIDTH = N_HEADS * HEAD_DIM
KV_WIDTH = N_KV * HEAD_DIM
IN_COLS = 2 * (2 * WIDTH + 2 * KV_WIDTH)
Q_BLOCK = 128
WINDOW = 128
GRID_W = 64
ROPE_THETA = 10000.0
N_BUCKETS = 32
MAX_DISTANCE = 128
EPS = 1e-6
MASK_VALUE = -1e30
LOG2E = 1.4426950408889634
Q_SCALE = HEAD_DIM ** -0.5 * LOG2E

_QA, _KA, _VA, _GA = 0, WIDTH, WIDTH + KV_WIDTH, WIDTH + 2 * KV_WIDTH
_HALF = 2 * WIDTH + 2 * KV_WIDTH
_QB, _KB, _VB, _GB = _HALF + _QA, _HALF + _KA, _HALF + _VA, _HALF + _GA
_QK_GROUP = ((_QA, WIDTH), (_QB, WIDTH), (_KA, KV_WIDTH), (_KB, KV_WIDTH))
_GATE_GROUP = ((_GA, WIDTH), (_GB, WIDTH))
_V_GROUP = ((_VA, KV_WIDTH), (_VB, KV_WIDTH))

_F32 = jnp.float32
_BF16 = jnp.bfloat16

_VMEM_LIMIT_BYTES = 48 * 1024 * 1024

_TM_IN = 1024
_TQ_A = 256
_TK_A = 256
_KEYS_B = _TQ_A + 2 * WINDOW
_KEYS_Q = Q_BLOCK + 2 * WINDOW
_KEYS_Q0 = slice(0, _KEYS_Q)
_KEYS_Q1 = slice(Q_BLOCK, Q_BLOCK + _KEYS_Q)
_TM_OUT = 1024
_TM_OUT_CHUNKS = (256, 256, 256, 128, 128)
_TN_MOD = 1536

_TNT = (((0,), (1,)), ((), ()))
_SHIFT, _SCALE, _GATE = 0, 1, 2


def _silu(v):
    return 0.5 * v * (1.0 + jnp.tanh(0.5 * v))


def _params(n_grid):
    return pltpu.CompilerParams(
        dimension_semantics=("arbitrary",) * n_grid,
        vmem_limit_bytes=_VMEM_LIMIT_BYTES)


def _mod_kernel(c_ref, w_ref, b_ref, o_ref):
    ca = _silu(c_ref[...])
    o_ref[...] = jnp.dot(ca.astype(_BF16), w_ref[...].astype(_BF16),
                         preferred_element_type=_F32) + b_ref[...]


def _adaln_mod(c, w, b):
    bsz, d = c.shape
    n = w.shape[1]
    return pl.pallas_call(
        _mod_kernel,
        out_shape=jax.ShapeDtypeStruct((bsz, n), _F32),
        grid=(n // _TN_MOD,),
        in_specs=[pl.BlockSpec((bsz, d), lambda j: (0, 0)),
                  pl.BlockSpec((d, _TN_MOD), lambda j: (0, j)),
                  pl.BlockSpec((1, _TN_MOD), lambda j: (0, j))],
        out_specs=pl.BlockSpec((bsz, _TN_MOD), lambda j: (0, j)),
        compiler_params=_params(1),
        name="adaln_mod",
    )(c, w, b.reshape(1, n))


def _bias_kernel(table_ref, bucket_ref, o_ref):
    masked = jnp.full((Q_BLOCK, Q_BLOCK), MASK_VALUE, _F32)
    for h in range(N_HEADS):
        table = table_ref[h] * LOG2E
        tiles = []
        for d in range(3):
            bucket = bucket_ref[d]
            acc = masked
            for b in range(N_BUCKETS):
                acc = jnp.where(bucket == b, table[b:b + 1, :], acc)
            tiles.append(acc)
        lo, mid, hi = tiles
        for jb, (left, right) in enumerate(((lo, masked), (mid, lo), (hi, mid), (masked, hi))):
            rows = slice(jb * Q_BLOCK, (jb + 1) * Q_BLOCK)
            o_ref[h, rows, 0:Q_BLOCK] = left
            o_ref[h, rows, Q_BLOCK:2 * Q_BLOCK] = right


def _t5_bucket(rel):
    nb = N_BUCKETS // 2
    max_exact = nb // 2
    ret = (rel > 0).astype(np.int32) * nb
    n = np.abs(rel)
    assert MAX_DISTANCE == 2 * max_exact ** 2 and nb - max_exact == 8
    n2 = n.astype(np.int64) ** 2
    large = max_exact + sum((n2 >= 2 ** k).astype(np.int32) for k in range(7, 31))
    large = np.minimum(large, nb - 1)
    return ret + np.where(n < max_exact, n, large)


def _t5_bias_tiles(rel_table):
    r = np.arange(Q_BLOCK)
    rel = ((np.arange(3) - 1)[:, None, None] * Q_BLOCK
           + r[None, :, None] - r[None, None, :])
    bucket = jnp.asarray(np.where(np.abs(rel) <= WINDOW, _t5_bucket(rel), -1).astype(np.int32))
    return pl.pallas_call(
        _bias_kernel,
        out_shape=jax.ShapeDtypeStruct((N_HEADS, _KEYS_B, _TQ_A), _F32),
        in_specs=[pl.BlockSpec(memory_space=pltpu.VMEM),
                  pl.BlockSpec(memory_space=pltpu.VMEM)],
        out_specs=pl.BlockSpec(memory_space=pltpu.VMEM),
        compiler_params=pltpu.CompilerParams(vmem_limit_bytes=_VMEM_LIMIT_BYTES),
        name="t5_bias",
    )(jnp.broadcast_to(rel_table.T[:, :, None], (N_HEADS, N_BUCKETS, Q_BLOCK)), bucket)


def _norm_rope(xh, gain, cos, sin):
    ms = jnp.mean(xh * xh, axis=0, keepdims=True)
    y = xh * lax.rsqrt(ms + EPS) * gain
    q = HEAD_DIM // 4
    rot = jnp.concatenate([-y[q:2 * q], y[0:q], -y[3 * q:4 * q], y[2 * q:3 * q]], axis=0)
    return y * cos + rot * sin


def _inproj_kernel(x_ref, shift_ref, scale_ref, gpre_ref, w_ref, cos_ref, sin_ref,
                   qn_ref, kn_ref,
                   qa_ref, ka_ref, va_ref, ga_ref, qb_ref, kb_ref, vb_ref, gb_ref):
    x = x_ref[0]
    ms = jnp.mean(x * x, axis=-1, keepdims=True)
    h = x * lax.rsqrt(ms + EPS) * gpre_ref[...]
    h = h * (1.0 + scale_ref[0, 0]) + shift_ref[0, 0]
    hb = h.astype(_BF16)

    def proj(group):
        w = jnp.concatenate([w_ref[:, lo:lo + n] for lo, n in group], axis=1).astype(_BF16)
        return lax.dot_general(w, hb, _TNT, preferred_element_type=_F32)

    cos = cos_ref[...]
    sin = sin_ref[...]
    qn = qn_ref[...]
    kn = kn_ref[...]

    def store_tiled(ref, rows, val):
        tq = ref.shape[3]
        for j in range(ref.shape[1]):
            ref[0, j, rows, :] = val[:, j * tq:(j + 1) * tq]

    qk = proj(_QK_GROUP)
    for hd in range(N_HEADS):
        rows = slice(hd * HEAD_DIM, (hd + 1) * HEAD_DIM)
        store_tiled(qa_ref, rows, (_norm_rope(qk[rows], qn, cos, sin) * Q_SCALE).astype(_BF16))
    store_tiled(qb_ref, slice(0, WIDTH), (qk[WIDTH:2 * WIDTH] * Q_SCALE).astype(_BF16))
    ka = qk[2 * WIDTH:2 * WIDTH + KV_WIDTH]
    ka = jnp.concatenate(
        [_norm_rope(ka[hd * HEAD_DIM:(hd + 1) * HEAD_DIM], kn, cos, sin) for hd in range(N_KV)],
        axis=0)
    ka_ref[0] = ka.T.astype(_BF16)
    kb_ref[0] = qk[2 * WIDTH + KV_WIDTH:].T.astype(_BF16)

    gates = _silu(proj(_GATE_GROUP))
    store_tiled(ga_ref, slice(0, WIDTH), gates[:WIDTH].astype(_BF16))
    store_tiled(gb_ref, slice(0, WIDTH), gates[WIDTH:].astype(_BF16))

    v = proj(_V_GROUP)
    va_ref[0] = v[:KV_WIDTH].astype(_BF16)
    store_tiled(vb_ref, slice(0, KV_WIDTH), v[KV_WIDTH:].astype(_BF16))


def _rope_tables(seq):
    f32 = np.float32
    rows = seq // GRID_W
    row = np.repeat(np.arange(rows), GRID_W).astype(f32)
    col = np.tile(np.arange(GRID_W), rows).astype(f32)
    half = HEAD_DIM // 2
    freqs = np.power(f32(ROPE_THETA), -np.arange(0, half, 2, dtype=f32) / f32(half)).astype(f32)

    def ang2(pos):
        ang = pos[:, None] * freqs[None, :]
        return np.concatenate([ang, ang], axis=-1)

    ang = np.concatenate([ang2(row), ang2(col)], axis=-1)
    return jnp.asarray(np.cos(ang).T), jnp.asarray(np.sin(ang).T)


def _in_proj(x, mod, g_pre, w_in, cos_t, sin_t, qn, kn):
    bsz, seq, d = x.shape
    tm = _TM_IN
    tok_t = lambda n: pl.BlockSpec((1, n, tm), lambda b, i: (b, 0, i))
    tok_m = lambda n: pl.BlockSpec((1, tm, n), lambda b, i: (b, i, 0))
    mod_row = lambda r: pl.BlockSpec((1, 1, 1, d), lambda b, i: (b, r, 0, 0))
    tab = pl.BlockSpec((HEAD_DIM, tm), lambda b, i: (0, i))
    gain = pl.BlockSpec((HEAD_DIM, tm), lambda b, i: (0, 0))
    shp_t = lambda n: jax.ShapeDtypeStruct((bsz, n, seq), _BF16)
    shp_m = lambda n: jax.ShapeDtypeStruct((bsz, seq, n), _BF16)
    tiled = pl.BlockSpec((1, tm // _TQ_A, WIDTH, _TQ_A), lambda b, i: (b, i, 0, 0))
    shp_tiled = jax.ShapeDtypeStruct((bsz, seq // _TQ_A, WIDTH, _TQ_A), _BF16)
    v_blocks = pl.BlockSpec((1, tm // Q_BLOCK, KV_WIDTH, Q_BLOCK), lambda b, i: (b, i, 0, 0))
    shp_v_blocks = jax.ShapeDtypeStruct((bsz, seq // Q_BLOCK, KV_WIDTH, Q_BLOCK), _BF16)
    return pl.pallas_call(
        _inproj_kernel,
        out_shape=(shp_tiled, shp_m(KV_WIDTH), shp_t(KV_WIDTH), shp_tiled,
                   shp_tiled, shp_m(KV_WIDTH), shp_v_blocks, shp_tiled),
        grid=(bsz, seq // tm),
        in_specs=[pl.BlockSpec((1, tm, d), lambda b, i: (b, i, 0)),
                  mod_row(_SHIFT), mod_row(_SCALE),
                  pl.BlockSpec((1, d), lambda b, i: (0, 0)),
                  pl.BlockSpec((d, IN_COLS), lambda b, i: (0, 0)),
                  tab, tab, gain, gain],
        out_specs=(tiled, tok_m(KV_WIDTH), tok_t(KV_WIDTH), tiled,
                   tiled, tok_m(KV_WIDTH), v_blocks, tiled),
        compiler_params=_params(2),
        name="in_proj",
    )(x, mod, mod, g_pre.reshape(1, d), w_in, cos_t, sin_t,
      jnp.broadcast_to(qn[:, None], (HEAD_DIM, tm)),
      jnp.broadcast_to(kn[:, None], (HEAD_DIM, tm)))


_PAIRS = N_HEADS // 2
_ONES_ROWS = 16
_LOOKAHEAD = 2


def _head_rows(hd):
    return slice(hd * HEAD_DIM, (hd + 1) * HEAD_DIM)


def _pair_q(head, pair):
    def padded(hd):
        qh = head(hd)
        z = jnp.zeros_like(qh)
        return jnp.concatenate([qh, z] if hd // GROUP == 0 else [z, qh], axis=0)
    return jnp.concatenate([padded(2 * pair), padded(2 * pair + 1)], axis=1)


def _pair_out(o, inv_l, gate, pair, tq):
    o = o * inv_l
    g0 = gate(2 * pair).astype(_F32)
    g1 = gate(2 * pair + 1).astype(_F32)
    return jnp.concatenate([o[:, :tq] * g0, o[:, tq:] * g1], axis=0).T.astype(_BF16)


def _swap_middle(a):
    b = Q_BLOCK
    return jnp.concatenate([a[:, :b], a[:, 2 * b:3 * b], a[:, b:2 * b], a[:, 3 * b:]], axis=1)


def _v_with_ones(v):
    return jnp.concatenate([v, jnp.ones((_ONES_ROWS, v.shape[1]), v.dtype)], axis=0)


def _mixers_kernel(qa_ref, ka_ref, va_ref, ga_ref, qb_ref, kb_ref, vb_ref, gb_ref,
                   sink_ref, bias_ref, o_ref, sa0_ref, sa1_ref, sb0_ref, sb1_ref):
    n_tiles, tq = qa_ref.shape[1], qa_ref.shape[3]
    seq = ka_ref.shape[1]
    n_blocks = seq // Q_BLOCK
    a_chunks = [slice(c * _TK_A, (c + 1) * _TK_A) for c in range(seq // _TK_A)]
    items = [(kind, p) for p in range(_PAIRS) for kind in ("a", "b")]
    bufs = (sa0_ref, sb0_ref, sa1_ref, sb1_ref) * (_PAIRS // 2)

    def staged(start, size, zero):
        return pl.ds(pl.multiple_of(zero + start, Q_BLOCK), size)

    def window_blocks(tile):
        first = (tq // Q_BLOCK) * tile - 1
        return [jnp.clip(first + j, 0, n_blocks - 1) for j in range(_KEYS_B // Q_BLOCK)]

    def score_chunks(kind, pair, buf, tile, zero):
        q_ref = qa_ref if kind == "a" else qb_ref

        def weights():
            return _pair_q(lambda hd: q_ref[0, tile, _head_rows(hd), :], pair)

        def global_chunk(rows):
            s = jnp.dot(ka_ref[0, rows, :], weights(), preferred_element_type=_F32)
            buf[staged(rows.start, _TK_A, zero), :] = s
            return jnp.max(s, axis=0, keepdims=True)

        def window():
            k = jnp.concatenate(
                [kb_ref[0, pl.ds(pl.multiple_of(blk * Q_BLOCK, Q_BLOCK), Q_BLOCK), :]
                 for blk in window_blocks(tile)], axis=0)
            s = jnp.dot(k, _swap_middle(weights()), preferred_element_type=_F32)

            def bias(keys, queries):
                return jnp.concatenate([bias_ref[2 * pair, keys, queries],
                                        bias_ref[2 * pair + 1, keys, queries]], axis=1)

            left = s[_KEYS_Q0, :tq] + bias(_KEYS_Q0, slice(0, Q_BLOCK))
            right = s[_KEYS_Q1, tq:] + bias(_KEYS_Q1, slice(Q_BLOCK, 2 * Q_BLOCK))
            left = jnp.concatenate(
                [jnp.where(tile == 0, MASK_VALUE, left[:Q_BLOCK]), left[Q_BLOCK:]], axis=0)
            right = jnp.concatenate(
                [right[:-Q_BLOCK], jnp.where(tile == n_tiles - 1, MASK_VALUE, right[-Q_BLOCK:])],
                axis=0)
            buf[staged(_KEYS_Q0.start, _KEYS_Q, zero), :tq] = left
            buf[staged(_KEYS_Q1.start, _KEYS_Q, zero), tq:] = right
            m = jnp.concatenate([jnp.max(left, axis=0, keepdims=True),
                                 jnp.max(right, axis=0, keepdims=True)], axis=1)
            return jnp.maximum(m, sink_ref[pair] * LOG2E)

        if kind == "a":
            return [functools.partial(global_chunk, rows) for rows in a_chunks]
        return [window]

    def pv_chunks(kind, pair, buf, tile, zero, m):
        kv = _head_rows(pair // (GROUP // 2))

        def global_chunk(rows):
            e = jnp.exp2(buf[staged(rows.start, _TK_A, zero), :] - m)
            return jnp.dot(_v_with_ones(va_ref[0, kv, rows]), e.astype(_BF16),
                           preferred_element_type=_F32)

        def window():
            e_left = jnp.exp2(buf[staged(_KEYS_Q0.start, _KEYS_Q, zero), :tq] - m[:, :tq])
            e_right = jnp.exp2(buf[staged(_KEYS_Q1.start, _KEYS_Q, zero), tq:] - m[:, tq:])
            v = jnp.concatenate([vb_ref[0, blk] for blk in window_blocks(tile)], axis=1)
            v = _v_with_ones(v[kv, :])
            return jnp.concatenate(
                [jnp.dot(v[:, _KEYS_Q0], e_left.astype(_BF16), preferred_element_type=_F32),
                 jnp.dot(v[:, _KEYS_Q1], e_right.astype(_BF16), preferred_element_type=_F32)],
                axis=1)

        if kind == "a":
            return [functools.partial(global_chunk, rows) for rows in a_chunks]
        return [window]

    def finish(kind, pair, tile, o, m):
        l = o[HEAD_DIM:HEAD_DIM + 1]
        if kind == "b":
            l = _swap_middle(l + jnp.exp2(sink_ref[pair] * LOG2E - m))
            o = _swap_middle(o)
        g_ref = ga_ref if kind == "a" else gb_ref
        lane0 = (0 if kind == "a" else WIDTH) + pair * 2 * HEAD_DIM
        o_ref[0, pl.ds(pl.multiple_of(tile * tq, tq), tq), lane0:lane0 + 2 * HEAD_DIM] = _pair_out(
            o[:HEAD_DIM], 1.0 / l, lambda hd: g_ref[0, tile, _head_rows(hd), :], pair, tq)

    def col_max(parts):
        return functools.reduce(jnp.maximum, parts)

    def tile_body(tile, carry):
        zero = jnp.minimum(tile, 0)
        next_tile = jnp.minimum(tile + 1, n_tiles - 1)
        col_maxes = list(carry)
        for idx, (kind, pair) in enumerate(items):
            ahead = (idx + _LOOKAHEAD) % len(items)
            ahead_tile = tile if idx + _LOOKAHEAD < len(items) else next_tile
            nxt = score_chunks(*items[ahead], bufs[ahead], ahead_tile, zero)
            m = col_maxes[idx]
            cur = pv_chunks(kind, pair, bufs[idx], tile, zero, m)
            parts, o = [], None
            for i in range(max(len(nxt), len(cur))):
                if i < len(nxt):
                    parts.append(nxt[i]())
                if i < len(cur):
                    pv = cur[i]()
                    o = pv if o is None else o + pv
            finish(kind, pair, tile, o, m)
            col_maxes.append(col_max(parts))
        return tuple(col_maxes[len(items):])

    first = jnp.minimum(pl.program_id(0), 0)
    carry0 = tuple(
        col_max([chunk() for chunk in score_chunks(*items[i], bufs[i], first, first)])
        for i in range(_LOOKAHEAD))
    def two_tiles(i, carry):
        return tile_body(2 * i + 1, tile_body(2 * i, carry))

    lax.fori_loop(0, n_tiles // 2, two_tiles, carry0)


def _mixers(qa, ka, va, ga, qb, kb, vb, gb, sink, bias):
    bsz, n_tiles, _, tq = qa.shape
    seq = ka.shape[1]
    tiled = pl.BlockSpec((1, n_tiles, WIDTH, tq), lambda b: (b, 0, 0, 0))
    k_spec = pl.BlockSpec((1, seq, KV_WIDTH), lambda b: (b, 0, 0))
    const = lambda shape: pl.BlockSpec(shape, lambda b: (0,) * len(shape))
    return pl.pallas_call(
        _mixers_kernel,
        out_shape=jax.ShapeDtypeStruct((bsz, seq, 2 * WIDTH), _BF16),
        grid=(bsz,),
        in_specs=[tiled, k_spec, pl.BlockSpec((1, KV_WIDTH, seq), lambda b: (b, 0, 0)), tiled,
                  tiled, k_spec,
                  pl.BlockSpec((1, seq // Q_BLOCK, KV_WIDTH, Q_BLOCK), lambda b: (b, 0, 0, 0)),
                  tiled,
                  const((_PAIRS, 1, 2 * tq)), const((N_HEADS, _KEYS_B, tq))],
        out_specs=pl.BlockSpec((1, seq, 2 * WIDTH), lambda b: (b, 0, 0)),
        scratch_shapes=([pltpu.VMEM((seq, 2 * tq), _F32)] * 2
                        + [pltpu.VMEM((_KEYS_B, 2 * tq), _F32)] * 2),
        compiler_params=_params(1),
        name="mixers",
    )(qa, ka, va, ga, qb, kb, vb, gb,
      jnp.tile(jnp.broadcast_to(sink[:, None], (N_HEADS, Q_BLOCK)).reshape(_PAIRS, 1, 2 * Q_BLOCK),
               (1, 1, tq // Q_BLOCK)),
      bias)


def _outproj_kernel(a_ref, w_ref, x_ref, gate_ref, gpost_ref, o_ref):
    w = w_ref[...].astype(_BF16)
    starts = np.cumsum((0,) + _TM_OUT_CHUNKS)
    assert starts[-1] == a_ref.shape[1]
    for lo, hi in zip(starts[:-1], starts[1:]):
        rows = slice(int(lo), int(hi))
        y = jnp.dot(a_ref[0, rows, :], w, preferred_element_type=_F32)
        ms = jnp.mean(y * y, axis=-1, keepdims=True)
        yn = y * lax.rsqrt(ms + EPS) * gpost_ref[...]
        o_ref[0, rows, :] = x_ref[0, rows, :] + gate_ref[0, 0] * yn


def _out_proj(o, w_out, x, mod, g_post):
    bsz, seq, d = x.shape
    tm = _TM_OUT
    xo = pl.BlockSpec((1, tm, d), lambda b, i: (b, i, 0))
    return pl.pallas_call(
        _outproj_kernel,
        out_shape=jax.ShapeDtypeStruct((bsz, seq, d), _F32),
        grid=(bsz, seq // tm),
        in_specs=[pl.BlockSpec((1, tm, 2 * WIDTH), lambda b, i: (b, i, 0)),
                  pl.BlockSpec((2 * WIDTH, d), lambda b, i: (0, 0)),
                  xo,
                  pl.BlockSpec((1, 1, 1, d), lambda b, i: (b, _GATE, 0, 0)),
                  pl.BlockSpec((1, d), lambda b, i: (0, 0))],
        out_specs=xo,
        compiler_params=_params(2),
        name="out_proj",
    )(o, w_out, x, mod, g_post.reshape(1, d))


def kernel(x, c, w_ada, b_ada, g_pre, g_post, w_in, qn_a, kn_a, sink_b, w_out, rel_table):
    bsz, seq, d = x.shape
    depth = w_ada.shape[0]
    cos_t, sin_t = _rope_tables(seq)
    bias = _t5_bias_tiles(rel_table)
    for l in range(depth):
        mod = _adaln_mod(c, w_ada[l], b_ada[l]).reshape(bsz, 3, 1, d)
        qa, ka, va, ga, qb, kb, vb, gb = _in_proj(
            x, mod, g_pre[l], w_in[l], cos_t, sin_t, qn_a[l], kn_a[l])
        o = _mixers(qa, ka, va, ga, qb, kb, vb, gb, sink_b[l], bias)
        x = _out_proj(o, w_out[l], x, mod, g_post[l])
    return x
```

```python
import functools

import jax
import jax.numpy as jnp
import numpy as np
from jax import lax
from jax.experimental import pallas as pl
from jax.experimental.pallas import tpu as pltpu

HEAD_DIM = 64
N_HEADS = 8
N_KV = 2
GROUP = N_HEADS // N_KV
WIDTH = N_HEADS * HEAD_DIM
KV_WIDTH = N_KV * HEAD_DIM
IN_COLS = 2 * (2 * WIDTH + 2 * KV_WIDTH)
Q_BLOCK = 128
WINDOW = 128
GRID_W = 64
ROPE_THETA = 10000.0
N_BUCKETS = 32
MAX_DISTANCE = 128
EPS = 1e-6
MASK_VALUE = -1e30
LOG2E = 1.4426950408889634
Q_SCALE = HEAD_DIM ** -0.5 * LOG2E

_QA, _KA, _VA, _GA = 0, WIDTH, WIDTH + KV_WIDTH, WIDTH + 2 * KV_WIDTH
_HALF = 2 * WIDTH + 2 * KV_WIDTH
_QB, _KB, _VB, _GB = _HALF + _QA, _HALF + _KA, _HALF + _VA, _HALF + _GA
_QK_GROUP = ((_QA, WIDTH), (_QB, WIDTH), (_KA, KV_WIDTH), (_KB, KV_WIDTH))
_GATE_GROUP = ((_GA, WIDTH), (_GB, WIDTH))
_V_GROUP = ((_VA, KV_WIDTH), (_VB, KV_WIDTH))

_F32 = jnp.float32
_BF16 = jnp.bfloat16

_VMEM_LIMIT_BYTES = 48 * 1024 * 1024

_TM_IN = 1024
_TQ_A = 256
_TK_A = 256
_KEYS_B = _TQ_A + 2 * WINDOW
_KEYS_Q = Q_BLOCK + 2 * WINDOW
_KEYS_Q0 = slice(0, _KEYS_Q)
_KEYS_Q1 = slice(Q_BLOCK, Q_BLOCK + _KEYS_Q)
_TM_OUT = 1024
_TM_OUT_CHUNKS = (256, 256, 256, 128, 128)
_TN_MOD = 1536

_TNT = (((0,), (1,)), ((), ()))
_SHIFT, _SCALE, _GATE = 0, 1, 2


def _silu(v):
    return 0.5 * v * (1.0 + jnp.tanh(0.5 * v))


def _params(n_grid):
    return pltpu.CompilerParams(
        dimension_semantics=("arbitrary",) * n_grid,
        vmem_limit_bytes=_VMEM_LIMIT_BYTES)


def _mod_kernel(c_ref, w_ref, b_ref, o_ref):
    ca = _silu(c_ref[...])
    o_ref[...] = jnp.dot(ca.astype(_BF16), w_ref[...].astype(_BF16),
                         preferred_element_type=_F32) + b_ref[...]


def _adaln_mod(c, w, b):
    bsz, d = c.shape
    n = w.shape[1]
    assert n % _TN_MOD == 0
    return pl.pallas_call(
        _mod_kernel,
        out_shape=jax.ShapeDtypeStruct((bsz, n), _F32),
        grid=(n // _TN_MOD,),
        in_specs=[pl.BlockSpec((bsz, d), lambda j: (0, 0)),
                  pl.BlockSpec((d, _TN_MOD), lambda j: (0, j)),
                  pl.BlockSpec((1, _TN_MOD), lambda j: (0, j))],
        out_specs=pl.BlockSpec((bsz, _TN_MOD), lambda j: (0, j)),
        compiler_params=_params(1),
        name="adaln_mod",
    )(c, w, b.reshape(1, n))


def _bias_kernel(table_ref, bucket_ref, o_ref):
    masked = jnp.full((Q_BLOCK, Q_BLOCK), MASK_VALUE, _F32)
    for h in range(N_HEADS):
        table = table_ref[h] * LOG2E
        tiles = []
        for d in range(3):
            bucket = bucket_ref[d]
            acc = masked
            for b in range(N_BUCKETS):
                acc = jnp.where(bucket == b, table[b:b + 1, :], acc)
            tiles.append(acc)
        lo, mid, hi = tiles
        for jb, (left, right) in enumerate(((lo, masked), (mid, lo), (hi, mid), (masked, hi))):
            rows = slice(jb * Q_BLOCK, (jb + 1) * Q_BLOCK)
            o_ref[h, rows, 0:Q_BLOCK] = left
            o_ref[h, rows, Q_BLOCK:2 * Q_BLOCK] = right


def _t5_bucket(rel):
    nb = N_BUCKETS // 2
    max_exact = nb // 2
    ret = (rel > 0).astype(np.int32) * nb
    n = np.abs(rel)
    assert MAX_DISTANCE == 2 * max_exact ** 2 and nb - max_exact == 8
    n2 = n.astype(np.int64) ** 2
    large = max_exact + sum((n2 >= 2 ** k).astype(np.int32) for k in range(7, 31))
    large = np.minimum(large, nb - 1)
    return ret + np.where(n < max_exact, n, large)


def _t5_bias_tiles(rel_table):
    r = np.arange(Q_BLOCK)
    rel = ((np.arange(3) - 1)[:, None, None] * Q_BLOCK
           + r[None, :, None] - r[None, None, :])
    bucket = jnp.asarray(np.where(np.abs(rel) <= WINDOW, _t5_bucket(rel), -1).astype(np.int32))
    return pl.pallas_call(
        _bias_kernel,
        out_shape=jax.ShapeDtypeStruct((N_HEADS, _KEYS_B, _TQ_A), _F32),
        in_specs=[pl.BlockSpec(memory_space=pltpu.VMEM),
                  pl.BlockSpec(memory_space=pltpu.VMEM)],
        out_specs=pl.BlockSpec(memory_space=pltpu.VMEM),
        compiler_params=pltpu.CompilerParams(vmem_limit_bytes=_VMEM_LIMIT_BYTES),
        name="t5_bias",
    )(jnp.broadcast_to(rel_table.T[:, :, None], (N_HEADS, N_BUCKETS, Q_BLOCK)), bucket)


def _norm_rope(xh, gain, cos, sin):
    ms = jnp.mean(xh * xh, axis=0, keepdims=True)
    y = xh * lax.rsqrt(ms + EPS) * gain
    q = HEAD_DIM // 4
    rot = jnp.concatenate([-y[q:2 * q], y[0:q], -y[3 * q:4 * q], y[2 * q:3 * q]], axis=0)
    return y * cos + rot * sin


def _inproj_kernel(x_ref, shift_ref, scale_ref, gpre_ref, w_ref, cos_ref, sin_ref,
                   qn_ref, kn_ref,
                   qa_ref, ka_ref, va_ref, ga_ref, qb_ref, kb_ref, vb_ref, gb_ref):
    x = x_ref[0]
    ms = jnp.mean(x * x, axis=-1, keepdims=True)
    h = x * lax.rsqrt(ms + EPS) * gpre_ref[...]
    h = h * (1.0 + scale_ref[0, 0]) + shift_ref[0, 0]
    hb = h.astype(_BF16)

    def proj(group):
        w = jnp.concatenate([w_ref[:, lo:lo + n] for lo, n in group], axis=1).astype(_BF16)
        return lax.dot_general(w, hb, _TNT, preferred_element_type=_F32)

    cos = cos_ref[...]
    sin = sin_ref[...]
    qn = qn_ref[...]
    kn = kn_ref[...]

    def store_tiled(ref, rows, val):
        tq = ref.shape[3]
        for j in range(ref.shape[1]):
            ref[0, j, rows, :] = val[:, j * tq:(j + 1) * tq]

    qk = proj(_QK_GROUP)
    for hd in range(N_HEADS):
        rows = slice(hd * HEAD_DIM, (hd + 1) * HEAD_DIM)
        store_tiled(qa_ref, rows, (_norm_rope(qk[rows], qn, cos, sin) * Q_SCALE).astype(_BF16))
    store_tiled(qb_ref, slice(0, WIDTH), (qk[WIDTH:2 * WIDTH] * Q_SCALE).astype(_BF16))
    ka = qk[2 * WIDTH:2 * WIDTH + KV_WIDTH]
    ka = jnp.concatenate(
        [_norm_rope(ka[hd * HEAD_DIM:(hd + 1) * HEAD_DIM], kn, cos, sin) for hd in range(N_KV)],
        axis=0)
    ka_ref[0] = ka.T.astype(_BF16)
    kb_ref[0] = qk[2 * WIDTH + KV_WIDTH:].T.astype(_BF16)

    gates = _silu(proj(_GATE_GROUP))
    store_tiled(ga_ref, slice(0, WIDTH), gates[:WIDTH].astype(_BF16))
    store_tiled(gb_ref, slice(0, WIDTH), gates[WIDTH:].astype(_BF16))

    v = proj(_V_GROUP)
    va_ref[0] = v[:KV_WIDTH].astype(_BF16)
    store_tiled(vb_ref, slice(0, KV_WIDTH), v[KV_WIDTH:].astype(_BF16))


def _rope_tables(seq):
    f32 = np.float32
    rows = seq // GRID_W
    row = np.repeat(np.arange(rows), GRID_W).astype(f32)
    col = np.tile(np.arange(GRID_W), rows).astype(f32)
    half = HEAD_DIM // 2
    freqs = np.power(f32(ROPE_THETA), -np.arange(0, half, 2, dtype=f32) / f32(half)).astype(f32)

    def ang2(pos):
        ang = pos[:, None] * freqs[None, :]
        return np.concatenate([ang, ang], axis=-1)

    ang = np.concatenate([ang2(row), ang2(col)], axis=-1)
    return jnp.asarray(np.cos(ang).T), jnp.asarray(np.sin(ang).T)


def _in_proj(x, mod, g_pre, w_in, cos_t, sin_t, qn, kn):
    bsz, seq, d = x.shape
    tm = _TM_IN
    assert seq % tm == 0 and tm % _TQ_A == 0 and w_in.shape == (d, IN_COLS)
    tok_t = lambda n: pl.BlockSpec((1, n, tm), lambda b, i: (b, 0, i))
    tok_m = lambda n: pl.BlockSpec((1, tm, n), lambda b, i: (b, i, 0))
    mod_row = lambda r: pl.BlockSpec((1, 1, 1, d), lambda b, i: (b, r, 0, 0))
    tab = pl.BlockSpec((HEAD_DIM, tm), lambda b, i: (0, i))
    gain = pl.BlockSpec((HEAD_DIM, tm), lambda b, i: (0, 0))
    shp_t = lambda n: jax.ShapeDtypeStruct((bsz, n, seq), _BF16)
    shp_m = lambda n: jax.ShapeDtypeStruct((bsz, seq, n), _BF16)
    tiled = pl.BlockSpec((1, tm // _TQ_A, WIDTH, _TQ_A), lambda b, i: (b, i, 0, 0))
    shp_tiled = jax.ShapeDtypeStruct((bsz, seq // _TQ_A, WIDTH, _TQ_A), _BF16)
    v_blocks = pl.BlockSpec((1, tm // Q_BLOCK, KV_WIDTH, Q_BLOCK), lambda b, i: (b, i, 0, 0))
    shp_v_blocks = jax.ShapeDtypeStruct((bsz, seq // Q_BLOCK, KV_WIDTH, Q_BLOCK), _BF16)
    return pl.pallas_call(
        _inproj_kernel,
        out_shape=(shp_tiled, shp_m(KV_WIDTH), shp_t(KV_WIDTH), shp_tiled,
                   shp_tiled, shp_m(KV_WIDTH), shp_v_blocks, shp_tiled),
        grid=(bsz, seq // tm),
        in_specs=[pl.BlockSpec((1, tm, d), lambda b, i: (b, i, 0)),
                  mod_row(_SHIFT), mod_row(_SCALE),
                  pl.BlockSpec((1, d), lambda b, i: (0, 0)),
                  pl.BlockSpec((d, IN_COLS), lambda b, i: (0, 0)),
                  tab, tab, gain, gain],
        out_specs=(tiled, tok_m(KV_WIDTH), tok_t(KV_WIDTH), tiled,
                   tiled, tok_m(KV_WIDTH), v_blocks, tiled),
        compiler_params=_params(2),
        name="in_proj",
    )(x, mod, mod, g_pre.reshape(1, d), w_in, cos_t, sin_t,
      jnp.broadcast_to(qn[:, None], (HEAD_DIM, tm)),
      jnp.broadcast_to(kn[:, None], (HEAD_DIM, tm)))


_PAIRS = N_HEADS // 2
_ONES_ROWS = 16
_LOOKAHEAD = 2
_TILES_PER_TRIP = 2


def _head_rows(hd):
    return slice(hd * HEAD_DIM, (hd + 1) * HEAD_DIM)


def _pair_q(head, pair):
    def padded(hd):
        qh = head(hd)
        z = jnp.zeros_like(qh)
        return jnp.concatenate([qh, z] if hd // GROUP == 0 else [z, qh], axis=0)
    return jnp.concatenate([padded(2 * pair), padded(2 * pair + 1)], axis=1)


def _pair_out(o, inv_l, gate, pair, tq):
    o = o * inv_l
    g0 = gate(2 * pair).astype(_F32)
    g1 = gate(2 * pair + 1).astype(_F32)
    return jnp.concatenate([o[:, :tq] * g0, o[:, tq:] * g1], axis=0).T.astype(_BF16)


def _swap_middle(a):
    b = Q_BLOCK
    return jnp.concatenate([a[:, :b], a[:, 2 * b:3 * b], a[:, b:2 * b], a[:, 3 * b:]], axis=1)


def _v_with_ones(v):
    return jnp.concatenate([v, jnp.ones((_ONES_ROWS, v.shape[1]), v.dtype)], axis=0)


def _mixers_kernel(qa_ref, ka_ref, va_ref, ga_ref, qb_ref, kb_ref, vb_ref, gb_ref,
                   sink_ref, bias_ref, o_ref, sa0_ref, sa1_ref, sb0_ref, sb1_ref):
    n_tiles, tq = qa_ref.shape[1], qa_ref.shape[3]
    seq = ka_ref.shape[1]
    n_blocks = seq // Q_BLOCK
    a_chunks = [slice(c * _TK_A, (c + 1) * _TK_A) for c in range(seq // _TK_A)]
    items = [(kind, p) for p in range(_PAIRS) for kind in ("a", "b")]
    bufs = (sa0_ref, sb0_ref, sa1_ref, sb1_ref) * (_PAIRS // 2)

    def staged(start, size, zero):
        return pl.ds(pl.multiple_of(zero + start, Q_BLOCK), size)

    def window_blocks(tile):
        first = (tq // Q_BLOCK) * tile - 1
        return [jnp.clip(first + j, 0, n_blocks - 1) for j in range(_KEYS_B // Q_BLOCK)]

    def score_chunks(kind, pair, buf, tile, zero):
        q_ref = qa_ref if kind == "a" else qb_ref

        def weights():
            return _pair_q(lambda hd: q_ref[0, tile, _head_rows(hd), :], pair)

        def global_chunk(rows):
            s = jnp.dot(ka_ref[0, rows, :], weights(), preferred_element_type=_F32)
            buf[staged(rows.start, _TK_A, zero), :] = s
            return jnp.max(s, axis=0, keepdims=True)

        def window():
            k = jnp.concatenate(
                [kb_ref[0, pl.ds(pl.multiple_of(blk * Q_BLOCK, Q_BLOCK), Q_BLOCK), :]
                 for blk in window_blocks(tile)], axis=0)
            w = _swap_middle(weights())

            def bias(keys, queries):
                return jnp.concatenate([bias_ref[2 * pair, keys, queries],
                                        bias_ref[2 * pair + 1, keys, queries]], axis=1)

            left = (jnp.dot(k[_KEYS_Q0], w[:, :tq], preferred_element_type=_F32)
                    + bias(_KEYS_Q0, slice(0, Q_BLOCK)))
            right = (jnp.dot(k[_KEYS_Q1], w[:, tq:], preferred_element_type=_F32)
                     + bias(_KEYS_Q1, slice(Q_BLOCK, 2 * Q_BLOCK)))
            left = jnp.concatenate(
                [jnp.where(tile == 0, MASK_VALUE, left[:Q_BLOCK]), left[Q_BLOCK:]], axis=0)
            right = jnp.concatenate(
                [right[:-Q_BLOCK], jnp.where(tile == n_tiles - 1, MASK_VALUE, right[-Q_BLOCK:])],
                axis=0)
            buf[staged(_KEYS_Q0.start, _KEYS_Q, zero), :tq] = left
            buf[staged(_KEYS_Q1.start, _KEYS_Q, zero), tq:] = right
            m = jnp.concatenate([jnp.max(left, axis=0, keepdims=True),
                                 jnp.max(right, axis=0, keepdims=True)], axis=1)
            return jnp.maximum(m, sink_ref[pair] * LOG2E)

        if kind == "a":
            return [functools.partial(global_chunk, rows) for rows in a_chunks]
        return [window]

    def pv_chunks(kind, pair, buf, tile, zero, m):
        kv = _head_rows(pair // (GROUP // 2))

        def global_chunk(rows):
            e = jnp.exp2(buf[staged(rows.start, _TK_A, zero), :] - m)
            return jnp.dot(_v_with_ones(va_ref[0, kv, rows]), e.astype(_BF16),
                           preferred_element_type=_F32)

        def window():
            e_left = jnp.exp2(buf[staged(_KEYS_Q0.start, _KEYS_Q, zero), :tq] - m[:, :tq])
            e_right = jnp.exp2(buf[staged(_KEYS_Q1.start, _KEYS_Q, zero), tq:] - m[:, tq:])
            v = jnp.concatenate([vb_ref[0, blk] for blk in window_blocks(tile)], axis=1)
            v = _v_with_ones(v[kv, :])
            return jnp.concatenate(
                [jnp.dot(v[:, _KEYS_Q0], e_left.astype(_BF16), preferred_element_type=_F32),
                 jnp.dot(v[:, _KEYS_Q1], e_right.astype(_BF16), preferred_element_type=_F32)],
                axis=1)

        if kind == "a":
            return [functools.partial(global_chunk, rows) for rows in a_chunks]
        return [window]

    def finish(kind, pair, tile, o, m):
        l = o[HEAD_DIM:HEAD_DIM + 1]
        if kind == "b":
            l = _swap_middle(l + jnp.exp2(sink_ref[pair] * LOG2E - m))
            o = _swap_middle(o)
        g_ref = ga_ref if kind == "a" else gb_ref
        lane0 = (0 if kind == "a" else WIDTH) + pair * 2 * HEAD_DIM
        o_ref[0, pl.ds(pl.multiple_of(tile * tq, tq), tq), lane0:lane0 + 2 * HEAD_DIM] = _pair_out(
            o[:HEAD_DIM], 1.0 / l, lambda hd: g_ref[0, tile, _head_rows(hd), :], pair, tq)

    def col_max(parts):
        return functools.reduce(jnp.maximum, parts)

    def tile_body(tile, carry):
        zero = jnp.minimum(tile, 0)
        next_tile = jnp.minimum(tile + 1, n_tiles - 1)
        col_maxes = list(carry)
        for idx, (kind, pair) in enumerate(items):
            ahead = (idx + _LOOKAHEAD) % len(items)
            ahead_tile = tile if idx + _LOOKAHEAD < len(items) else next_tile
            nxt = score_chunks(*items[ahead], bufs[ahead], ahead_tile, zero)
            m = col_maxes[idx]
            cur = pv_chunks(kind, pair, bufs[idx], tile, zero, m)
            parts, o = [], None
            for i in range(max(len(nxt), len(cur))):
                if i < len(nxt):
                    parts.append(nxt[i]())
                if i < len(cur):
                    pv = cur[i]()
                    o = pv if o is None else o + pv
            finish(kind, pair, tile, o, m)
            col_maxes.append(col_max(parts))
        return tuple(col_maxes[len(items):])

    first = jnp.minimum(pl.program_id(0), 0)
    carry0 = tuple(
        col_max([chunk() for chunk in score_chunks(*items[i], bufs[i], first, first)])
        for i in range(_LOOKAHEAD))
    def trip(i, carry):
        for j in range(_TILES_PER_TRIP):
            carry = tile_body(_TILES_PER_TRIP * i + j, carry)
        return carry

    lax.fori_loop(0, n_tiles // _TILES_PER_TRIP, trip, carry0)


def _mixers(qa, ka, va, ga, qb, kb, vb, gb, sink, bias):
    bsz, n_tiles, _, tq = qa.shape
    seq = ka.shape[1]
    assert tq == 2 * Q_BLOCK == 2 * WINDOW and seq % _TK_A == 0
    assert n_tiles % _TILES_PER_TRIP == 0
    tiled = pl.BlockSpec((1, n_tiles, WIDTH, tq), lambda b: (b, 0, 0, 0))
    k_spec = pl.BlockSpec((1, seq, KV_WIDTH), lambda b: (b, 0, 0))
    const = lambda shape: pl.BlockSpec(shape, lambda b: (0,) * len(shape))
    return pl.pallas_call(
        _mixers_kernel,
        out_shape=jax.ShapeDtypeStruct((bsz, seq, 2 * WIDTH), _BF16),
        grid=(bsz,),
        in_specs=[tiled, k_spec, pl.BlockSpec((1, KV_WIDTH, seq), lambda b: (b, 0, 0)), tiled,
                  tiled, k_spec,
                  pl.BlockSpec((1, seq // Q_BLOCK, KV_WIDTH, Q_BLOCK), lambda b: (b, 0, 0, 0)),
                  tiled,
                  const((_PAIRS, 1, 2 * tq)), const((N_HEADS, _KEYS_B, tq))],
        out_specs=pl.BlockSpec((1, seq, 2 * WIDTH), lambda b: (b, 0, 0)),
        scratch_shapes=([pltpu.VMEM((seq, 2 * tq), _F32)] * 2
                        + [pltpu.VMEM((_KEYS_B, 2 * tq), _F32)] * 2),
        compiler_params=_params(1),
        name="mixers",
    )(qa, ka, va, ga, qb, kb, vb, gb,
      jnp.tile(jnp.broadcast_to(sink[:, None], (N_HEADS, Q_BLOCK)).reshape(_PAIRS, 1, 2 * Q_BLOCK),
               (1, 1, tq // Q_BLOCK)),
      bias)


def _outproj_kernel(a_ref, w_ref, x_ref, gate_ref, gpost_ref, o_ref):
    w = w_ref[...].astype(_BF16)
    starts = np.cumsum((0,) + _TM_OUT_CHUNKS)
    assert starts[-1] == a_ref.shape[1]
    for lo, hi in zip(starts[:-1], starts[1:]):
        rows = slice(int(lo), int(hi))
        y = jnp.dot(a_ref[0, rows, :], w, preferred_element_type=_F32)
        ms = jnp.mean(y * y, axis=-1, keepdims=True)
        yn = y * lax.rsqrt(ms + EPS) * gpost_ref[...]
        o_ref[0, rows, :] = x_ref[0, rows, :] + gate_ref[0, 0] * yn


def _out_proj(o, w_out, x, mod, g_post):
    bsz, seq, d = x.shape
    tm = _TM_OUT
    assert seq % tm == 0 and w_out.shape == (2 * WIDTH, d)
    xo = pl.BlockSpec((1, tm, d), lambda b, i: (b, i, 0))
    return pl.pallas_call(
        _outproj_kernel,
        out_shape=jax.ShapeDtypeStruct((bsz, seq, d), _F32),
        grid=(bsz, seq // tm),
        in_specs=[pl.BlockSpec((1, tm, 2 * WIDTH), lambda b, i: (b, i, 0)),
                  pl.BlockSpec((2 * WIDTH, d), lambda b, i: (0, 0)),
                  xo,
                  pl.BlockSpec((1, 1, 1, d), lambda b, i: (b, _GATE, 0, 0)),
                  pl.BlockSpec((1, d), lambda b, i: (0, 0))],
        out_specs=xo,
        compiler_params=_params(2),
        name="out_proj",
    )(o, w_out, x, mod, g_post.reshape(1, d))


def kernel(x, c, w_ada, b_ada, g_pre, g_post, w_in, qn_a, kn_a, sink_b, w_out, rel_table):
    bsz, seq, d = x.shape
    depth = w_ada.shape[0]
    cos_t, sin_t = _rope_tables(seq)
    bias = _t5_bias_tiles(rel_table)
    for l in range(depth):
        mod = _adaln_mod(c, w_ada[l], b_ada[l]).reshape(bsz, 3, 1, d)
        qa, ka, va, ga, qb, kb, vb, gb = _in_proj(
            x, mod, g_pre[l], w_in[l], cos_t, sin_t, qn_a[l], kn_a[l])
        o = _mixers(qa, ka, va, ga, qb, kb, vb, gb, sink_b[l], bias)
        x = _out_proj(o, w_out[l], x, mod, g_post[l])
    return x
```

```python
import functools

import jax
import jax.numpy as jnp
import numpy as np
from jax import lax
from jax.experimental import pallas as pl
from jax.experimental.pallas import tpu as pltpu

HEAD_DIM = 64
N_HEADS = 8
N_KV = 2
GROUP = N_HEADS // N_KV
WIDTH = N_HEADS * HEAD_DIM
KV_WIDTH = N_KV * HEAD_DIM
IN_COLS = 2 * (2 * WIDTH + 2 * KV_WIDTH)
Q_BLOCK = 128
WINDOW = 128
GRID_W = 64
ROPE_THETA = 10000.0
N_BUCKETS = 32
MAX_DISTANCE = 128
EPS = 1e-6
MASK_VALUE = -1e30
LOG2E = 1.4426950408889634
Q_SCALE = HEAD_DIM ** -0.5 * LOG2E

_QA, _KA, _VA, _GA = 0, WIDTH, WIDTH + KV_WIDTH, WIDTH + 2 * KV_WIDTH
_HALF = 2 * WIDTH + 2 * KV_WIDTH
_QB, _KB, _VB, _GB = _HALF + _QA, _HALF + _KA, _HALF + _VA, _HALF + _GA
_QK_GROUP = ((_QA, WIDTH), (_QB, WIDTH), (_KA, KV_WIDTH), (_KB, KV_WIDTH))
_GATE_GROUP = ((_GA, WIDTH), (_GB, WIDTH))
_V_GROUP = ((_VA, KV_WIDTH), (_VB, KV_WIDTH))

_F32 = jnp.float32
_BF16 = jnp.bfloat16

_VMEM_BYTES_V7X = 64 * 1024 * 1024
_VMEM_LIMIT_BYTES = 48 * 1024 * 1024

_TM_IN = 1024
_TQ_A = 256
_TK_A = 256
_KEYS_B = _TQ_A + 2 * WINDOW
_KEYS_Q = Q_BLOCK + 2 * WINDOW
_KEYS_Q0 = slice(0, _KEYS_Q)
_KEYS_Q1 = slice(Q_BLOCK, Q_BLOCK + _KEYS_Q)
_TM_OUT = 2048
_TM_OUT_CHUNKS = (256,) * 7 + (128, 128)
_TN_MOD = 1536

_TNT = (((0,), (1,)), ((), ()))
_SHIFT, _SCALE, _GATE = 0, 1, 2


def _silu(v):
    return 0.5 * v * (1.0 + jnp.tanh(0.5 * v))


def _params(n_grid, vmem_limit_bytes=_VMEM_LIMIT_BYTES):
    return pltpu.CompilerParams(
        dimension_semantics=("arbitrary",) * n_grid,
        vmem_limit_bytes=vmem_limit_bytes)


def _mod_kernel(c_ref, w_ref, b_ref, o_ref):
    ca = _silu(c_ref[...])
    o_ref[...] = jnp.dot(ca.astype(_BF16), w_ref[...].astype(_BF16),
                         preferred_element_type=_F32) + b_ref[...]


def _adaln_mod(c, w, b):
    bsz, d = c.shape
    n = w.shape[1]
    assert n % _TN_MOD == 0
    return pl.pallas_call(
        _mod_kernel,
        out_shape=jax.ShapeDtypeStruct((bsz, n), _F32),
        grid=(n // _TN_MOD,),
        in_specs=[pl.BlockSpec((bsz, d), lambda j: (0, 0)),
                  pl.BlockSpec((d, _TN_MOD), lambda j: (0, j)),
                  pl.BlockSpec((1, _TN_MOD), lambda j: (0, j))],
        out_specs=pl.BlockSpec((bsz, _TN_MOD), lambda j: (0, j)),
        compiler_params=_params(1),
        name="adaln_mod",
    )(c, w, b.reshape(1, n))


def _bias_kernel(present, table_ref, bucket_ref, o_ref):
    masked = jnp.full((Q_BLOCK, Q_BLOCK), MASK_VALUE, _F32)
    for h in range(N_HEADS):
        table = table_ref[h] * LOG2E
        tiles = []
        for d in range(3):
            bucket = bucket_ref[d]
            acc = masked
            for b in present[d]:
                acc = jnp.where(bucket == b, table[b:b + 1, :], acc)
            tiles.append(acc)
        lo, mid, hi = tiles
        for jb, (left, right) in enumerate(((lo, masked), (mid, lo), (hi, mid), (masked, hi))):
            rows = slice(jb * Q_BLOCK, (jb + 1) * Q_BLOCK)
            o_ref[h, rows, 0:Q_BLOCK] = left
            o_ref[h, rows, Q_BLOCK:2 * Q_BLOCK] = right


def _t5_bucket(rel):
    nb = N_BUCKETS // 2
    max_exact = nb // 2
    ret = (rel > 0).astype(np.int32) * nb
    n = np.abs(rel)
    assert MAX_DISTANCE == 2 * max_exact ** 2 and nb - max_exact == 8
    n2 = n.astype(np.int64) ** 2
    large = max_exact + sum((n2 >= 2 ** k).astype(np.int32) for k in range(7, 31))
    large = np.minimum(large, nb - 1)
    return ret + np.where(n < max_exact, n, large)


def _t5_bias_tiles(rel_table):
    r = np.arange(Q_BLOCK)
    rel = ((np.arange(3) - 1)[:, None, None] * Q_BLOCK
           + r[None, :, None] - r[None, None, :])
    bucket = np.where(np.abs(rel) <= WINDOW, _t5_bucket(rel), -1).astype(np.int32)
    present = tuple(tuple(int(b) for b in np.unique(tile) if b >= 0) for tile in bucket)
    return pl.pallas_call(
        functools.partial(_bias_kernel, present),
        out_shape=jax.ShapeDtypeStruct((N_HEADS, _KEYS_B, _TQ_A), _F32),
        in_specs=[pl.BlockSpec(memory_space=pltpu.VMEM),
                  pl.BlockSpec(memory_space=pltpu.VMEM)],
        out_specs=pl.BlockSpec(memory_space=pltpu.VMEM),
        compiler_params=pltpu.CompilerParams(vmem_limit_bytes=_VMEM_LIMIT_BYTES),
        name="t5_bias",
    )(jnp.broadcast_to(rel_table.T[:, :, None], (N_HEADS, N_BUCKETS, Q_BLOCK)),
      jnp.asarray(bucket))


def _norm_rope(xh, gain, cos, sin):
    ms = jnp.mean(xh * xh, axis=0, keepdims=True)
    y = xh * lax.rsqrt(ms + EPS) * gain
    q = HEAD_DIM // 4
    rot = jnp.concatenate([-y[q:2 * q], y[0:q], -y[3 * q:4 * q], y[2 * q:3 * q]], axis=0)
    return y * cos + rot * sin


def _inproj_kernel(x_ref, shift_ref, scale_ref, gpre_ref, w_ref, cos_ref, sin_ref,
                   qn_ref, kn_ref,
                   qa_ref, ka_ref, va_ref, ga_ref, qb_ref, kb_ref, vb_ref, gb_ref):
    x = x_ref[0]
    ms = jnp.mean(x * x, axis=-1, keepdims=True)
    h = x * lax.rsqrt(ms + EPS) * gpre_ref[...]
    h = h * (1.0 + scale_ref[0, 0]) + shift_ref[0, 0]
    hb = h.astype(_BF16)

    def proj(group):
        w = jnp.concatenate([w_ref[:, lo:lo + n] for lo, n in group], axis=1).astype(_BF16)
        return lax.dot_general(w, hb, _TNT, preferred_element_type=_F32)

    cos = cos_ref[...]
    sin = sin_ref[...]
    qn = qn_ref[...]
    kn = kn_ref[...]

    def store_tiled(ref, rows, val):
        tq = ref.shape[3]
        for j in range(ref.shape[1]):
            ref[0, j, rows, :] = val[:, j * tq:(j + 1) * tq]

    qk = proj(_QK_GROUP)
    for hd in range(N_HEADS):
        rows = slice(hd * HEAD_DIM, (hd + 1) * HEAD_DIM)
        store_tiled(qa_ref, rows, (_norm_rope(qk[rows], qn, cos, sin) * Q_SCALE).astype(_BF16))
    store_tiled(qb_ref, slice(0, WIDTH), (qk[WIDTH:2 * WIDTH] * Q_SCALE).astype(_BF16))
    ka = qk[2 * WIDTH:2 * WIDTH + KV_WIDTH]
    ka = jnp.concatenate(
        [_norm_rope(ka[hd * HEAD_DIM:(hd + 1) * HEAD_DIM], kn, cos, sin) for hd in range(N_KV)],
        axis=0)
    ka_ref[0] = ka.T.astype(_BF16)
    kb_ref[0] = qk[2 * WIDTH + KV_WIDTH:].T.astype(_BF16)

    gates = _silu(proj(_GATE_GROUP))
    store_tiled(ga_ref, slice(0, WIDTH), gates[:WIDTH].astype(_BF16))
    store_tiled(gb_ref, slice(0, WIDTH), gates[WIDTH:].astype(_BF16))

    v = proj(_V_GROUP)
    va_ref[0] = v[:KV_WIDTH].astype(_BF16)
    store_tiled(vb_ref, slice(0, KV_WIDTH), v[KV_WIDTH:].astype(_BF16))


def _rope_tables(seq):
    f32 = np.float32
    rows = seq // GRID_W
    row = np.repeat(np.arange(rows), GRID_W).astype(f32)
    col = np.tile(np.arange(GRID_W), rows).astype(f32)
    half = HEAD_DIM // 2
    freqs = np.power(f32(ROPE_THETA), -np.arange(0, half, 2, dtype=f32) / f32(half)).astype(f32)

    def ang2(pos):
        ang = pos[:, None] * freqs[None, :]
        return np.concatenate([ang, ang], axis=-1)

    ang = np.concatenate([ang2(row), ang2(col)], axis=-1)
    return jnp.asarray(np.cos(ang).T), jnp.asarray(np.sin(ang).T)


def _in_proj(x, mod, g_pre, w_in, cos_t, sin_t, qn, kn):
    bsz, seq, d = x.shape
    tm = _TM_IN
    assert seq % tm == 0 and tm % _TQ_A == 0 and w_in.shape == (d, IN_COLS)
    tok_t = lambda n: pl.BlockSpec((1, n, tm), lambda b, i: (b, 0, i))
    tok_m = lambda n: pl.BlockSpec((1, tm, n), lambda b, i: (b, i, 0))
    mod_row = lambda r: pl.BlockSpec((1, 1, 1, d), lambda b, i: (b, r, 0, 0))
    tab = pl.BlockSpec((HEAD_DIM, tm), lambda b, i: (0, i))
    gain = pl.BlockSpec((HEAD_DIM, tm), lambda b, i: (0, 0))
    shp_t = lambda n: jax.ShapeDtypeStruct((bsz, n, seq), _BF16)
    shp_m = lambda n: jax.ShapeDtypeStruct((bsz, seq, n), _BF16)
    tiled = pl.BlockSpec((1, tm // _TQ_A, WIDTH, _TQ_A), lambda b, i: (b, i, 0, 0))
    shp_tiled = jax.ShapeDtypeStruct((bsz, seq // _TQ_A, WIDTH, _TQ_A), _BF16)
    v_blocks = pl.BlockSpec((1, tm // Q_BLOCK, KV_WIDTH, Q_BLOCK), lambda b, i: (b, i, 0, 0))
    shp_v_blocks = jax.ShapeDtypeStruct((bsz, seq // Q_BLOCK, KV_WIDTH, Q_BLOCK), _BF16)
    return pl.pallas_call(
        _inproj_kernel,
        out_shape=(shp_tiled, shp_m(KV_WIDTH), shp_t(KV_WIDTH), shp_tiled,
                   shp_tiled, shp_m(KV_WIDTH), shp_v_blocks, shp_tiled),
        grid=(bsz, seq // tm),
        in_specs=[pl.BlockSpec((1, tm, d), lambda b, i: (b, i, 0)),
                  mod_row(_SHIFT), mod_row(_SCALE),
                  pl.BlockSpec((1, d), lambda b, i: (0, 0)),
                  pl.BlockSpec((d, IN_COLS), lambda b, i: (0, 0)),
                  tab, tab, gain, gain],
        out_specs=(tiled, tok_m(KV_WIDTH), tok_t(KV_WIDTH), tiled,
                   tiled, tok_m(KV_WIDTH), v_blocks, tiled),
        compiler_params=_params(2),
        name="in_proj",
    )(x, mod, mod, g_pre.reshape(1, d), w_in, cos_t, sin_t,
      jnp.broadcast_to(qn[:, None], (HEAD_DIM, tm)),
      jnp.broadcast_to(kn[:, None], (HEAD_DIM, tm)))


_PAIRS = N_HEADS // 2
_ONES_ROWS = 16
_LOOKAHEAD = 2
_TILES_PER_TRIP = 2


def _head_rows(hd):
    return slice(hd * HEAD_DIM, (hd + 1) * HEAD_DIM)


def _pair_q(head, pair):
    def padded(hd):
        qh = head(hd)
        z = jnp.zeros_like(qh)
        return jnp.concatenate([qh, z] if hd // GROUP == 0 else [z, qh], axis=0)
    return jnp.concatenate([padded(2 * pair), padded(2 * pair + 1)], axis=1)


def _pair_out(o, inv_l, gate, pair, tq):
    o = o * inv_l
    g0 = gate(2 * pair).astype(_F32)
    g1 = gate(2 * pair + 1).astype(_F32)
    return jnp.concatenate([o[:, :tq] * g0, o[:, tq:] * g1], axis=0).T.astype(_BF16)


def _swap_middle(a):
    b = Q_BLOCK
    return jnp.concatenate([a[:, :b], a[:, 2 * b:3 * b], a[:, b:2 * b], a[:, 3 * b:]], axis=1)


def _v_with_ones(v):
    return jnp.concatenate([v, jnp.ones((_ONES_ROWS, v.shape[1]), v.dtype)], axis=0)


def _mixers_kernel(qa_ref, ka_ref, va_ref, ga_ref, qb_ref, kb_ref, vb_ref, gb_ref,
                   sink_ref, bias_ref, o_ref, sa0_ref, sa1_ref, sb0_ref, sb1_ref):
    n_tiles, tq = qa_ref.shape[1], qa_ref.shape[3]
    seq = ka_ref.shape[1]
    n_blocks = seq // Q_BLOCK
    a_chunks = [slice(c * _TK_A, (c + 1) * _TK_A) for c in range(seq // _TK_A)]
    items = [(kind, p) for p in range(_PAIRS) for kind in ("a", "b")]
    bufs = (sa0_ref, sb0_ref, sa1_ref, sb1_ref) * (_PAIRS // 2)

    def staged(start, size, zero):
        return pl.ds(pl.multiple_of(zero + start, Q_BLOCK), size)

    def window_blocks(tile):
        first = (tq // Q_BLOCK) * tile - 1
        return [jnp.clip(first + j, 0, n_blocks - 1) for j in range(_KEYS_B // Q_BLOCK)]

    def score_chunks(kind, pair, buf, tile, zero):
        q_ref = qa_ref if kind == "a" else qb_ref

        def weights():
            return _pair_q(lambda hd: q_ref[0, tile, _head_rows(hd), :], pair)

        def global_chunk(rows):
            s = jnp.dot(ka_ref[0, rows, :], weights(), preferred_element_type=_F32)
            buf[staged(rows.start, _TK_A, zero), :] = s
            return jnp.max(s, axis=0, keepdims=True)

        def window():
            k = jnp.concatenate(
                [kb_ref[0, pl.ds(pl.multiple_of(blk * Q_BLOCK, Q_BLOCK), Q_BLOCK), :]
                 for blk in window_blocks(tile)], axis=0)
            w = _swap_middle(weights())

            def bias(keys, queries):
                return jnp.concatenate([bias_ref[2 * pair, keys, queries],
                                        bias_ref[2 * pair + 1, keys, queries]], axis=1)

            left = (jnp.dot(k[_KEYS_Q0], w[:, :tq], preferred_element_type=_F32)
                    + bias(_KEYS_Q0, slice(0, Q_BLOCK)))
            right = (jnp.dot(k[_KEYS_Q1], w[:, tq:], preferred_element_type=_F32)
                     + bias(_KEYS_Q1, slice(Q_BLOCK, 2 * Q_BLOCK)))
            left = jnp.concatenate(
                [jnp.where(tile == 0, MASK_VALUE, left[:Q_BLOCK]), left[Q_BLOCK:]], axis=0)
            right = jnp.concatenate(
                [right[:-Q_BLOCK], jnp.where(tile == n_tiles - 1, MASK_VALUE, right[-Q_BLOCK:])],
                axis=0)
            buf[staged(_KEYS_Q0.start, _KEYS_Q, zero), :tq] = left
            buf[staged(_KEYS_Q1.start, _KEYS_Q, zero), tq:] = right
            m = jnp.concatenate([jnp.max(left, axis=0, keepdims=True),
                                 jnp.max(right, axis=0, keepdims=True)], axis=1)
            return jnp.maximum(m, sink_ref[pair] * LOG2E)

        if kind == "a":
            return [functools.partial(global_chunk, rows) for rows in a_chunks]
        return [window]

    def pv_chunks(kind, pair, buf, tile, zero, m):
        kv = _head_rows(pair // (GROUP // 2))

        def global_chunk(rows):
            e = jnp.exp2(buf[staged(rows.start, _TK_A, zero), :] - m)
            return jnp.dot(_v_with_ones(va_ref[0, kv, rows]), e.astype(_BF16),
                           preferred_element_type=_F32)

        def window():
            e_left = jnp.exp2(buf[staged(_KEYS_Q0.start, _KEYS_Q, zero), :tq] - m[:, :tq])
            e_right = jnp.exp2(buf[staged(_KEYS_Q1.start, _KEYS_Q, zero), tq:] - m[:, tq:])
            v = jnp.concatenate([vb_ref[0, blk] for blk in window_blocks(tile)], axis=1)
            v = _v_with_ones(v[kv, :])
            return jnp.concatenate(
                [jnp.dot(v[:, _KEYS_Q0], e_left.astype(_BF16), preferred_element_type=_F32),
                 jnp.dot(v[:, _KEYS_Q1], e_right.astype(_BF16), preferred_element_type=_F32)],
                axis=1)

        if kind == "a":
            return [functools.partial(global_chunk, rows) for rows in a_chunks]
        return [window]

    def finish(kind, pair, tile, o, m):
        l = o[HEAD_DIM:HEAD_DIM + 1]
        if kind == "b":
            l = _swap_middle(l + jnp.exp2(sink_ref[pair] * LOG2E - m))
            o = _swap_middle(o)
        g_ref = ga_ref if kind == "a" else gb_ref
        lane0 = (0 if kind == "a" else WIDTH) + pair * 2 * HEAD_DIM
        o_ref[0, pl.ds(pl.multiple_of(tile * tq, tq), tq), lane0:lane0 + 2 * HEAD_DIM] = _pair_out(
            o[:HEAD_DIM], 1.0 / l, lambda hd: g_ref[0, tile, _head_rows(hd), :], pair, tq)

    def col_max(parts):
        return functools.reduce(jnp.maximum, parts)

    def tile_body(tile, carry):
        zero = jnp.minimum(tile, 0)
        next_tile = jnp.minimum(tile + 1, n_tiles - 1)
        col_maxes = list(carry)
        for idx, (kind, pair) in enumerate(items):
            ahead = (idx + _LOOKAHEAD) % len(items)
            ahead_tile = tile if idx + _LOOKAHEAD < len(items) else next_tile
            nxt = score_chunks(*items[ahead], bufs[ahead], ahead_tile, zero)
            m = col_maxes[idx]
            cur = pv_chunks(kind, pair, bufs[idx], tile, zero, m)
            parts, o = [], None
            for i in range(max(len(nxt), len(cur))):
                if i < len(nxt):
                    parts.append(nxt[i]())
                if i < len(cur):
                    pv = cur[i]()
                    o = pv if o is None else o + pv
            finish(kind, pair, tile, o, m)
            col_maxes.append(col_max(parts))
        return tuple(col_maxes[len(items):])

    first = jnp.minimum(pl.program_id(0), 0)
    carry0 = tuple(
        col_max([chunk() for chunk in score_chunks(*items[i], bufs[i], first, first)])
        for i in range(_LOOKAHEAD))
    def trip(i, carry):
        for j in range(_TILES_PER_TRIP):
            carry = tile_body(_TILES_PER_TRIP * i + j, carry)
        return carry

    lax.fori_loop(0, n_tiles // _TILES_PER_TRIP, trip, carry0)


def _mixers(qa, ka, va, ga, qb, kb, vb, gb, sink, bias):
    bsz, n_tiles, _, tq = qa.shape
    seq = ka.shape[1]
    assert tq == 2 * Q_BLOCK == 2 * WINDOW and seq % _TK_A == 0
    assert n_tiles % _TILES_PER_TRIP == 0
    tiled = pl.BlockSpec((1, n_tiles, WIDTH, tq), lambda b: (b, 0, 0, 0))
    k_spec = pl.BlockSpec((1, seq, KV_WIDTH), lambda b: (b, 0, 0))
    const = lambda shape: pl.BlockSpec(shape, lambda b: (0,) * len(shape))
    return pl.pallas_call(
        _mixers_kernel,
        out_shape=jax.ShapeDtypeStruct((bsz, seq, 2 * WIDTH), _BF16),
        grid=(bsz,),
        in_specs=[tiled, k_spec, pl.BlockSpec((1, KV_WIDTH, seq), lambda b: (b, 0, 0)), tiled,
                  tiled, k_spec,
                  pl.BlockSpec((1, seq // Q_BLOCK, KV_WIDTH, Q_BLOCK), lambda b: (b, 0, 0, 0)),
                  tiled,
                  const((_PAIRS, 1, 2 * tq)), const((N_HEADS, _KEYS_B, tq))],
        out_specs=pl.BlockSpec((1, seq, 2 * WIDTH), lambda b: (b, 0, 0)),
        scratch_shapes=([pltpu.VMEM((seq, 2 * tq), _F32)] * 2
                        + [pltpu.VMEM((_KEYS_B, 2 * tq), _F32)] * 2),
        compiler_params=_params(1),
        name="mixers",
    )(qa, ka, va, ga, qb, kb, vb, gb,
      jnp.tile(jnp.broadcast_to(sink[:, None], (N_HEADS, Q_BLOCK)).reshape(_PAIRS, 1, 2 * Q_BLOCK),
               (1, 1, tq // Q_BLOCK)),
      bias)


def _outproj_kernel(a_ref, w_ref, x_ref, gate_ref, gpost_ref, o_ref):
    w = w_ref[...].astype(_BF16)
    starts = np.cumsum((0,) + _TM_OUT_CHUNKS)
    assert starts[-1] == a_ref.shape[1]
    for lo, hi in zip(starts[:-1], starts[1:]):
        rows = slice(int(lo), int(hi))
        y = jnp.dot(a_ref[0, rows, :], w, preferred_element_type=_F32)
        ms = jnp.mean(y * y, axis=-1, keepdims=True)
        yn = y * lax.rsqrt(ms + EPS) * gpost_ref[...]
        o_ref[0, rows, :] = x_ref[0, rows, :] + gate_ref[0, 0] * yn


def _out_proj(o, w_out, x, mod, g_post):
    bsz, seq, d = x.shape
    tm = _TM_OUT
    assert seq % tm == 0 and w_out.shape == (2 * WIDTH, d)
    xo = pl.BlockSpec((1, tm, d), lambda b, i: (b, i, 0))
    windows = 2 * (2 * tm * d * 4 + tm * 2 * WIDTH * 2) + 2 * WIDTH * d * 4
    vmem_limit = windows + 8 * max(_TM_OUT_CHUNKS) * d * 4
    assert vmem_limit <= _VMEM_BYTES_V7X - (4 << 20)
    return pl.pallas_call(
        _outproj_kernel,
        out_shape=jax.ShapeDtypeStruct((bsz, seq, d), _F32),
        grid=(bsz, seq // tm),
        in_specs=[pl.BlockSpec((1, tm, 2 * WIDTH), lambda b, i: (b, i, 0)),
                  pl.BlockSpec((2 * WIDTH, d), lambda b, i: (0, 0)),
                  xo,
                  pl.BlockSpec((1, 1, 1, d), lambda b, i: (b, _GATE, 0, 0)),
                  pl.BlockSpec((1, d), lambda b, i: (0, 0))],
        out_specs=xo,
        compiler_params=_params(2, vmem_limit),
        name="out_proj",
    )(o, w_out, x, mod, g_post.reshape(1, d))


def kernel(x, c, w_ada, b_ada, g_pre, g_post, w_in, qn_a, kn_a, sink_b, w_out, rel_table):
    bsz, seq, d = x.shape
    depth = w_ada.shape[0]
    cos_t, sin_t = _rope_tables(seq)
    bias = _t5_bias_tiles(rel_table)
    for l in range(depth):
        mod = _adaln_mod(c, w_ada[l], b_ada[l]).reshape(bsz, 3, 1, d)
        qa, ka, va, ga, qb, kb, vb, gb = _in_proj(
            x, mod, g_pre[l], w_in[l], cos_t, sin_t, qn_a[l], kn_a[l])
        o = _mixers(qa, ka, va, ga, qb, kb, vb, gb, sink_b[l], bias)
        x = _out_proj(o, w_out[l], x, mod, g_post[l])
    return x
```

```python
import functools

import jax
import jax.numpy as jnp
import numpy as np
from jax import lax
from jax.experimental import pallas as pl
from jax.experimental.pallas import tpu as pltpu

HEAD_DIM = 64
N_HEADS = 8
N_KV = 2
GROUP = N_HEADS // N_KV
WIDTH = N_HEADS * HEAD_DIM
KV_WIDTH = N_KV * HEAD_DIM
IN_COLS = 2 * (2 * WIDTH + 2 * KV_WIDTH)
Q_BLOCK = 128
WINDOW = 128
GRID_W = 64
ROPE_THETA = 10000.0
N_BUCKETS = 32
MAX_DISTANCE = 128
EPS = 1e-6
MASK_VALUE = -1e30
LOG2E = 1.4426950408889634
Q_SCALE = HEAD_DIM ** -0.5 * LOG2E

_QA, _KA, _VA, _GA = 0, WIDTH, WIDTH + KV_WIDTH, WIDTH + 2 * KV_WIDTH
_HALF = 2 * WIDTH + 2 * KV_WIDTH
_QB, _KB, _VB, _GB = _HALF + _QA, _HALF + _KA, _HALF + _VA, _HALF + _GA
_QK_GROUP = ((_QA, WIDTH), (_QB, WIDTH), (_KA, KV_WIDTH), (_KB, KV_WIDTH))
_GATE_GROUP = ((_GA, WIDTH), (_GB, WIDTH))
_V_GROUP = ((_VA, KV_WIDTH), (_VB, KV_WIDTH))

_F32 = jnp.float32
_BF16 = jnp.bfloat16

_VMEM_BYTES_V7X = 64 * 1024 * 1024
_VMEM_LIMIT_BYTES = 48 * 1024 * 1024

_TM_IN = 1024
_TQ_A = 256
_TK_A = 256
_KEYS_B = _TQ_A + 2 * WINDOW
_KEYS_Q = Q_BLOCK + 2 * WINDOW
_KEYS_Q0 = slice(0, _KEYS_Q)
_KEYS_Q1 = slice(Q_BLOCK, Q_BLOCK + _KEYS_Q)
_TM_OUT = 1024
_TM_OUT_CHUNKS = (256, 256, 256, 128, 128)
_TN_MOD = 1536

_TNT = (((0,), (1,)), ((), ()))
_SHIFT, _SCALE, _GATE = 0, 1, 2


def _silu(v):
    return 0.5 * v * (1.0 + jnp.tanh(0.5 * v))


def _params(n_grid, vmem_limit_bytes=_VMEM_LIMIT_BYTES):
    return pltpu.CompilerParams(
        dimension_semantics=("arbitrary",) * n_grid,
        vmem_limit_bytes=vmem_limit_bytes)


def _mod_kernel(c_ref, w_ref, b_ref, o_ref):
    ca = _silu(c_ref[...])
    o_ref[...] = jnp.dot(ca.astype(_BF16), w_ref[...].astype(_BF16),
                         preferred_element_type=_F32) + b_ref[...]


def _adaln_mod(c, w, b):
    bsz, d = c.shape
    n = w.shape[1]
    assert n % _TN_MOD == 0
    return pl.pallas_call(
        _mod_kernel,
        out_shape=jax.ShapeDtypeStruct((bsz, n), _F32),
        grid=(n // _TN_MOD,),
        in_specs=[pl.BlockSpec((bsz, d), lambda j: (0, 0)),
                  pl.BlockSpec((d, _TN_MOD), lambda j: (0, j)),
                  pl.BlockSpec((1, _TN_MOD), lambda j: (0, j))],
        out_specs=pl.BlockSpec((bsz, _TN_MOD), lambda j: (0, j)),
        compiler_params=_params(1),
        name="adaln_mod",
    )(c, w, b.reshape(1, n))


def _bias_kernel(present, table_ref, bucket_ref, o_ref):
    masked = jnp.full((Q_BLOCK, Q_BLOCK), MASK_VALUE, _F32)
    for h in range(N_HEADS):
        table = table_ref[h] * LOG2E
        tiles = []
        for d in range(3):
            bucket = bucket_ref[d]
            acc = masked
            for b in present[d]:
                acc = jnp.where(bucket == b, table[b:b + 1, :], acc)
            tiles.append(acc)
        lo, mid, hi = tiles
        for jb, (left, right) in enumerate(((lo, masked), (mid, lo), (hi, mid), (masked, hi))):
            rows = slice(jb * Q_BLOCK, (jb + 1) * Q_BLOCK)
            o_ref[h, rows, 0:Q_BLOCK] = left
            o_ref[h, rows, Q_BLOCK:2 * Q_BLOCK] = right


def _t5_bucket(rel):
    nb = N_BUCKETS // 2
    max_exact = nb // 2
    ret = (rel > 0).astype(np.int32) * nb
    n = np.abs(rel)
    assert MAX_DISTANCE == 2 * max_exact ** 2 and nb - max_exact == 8
    n2 = n.astype(np.int64) ** 2
    large = max_exact + sum((n2 >= 2 ** k).astype(np.int32) for k in range(7, 31))
    large = np.minimum(large, nb - 1)
    return ret + np.where(n < max_exact, n, large)


def _t5_bias_tiles(rel_table):
    r = np.arange(Q_BLOCK)
    rel = ((np.arange(3) - 1)[:, None, None] * Q_BLOCK
           + r[None, :, None] - r[None, None, :])
    bucket = np.where(np.abs(rel) <= WINDOW, _t5_bucket(rel), -1).astype(np.int32)
    present = tuple(tuple(int(b) for b in np.unique(tile) if b >= 0) for tile in bucket)
    return pl.pallas_call(
        functools.partial(_bias_kernel, present),
        out_shape=jax.ShapeDtypeStruct((N_HEADS, _KEYS_B, _TQ_A), _F32),
        in_specs=[pl.BlockSpec(memory_space=pltpu.VMEM),
                  pl.BlockSpec(memory_space=pltpu.VMEM)],
        out_specs=pl.BlockSpec(memory_space=pltpu.VMEM),
        compiler_params=pltpu.CompilerParams(vmem_limit_bytes=_VMEM_LIMIT_BYTES),
        name="t5_bias",
    )(jnp.broadcast_to(rel_table.T[:, :, None], (N_HEADS, N_BUCKETS, Q_BLOCK)),
      jnp.asarray(bucket))


def _norm_rope(xh, gain, cos, sin):
    ms = jnp.mean(xh * xh, axis=0, keepdims=True)
    y = xh * lax.rsqrt(ms + EPS) * gain
    q = HEAD_DIM // 4
    rot = jnp.concatenate([-y[q:2 * q], y[0:q], -y[3 * q:4 * q], y[2 * q:3 * q]], axis=0)
    return y * cos + rot * sin


def _inproj_kernel(x_ref, shift_ref, scale_ref, gpre_ref, w_ref, cos_ref, sin_ref,
                   qn_ref, kn_ref,
                   qa_ref, ka_ref, va_ref, ga_ref, qb_ref, kb_ref, vb_ref, gb_ref):
    x = x_ref[0]
    ms = jnp.mean(x * x, axis=-1, keepdims=True)
    h = x * lax.rsqrt(ms + EPS) * gpre_ref[...]
    h = h * (1.0 + scale_ref[0, 0]) + shift_ref[0, 0]
    hb = h.astype(_BF16)

    def proj(group):
        w = jnp.concatenate([w_ref[:, lo:lo + n] for lo, n in group], axis=1).astype(_BF16)
        return lax.dot_general(w, hb, _TNT, preferred_element_type=_F32)

    cos = cos_ref[...]
    sin = sin_ref[...]
    qn = qn_ref[...]
    kn = kn_ref[...]

    def store_tiled(ref, rows, val):
        tq = ref.shape[3]
        for j in range(ref.shape[1]):
            ref[0, j, rows, :] = val[:, j * tq:(j + 1) * tq]

    qk = proj(_QK_GROUP)
    for hd in range(N_HEADS):
        rows = slice(hd * HEAD_DIM, (hd + 1) * HEAD_DIM)
        store_tiled(qa_ref, rows, (_norm_rope(qk[rows], qn, cos, sin) * Q_SCALE).astype(_BF16))
    store_tiled(qb_ref, slice(0, WIDTH), (qk[WIDTH:2 * WIDTH] * Q_SCALE).astype(_BF16))
    ka = qk[2 * WIDTH:2 * WIDTH + KV_WIDTH]
    ka = jnp.concatenate(
        [_norm_rope(ka[hd * HEAD_DIM:(hd + 1) * HEAD_DIM], kn, cos, sin) for hd in range(N_KV)],
        axis=0)
    ka_ref[0] = ka.T.astype(_BF16)
    kb_ref[0] = qk[2 * WIDTH + KV_WIDTH:].T.astype(_BF16)

    gates = _silu(proj(_GATE_GROUP))
    store_tiled(ga_ref, slice(0, WIDTH), gates[:WIDTH].astype(_BF16))
    store_tiled(gb_ref, slice(0, WIDTH), gates[WIDTH:].astype(_BF16))

    v = proj(_V_GROUP)
    va_ref[0] = v[:KV_WIDTH].astype(_BF16)
    store_tiled(vb_ref, slice(0, KV_WIDTH), v[KV_WIDTH:].astype(_BF16))


def _rope_tables(seq):
    f32 = np.float32
    rows = seq // GRID_W
    row = np.repeat(np.arange(rows), GRID_W).astype(f32)
    col = np.tile(np.arange(GRID_W), rows).astype(f32)
    half = HEAD_DIM // 2
    freqs = np.power(f32(ROPE_THETA), -np.arange(0, half, 2, dtype=f32) / f32(half)).astype(f32)

    def ang2(pos):
        ang = pos[:, None] * freqs[None, :]
        return np.concatenate([ang, ang], axis=-1)

    ang = np.concatenate([ang2(row), ang2(col)], axis=-1)
    return jnp.asarray(np.cos(ang).T), jnp.asarray(np.sin(ang).T)


def _in_proj(x, mod, g_pre, w_in, cos_t, sin_t, qn, kn):
    bsz, seq, d = x.shape
    tm = _TM_IN
    assert seq % tm == 0 and tm % _TQ_A == 0 and w_in.shape == (d, IN_COLS)
    tok_t = lambda n: pl.BlockSpec((1, n, tm), lambda b, i: (b, 0, i))
    tok_m = lambda n: pl.BlockSpec((1, tm, n), lambda b, i: (b, i, 0))
    mod_row = lambda r: pl.BlockSpec((1, 1, 1, d), lambda b, i: (b, r, 0, 0))
    tab = pl.BlockSpec((HEAD_DIM, tm), lambda b, i: (0, i))
    gain = pl.BlockSpec((HEAD_DIM, tm), lambda b, i: (0, 0))
    shp_t = lambda n: jax.ShapeDtypeStruct((bsz, n, seq), _BF16)
    shp_m = lambda n: jax.ShapeDtypeStruct((bsz, seq, n), _BF16)
    tiled = pl.BlockSpec((1, tm // _TQ_A, WIDTH, _TQ_A), lambda b, i: (b, i, 0, 0))
    shp_tiled = jax.ShapeDtypeStruct((bsz, seq // _TQ_A, WIDTH, _TQ_A), _BF16)
    v_blocks = pl.BlockSpec((1, tm // Q_BLOCK, KV_WIDTH, Q_BLOCK), lambda b, i: (b, i, 0, 0))
    shp_v_blocks = jax.ShapeDtypeStruct((bsz, seq // Q_BLOCK, KV_WIDTH, Q_BLOCK), _BF16)
    return pl.pallas_call(
        _inproj_kernel,
        out_shape=(shp_tiled, shp_m(KV_WIDTH), shp_t(KV_WIDTH), shp_tiled,
                   shp_tiled, shp_m(KV_WIDTH), shp_v_blocks, shp_tiled),
        grid=(bsz, seq // tm),
        in_specs=[pl.BlockSpec((1, tm, d), lambda b, i: (b, i, 0)),
                  mod_row(_SHIFT), mod_row(_SCALE),
                  pl.BlockSpec((1, d), lambda b, i: (0, 0)),
                  pl.BlockSpec((d, IN_COLS), lambda b, i: (0, 0)),
                  tab, tab, gain, gain],
        out_specs=(tiled, tok_m(KV_WIDTH), tok_t(KV_WIDTH), tiled,
                   tiled, tok_m(KV_WIDTH), v_blocks, tiled),
        compiler_params=_params(2),
        name="in_proj",
    )(x, mod, mod, g_pre.reshape(1, d), w_in, cos_t, sin_t,
      jnp.broadcast_to(qn[:, None], (HEAD_DIM, tm)),
      jnp.broadcast_to(kn[:, None], (HEAD_DIM, tm)))


_PAIRS = N_HEADS // 2
_ONES_ROWS = 16
_LOOKAHEAD = 2
_TILES_PER_TRIP = 2


def _head_rows(hd):
    return slice(hd * HEAD_DIM, (hd + 1) * HEAD_DIM)


def _pair_q(head, pair):
    def padded(hd):
        qh = head(hd)
        z = jnp.zeros_like(qh)
        return jnp.concatenate([qh, z] if hd // GROUP == 0 else [z, qh], axis=0)
    return jnp.concatenate([padded(2 * pair), padded(2 * pair + 1)], axis=1)


def _pair_out(o, inv_l, gate, pair, tq):
    o = o * inv_l
    g0 = gate(2 * pair).astype(_F32)
    g1 = gate(2 * pair + 1).astype(_F32)
    return jnp.concatenate([o[:, :tq] * g0, o[:, tq:] * g1], axis=0).T.astype(_BF16)


def _swap_middle(a):
    b = Q_BLOCK
    return jnp.concatenate([a[:, :b], a[:, 2 * b:3 * b], a[:, b:2 * b], a[:, 3 * b:]], axis=1)


def _v_with_ones(v):
    return jnp.concatenate([v, jnp.ones((_ONES_ROWS, v.shape[1]), v.dtype)], axis=0)


def _mixers_kernel(qa_ref, ka_ref, va_ref, ga_ref, qb_ref, kb_ref, vb_ref, gb_ref,
                   qa_nx_ref, ka_nx_ref, qb_nx_ref, kb_nx_ref, sink_ref, bias_ref,
                   o_ref, sa0_ref, sa1_ref, sb0_ref, sb1_ref, m_ref):
    n_tiles, tq = qa_ref.shape[1], qa_ref.shape[3]
    seq = ka_ref.shape[1]
    n_blocks = seq // Q_BLOCK
    a_chunks = [slice(c * _TK_A, (c + 1) * _TK_A) for c in range(seq // _TK_A)]
    items = [(kind, p) for p in range(_PAIRS) for kind in ("a", "b")]
    bufs = (sa0_ref, sb0_ref, sa1_ref, sb1_ref) * (_PAIRS // 2)

    def staged(start, size, zero):
        return pl.ds(pl.multiple_of(zero + start, Q_BLOCK), size)

    def window_blocks(tile):
        first = (tq // Q_BLOCK) * tile - 1
        return [jnp.clip(first + j, 0, n_blocks - 1) for j in range(_KEYS_B // Q_BLOCK)]

    def score_chunks(kind, pair, buf, tile, zero, next_batch=False):
        if next_batch:
            q_ref, k_ref = (qa_nx_ref, ka_nx_ref) if kind == "a" else (qb_nx_ref, kb_nx_ref)
        else:
            q_ref, k_ref = (qa_ref, ka_ref) if kind == "a" else (qb_ref, kb_ref)

        def weights():
            return _pair_q(lambda hd: q_ref[0, tile, _head_rows(hd), :], pair)

        def global_chunk(rows):
            s = jnp.dot(k_ref[0, rows, :], weights(), preferred_element_type=_F32)
            buf[staged(rows.start, _TK_A, zero), :] = s
            return jnp.max(s, axis=0, keepdims=True)

        def window():
            k = jnp.concatenate(
                [k_ref[0, pl.ds(pl.multiple_of(blk * Q_BLOCK, Q_BLOCK), Q_BLOCK), :]
                 for blk in window_blocks(tile)], axis=0)
            w = _swap_middle(weights())

            def bias(keys, queries):
                return jnp.concatenate([bias_ref[2 * pair, keys, queries],
                                        bias_ref[2 * pair + 1, keys, queries]], axis=1)

            left = (jnp.dot(k[_KEYS_Q0], w[:, :tq], preferred_element_type=_F32)
                    + bias(_KEYS_Q0, slice(0, Q_BLOCK)))
            right = (jnp.dot(k[_KEYS_Q1], w[:, tq:], preferred_element_type=_F32)
                     + bias(_KEYS_Q1, slice(Q_BLOCK, 2 * Q_BLOCK)))
            left = jnp.concatenate(
                [jnp.where(tile == 0, MASK_VALUE, left[:Q_BLOCK]), left[Q_BLOCK:]], axis=0)
            right = jnp.concatenate(
                [right[:-Q_BLOCK], jnp.where(tile == n_tiles - 1, MASK_VALUE, right[-Q_BLOCK:])],
                axis=0)
            buf[staged(_KEYS_Q0.start, _KEYS_Q, zero), :tq] = left
            buf[staged(_KEYS_Q1.start, _KEYS_Q, zero), tq:] = right
            m = jnp.concatenate([jnp.max(left, axis=0, keepdims=True),
                                 jnp.max(right, axis=0, keepdims=True)], axis=1)
            return jnp.maximum(m, sink_ref[pair] * LOG2E)

        if kind == "a":
            return [functools.partial(global_chunk, rows) for rows in a_chunks]
        return [window]

    def pv_chunks(kind, pair, buf, tile, zero, m):
        kv = _head_rows(pair // (GROUP // 2))

        def global_chunk(rows):
            e = jnp.exp2(buf[staged(rows.start, _TK_A, zero), :] - m)
            return jnp.dot(_v_with_ones(va_ref[0, kv, rows]), e.astype(_BF16),
                           preferred_element_type=_F32)

        def window():
            e_left = jnp.exp2(buf[staged(_KEYS_Q0.start, _KEYS_Q, zero), :tq] - m[:, :tq])
            e_right = jnp.exp2(buf[staged(_KEYS_Q1.start, _KEYS_Q, zero), tq:] - m[:, tq:])
            v = jnp.concatenate([vb_ref[0, blk] for blk in window_blocks(tile)], axis=1)
            v = _v_with_ones(v[kv, :])
            return jnp.concatenate(
                [jnp.dot(v[:, _KEYS_Q0], e_left.astype(_BF16), preferred_element_type=_F32),
                 jnp.dot(v[:, _KEYS_Q1], e_right.astype(_BF16), preferred_element_type=_F32)],
                axis=1)

        if kind == "a":
            return [functools.partial(global_chunk, rows) for rows in a_chunks]
        return [window]

    def finish(kind, pair, tile, o, m):
        l = o[HEAD_DIM:HEAD_DIM + 1]
        if kind == "b":
            l = _swap_middle(l + jnp.exp2(sink_ref[pair] * LOG2E - m))
            o = _swap_middle(o)
        g_ref = ga_ref if kind == "a" else gb_ref
        lane0 = (0 if kind == "a" else WIDTH) + pair * 2 * HEAD_DIM
        o_ref[0, pl.ds(pl.multiple_of(tile * tq, tq), tq), lane0:lane0 + 2 * HEAD_DIM] = _pair_out(
            o[:HEAD_DIM], 1.0 / l, lambda hd: g_ref[0, tile, _head_rows(hd), :], pair, tq)

    def col_max(parts):
        return functools.reduce(jnp.maximum, parts)

    def tile_body(tile, carry, last_tile=False):
        zero = jnp.minimum(tile, 0)
        col_maxes = list(carry)
        for idx, (kind, pair) in enumerate(items):
            ahead = (idx + _LOOKAHEAD) % len(items)
            if idx + _LOOKAHEAD < len(items):
                nxt = score_chunks(*items[ahead], bufs[ahead], tile, zero)
            elif last_tile:
                nxt = score_chunks(*items[ahead], bufs[ahead], 0, zero, next_batch=True)
            else:
                nxt = score_chunks(*items[ahead], bufs[ahead], tile + 1, zero)
            m = col_maxes[idx]
            cur = pv_chunks(kind, pair, bufs[idx], tile, zero, m)
            parts, o = [], None
            for i in range(max(len(nxt), len(cur))):
                if i < len(nxt):
                    parts.append(nxt[i]())
                if i < len(cur):
                    pv = cur[i]()
                    o = pv if o is None else o + pv
            finish(kind, pair, tile, o, m)
            col_maxes.append(col_max(parts))
        return tuple(col_maxes[len(items):])

    first = jnp.minimum(pl.program_id(0), 0)

    @pl.when(pl.program_id(0) == 0)
    def _():
        for i in range(_LOOKAHEAD):
            m_ref[i] = col_max(
                [chunk() for chunk in score_chunks(*items[i], bufs[i], first, first)])

    def trip(i, carry, last_trip=False):
        for j in range(_TILES_PER_TRIP):
            carry = tile_body(_TILES_PER_TRIP * i + j, carry,
                              last_tile=last_trip and j == _TILES_PER_TRIP - 1)
        return carry

    n_trips = n_tiles // _TILES_PER_TRIP
    carry = lax.fori_loop(0, n_trips - 1, trip, tuple(m_ref[i] for i in range(_LOOKAHEAD)))
    carry = trip(first + (n_trips - 1), carry, last_trip=True)
    for i in range(_LOOKAHEAD):
        m_ref[i] = carry[i]


def _mixers(qa, ka, va, ga, qb, kb, vb, gb, sink, bias):
    bsz, n_tiles, _, tq = qa.shape
    seq = ka.shape[1]
    assert tq == 2 * Q_BLOCK == 2 * WINDOW and seq % _TK_A == 0
    assert n_tiles % _TILES_PER_TRIP == 0
    tiled = pl.BlockSpec((1, n_tiles, WIDTH, tq), lambda b: (b, 0, 0, 0))
    k_spec = pl.BlockSpec((1, seq, KV_WIDTH), lambda b: (b, 0, 0))
    const = lambda shape: pl.BlockSpec(shape, lambda b: (0,) * len(shape))
    nxt = lambda b: jnp.minimum(b + 1, bsz - 1)
    q_next = pl.BlockSpec((1, 1, WIDTH, tq), lambda b: (nxt(b), 0, 0, 0))
    return pl.pallas_call(
        _mixers_kernel,
        out_shape=jax.ShapeDtypeStruct((bsz, seq, 2 * WIDTH), _BF16),
        grid=(bsz,),
        in_specs=[tiled, k_spec, pl.BlockSpec((1, KV_WIDTH, seq), lambda b: (b, 0, 0)), tiled,
                  tiled, k_spec,
                  pl.BlockSpec((1, seq // Q_BLOCK, KV_WIDTH, Q_BLOCK), lambda b: (b, 0, 0, 0)),
                  tiled,
                  q_next, pl.BlockSpec((1, seq, KV_WIDTH), lambda b: (nxt(b), 0, 0)),
                  q_next, pl.BlockSpec((1, _KEYS_B, KV_WIDTH), lambda b: (nxt(b), 0, 0)),
                  const((_PAIRS, 1, 2 * tq)), const((N_HEADS, _KEYS_B, tq))],
        out_specs=pl.BlockSpec((1, seq, 2 * WIDTH), lambda b: (b, 0, 0)),
        scratch_shapes=([pltpu.VMEM((seq, 2 * tq), _F32)] * 2
                        + [pltpu.VMEM((_KEYS_B, 2 * tq), _F32)] * 2
                        + [pltpu.VMEM((_LOOKAHEAD, 1, 2 * tq), _F32)]),
        compiler_params=_params(1),
        name="mixers",
    )(qa, ka, va, ga, qb, kb, vb, gb, qa, ka, qb, kb,
      jnp.tile(jnp.broadcast_to(sink[:, None], (N_HEADS, Q_BLOCK)).reshape(_PAIRS, 1, 2 * Q_BLOCK),
               (1, 1, tq // Q_BLOCK)),
      bias)


def _outproj_kernel(a_ref, w_ref, x_ref, gate_ref, gpost_ref, o_ref):
    w = w_ref[...].astype(_BF16)
    starts = np.cumsum((0,) + _TM_OUT_CHUNKS)
    assert starts[-1] == a_ref.shape[1]
    for lo, hi in zip(starts[:-1], starts[1:]):
        rows = slice(int(lo), int(hi))
        y = jnp.dot(a_ref[0, rows, :], w, preferred_element_type=_F32)
        ms = jnp.mean(y * y, axis=-1, keepdims=True)
        yn = y * lax.rsqrt(ms + EPS) * gpost_ref[...]
        o_ref[0, rows, :] = x_ref[0, rows, :] + gate_ref[0, 0] * yn


def _out_proj(o, w_out, x, mod, g_post):
    bsz, seq, d = x.shape
    tm = _TM_OUT
    assert seq % tm == 0 and w_out.shape == (2 * WIDTH, d)
    xo = pl.BlockSpec((1, tm, d), lambda b, i: (b, i, 0))
    windows = 2 * (2 * tm * d * 4 + tm * 2 * WIDTH * 2) + 2 * WIDTH * d * 4
    vmem_limit = windows + 8 * max(_TM_OUT_CHUNKS) * d * 4
    assert vmem_limit <= _VMEM_BYTES_V7X - (4 << 20)
    return pl.pallas_call(
        _outproj_kernel,
        out_shape=jax.ShapeDtypeStruct((bsz, seq, d), _F32),
        grid=(bsz, seq // tm),
        in_specs=[pl.BlockSpec((1, tm, 2 * WIDTH), lambda b, i: (b, i, 0)),
                  pl.BlockSpec((2 * WIDTH, d), lambda b, i: (0, 0)),
                  xo,
                  pl.BlockSpec((1, 1, 1, d), lambda b, i: (b, _GATE, 0, 0)),
                  pl.BlockSpec((1, d), lambda b, i: (0, 0))],
        out_specs=xo,
        compiler_params=_params(2, vmem_limit),
        name="out_proj",
    )(o, w_out, x, mod, g_post.reshape(1, d))


def kernel(x, c, w_ada, b_ada, g_pre, g_post, w_in, qn_a, kn_a, sink_b, w_out, rel_table):
    bsz, seq, d = x.shape
    depth = w_ada.shape[0]
    cos_t, sin_t = _rope_tables(seq)
    bias = _t5_bias_tiles(rel_table)
    for l in range(depth):
        mod = _adaln_mod(c, w_ada[l], b_ada[l]).reshape(bsz, 3, 1, d)
        qa, ka, va, ga, qb, kb, vb, gb = _in_proj(
            x, mod, g_pre[l], w_in[l], cos_t, sin_t, qn_a[l], kn_a[l])
        o = _mixers(qa, ka, va, ga, qb, kb, vb, gb, sink_b[l], bias)
        x = _out_proj(o, w_out[l], x, mod, g_post[l])
    return x
```

```python
import functools

import jax
import jax.numpy as jnp
import numpy as np
from jax import lax
from jax.experimental import pallas as pl
from jax.experimental.pallas import tpu as pltpu

HEAD_DIM = 64
N_HEADS = 8
N_KV = 2
GROUP = N_HEADS // N_KV
WIDTH = N_HEADS * HEAD_DIM
KV_WIDTH = N_KV * HEAD_DIM
IN_COLS = 2 * (2 * WIDTH + 2 * KV_WIDTH)
Q_BLOCK = 128
WINDOW = 128
GRID_W = 64
ROPE_THETA = 10000.0
N_BUCKETS = 32
MAX_DISTANCE = 128
EPS = 1e-6
MASK_VALUE = -1e30
LOG2E = 1.4426950408889634
Q_SCALE = HEAD_DIM ** -0.5 * LOG2E

_QA, _KA, _VA, _GA = 0, WIDTH, WIDTH + KV_WIDTH, WIDTH + 2 * KV_WIDTH
_HALF = 2 * WIDTH + 2 * KV_WIDTH
_QB, _KB, _VB, _GB = _HALF + _QA, _HALF + _KA, _HALF + _VA, _HALF + _GA
_QK_GROUP = ((_QA, WIDTH), (_QB, WIDTH), (_KA, KV_WIDTH), (_KB, KV_WIDTH))
_GATE_GROUP = ((_GA, WIDTH), (_GB, WIDTH))
_V_GROUP = ((_VA, KV_WIDTH), (_VB, KV_WIDTH))

_F32 = jnp.float32
_BF16 = jnp.bfloat16

_VMEM_BYTES_V7X = 64 * 1024 * 1024
_VMEM_LIMIT_BYTES = 48 * 1024 * 1024

_TM_IN = 1024
_TQ_A = 256
_TK_A = 256
_KEYS_B = _TQ_A + 2 * WINDOW
_KEYS_Q = Q_BLOCK + 2 * WINDOW
_KEYS_Q0 = slice(0, _KEYS_Q)
_KEYS_Q1 = slice(Q_BLOCK, Q_BLOCK + _KEYS_Q)
_TM_OUT = 1024
_TM_OUT_CHUNKS = (256, 256, 256, 128, 128)
_TN_MOD = 1536

_TNT = (((0,), (1,)), ((), ()))
_SHIFT, _SCALE, _GATE = 0, 1, 2


def _silu(v):
    return 0.5 * v * (1.0 + jnp.tanh(0.5 * v))


def _params(n_grid, vmem_limit_bytes=_VMEM_LIMIT_BYTES):
    return pltpu.CompilerParams(
        dimension_semantics=("arbitrary",) * n_grid,
        vmem_limit_bytes=vmem_limit_bytes)


def _mod_kernel(c_ref, w_ref, b_ref, o_ref):
    ca = _silu(c_ref[...])
    o_ref[...] = jnp.dot(ca.astype(_BF16), w_ref[...].astype(_BF16),
                         preferred_element_type=_F32) + b_ref[...]


def _adaln_mod(c, w, b):
    bsz, d = c.shape
    n = w.shape[1]
    assert n % _TN_MOD == 0
    return pl.pallas_call(
        _mod_kernel,
        out_shape=jax.ShapeDtypeStruct((bsz, n), _F32),
        grid=(n // _TN_MOD,),
        in_specs=[pl.BlockSpec((bsz, d), lambda j: (0, 0)),
                  pl.BlockSpec((d, _TN_MOD), lambda j: (0, j)),
                  pl.BlockSpec((1, _TN_MOD), lambda j: (0, j))],
        out_specs=pl.BlockSpec((bsz, _TN_MOD), lambda j: (0, j)),
        compiler_params=_params(1),
        name="adaln_mod",
    )(c, w, b.reshape(1, n))


def _bias_kernel(present, table_ref, bucket_ref, o_ref):
    masked = jnp.full((Q_BLOCK, Q_BLOCK), MASK_VALUE, _F32)
    for h in range(N_HEADS):
        table = table_ref[h] * LOG2E
        tiles = []
        for d in range(3):
            bucket = bucket_ref[d]
            acc = masked
            for b in present[d]:
                acc = jnp.where(bucket == b, table[b:b + 1, :], acc)
            tiles.append(acc)
        lo, mid, hi = tiles
        for jb, (left, right) in enumerate(((lo, masked), (mid, lo), (hi, mid), (masked, hi))):
            rows = slice(jb * Q_BLOCK, (jb + 1) * Q_BLOCK)
            o_ref[h, rows, 0:Q_BLOCK] = left
            o_ref[h, rows, Q_BLOCK:2 * Q_BLOCK] = right


def _t5_bucket(rel):
    nb = N_BUCKETS // 2
    max_exact = nb // 2
    ret = (rel > 0).astype(np.int32) * nb
    n = np.abs(rel)
    assert MAX_DISTANCE == 2 * max_exact ** 2 and nb - max_exact == 8
    n2 = n.astype(np.int64) ** 2
    large = max_exact + sum((n2 >= 2 ** k).astype(np.int32) for k in range(7, 31))
    large = np.minimum(large, nb - 1)
    return ret + np.where(n < max_exact, n, large)


def _t5_bias_tiles(rel_table):
    r = np.arange(Q_BLOCK)
    rel = ((np.arange(3) - 1)[:, None, None] * Q_BLOCK
           + r[None, :, None] - r[None, None, :])
    bucket = np.where(np.abs(rel) <= WINDOW, _t5_bucket(rel), -1).astype(np.int32)
    present = tuple(tuple(int(b) for b in np.unique(tile) if b >= 0) for tile in bucket)
    return pl.pallas_call(
        functools.partial(_bias_kernel, present),
        out_shape=jax.ShapeDtypeStruct((N_HEADS, _KEYS_B, _TQ_A), _F32),
        in_specs=[pl.BlockSpec(memory_space=pltpu.VMEM),
                  pl.BlockSpec(memory_space=pltpu.VMEM)],
        out_specs=pl.BlockSpec(memory_space=pltpu.VMEM),
        compiler_params=pltpu.CompilerParams(vmem_limit_bytes=_VMEM_LIMIT_BYTES),
        name="t5_bias",
    )(jnp.broadcast_to(rel_table.T[:, :, None], (N_HEADS, N_BUCKETS, Q_BLOCK)),
      jnp.asarray(bucket))


def _norm_rope(xh, gain, cos, sin):
    ms = jnp.mean(xh * xh, axis=0, keepdims=True)
    y = xh * lax.rsqrt(ms + EPS) * gain
    q = HEAD_DIM // 4
    rot = jnp.concatenate([-y[q:2 * q], y[0:q], -y[3 * q:4 * q], y[2 * q:3 * q]], axis=0)
    return y * cos + rot * sin


def _inproj_kernel(x_ref, shift_ref, scale_ref, gpre_ref, w_ref, cos_ref, sin_ref,
                   qn_ref, kn_ref,
                   qa_ref, ka_ref, va_ref, ga_ref, qb_ref, kb_ref, vb_ref, gb_ref):
    x = x_ref[0]
    ms = jnp.mean(x * x, axis=-1, keepdims=True)
    h = x * lax.rsqrt(ms + EPS) * gpre_ref[...]
    h = h * (1.0 + scale_ref[0, 0]) + shift_ref[0, 0]
    hb = h.astype(_BF16)

    def proj(group):
        w = jnp.concatenate([w_ref[:, lo:lo + n] for lo, n in group], axis=1).astype(_BF16)
        return lax.dot_general(w, hb, _TNT, preferred_element_type=_F32)

    cos = cos_ref[...]
    sin = sin_ref[...]
    qn = qn_ref[...]
    kn = kn_ref[...]

    def store_tiled(ref, rows, val):
        tq = ref.shape[3]
        for j in range(ref.shape[1]):
            ref[0, j, rows, :] = val[:, j * tq:(j + 1) * tq]

    qk = proj(_QK_GROUP)
    for hd in range(N_HEADS):
        rows = slice(hd * HEAD_DIM, (hd + 1) * HEAD_DIM)
        store_tiled(qa_ref, rows, (_norm_rope(qk[rows], qn, cos, sin) * Q_SCALE).astype(_BF16))
    store_tiled(qb_ref, slice(0, WIDTH), (qk[WIDTH:2 * WIDTH] * Q_SCALE).astype(_BF16))
    ka = qk[2 * WIDTH:2 * WIDTH + KV_WIDTH]
    ka = jnp.concatenate(
        [_norm_rope(ka[hd * HEAD_DIM:(hd + 1) * HEAD_DIM], kn, cos, sin) for hd in range(N_KV)],
        axis=0)
    ka_ref[0] = ka.T.astype(_BF16)
    kb_ref[0] = qk[2 * WIDTH + KV_WIDTH:].T.astype(_BF16)

    gates = _silu(proj(_GATE_GROUP))
    store_tiled(ga_ref, slice(0, WIDTH), gates[:WIDTH].astype(_BF16))
    store_tiled(gb_ref, slice(0, WIDTH), gates[WIDTH:].astype(_BF16))

    v = proj(_V_GROUP)
    va_ref[0] = v[:KV_WIDTH].astype(_BF16)
    store_tiled(vb_ref, slice(0, KV_WIDTH), v[KV_WIDTH:].astype(_BF16))


def _rope_tables(seq):
    f32 = np.float32
    rows = seq // GRID_W
    row = np.repeat(np.arange(rows), GRID_W).astype(f32)
    col = np.tile(np.arange(GRID_W), rows).astype(f32)
    half = HEAD_DIM // 2
    freqs = np.power(f32(ROPE_THETA), -np.arange(0, half, 2, dtype=f32) / f32(half)).astype(f32)

    def ang2(pos):
        ang = pos[:, None] * freqs[None, :]
        return np.concatenate([ang, ang], axis=-1)

    ang = np.concatenate([ang2(row), ang2(col)], axis=-1)
    return jnp.asarray(np.cos(ang).T), jnp.asarray(np.sin(ang).T)


def _in_proj(x, mod, g_pre, w_in, cos_t, sin_t, qn, kn):
    bsz, seq, d = x.shape
    tm = _TM_IN
    assert seq % tm == 0 and tm % _TQ_A == 0 and w_in.shape == (d, IN_COLS)
    tok_t = lambda n: pl.BlockSpec((1, n, tm), lambda b, i: (b, 0, i))
    tok_m = lambda n: pl.BlockSpec((1, tm, n), lambda b, i: (b, i, 0))
    mod_row = lambda r: pl.BlockSpec((1, 1, 1, d), lambda b, i: (b, r, 0, 0))
    tab = pl.BlockSpec((HEAD_DIM, tm), lambda b, i: (0, i))
    gain = pl.BlockSpec((HEAD_DIM, tm), lambda b, i: (0, 0))
    shp_t = lambda n: jax.ShapeDtypeStruct((bsz, n, seq), _BF16)
    shp_m = lambda n: jax.ShapeDtypeStruct((bsz, seq, n), _BF16)
    tiled = pl.BlockSpec((1, tm // _TQ_A, WIDTH, _TQ_A), lambda b, i: (b, i, 0, 0))
    shp_tiled = jax.ShapeDtypeStruct((bsz, seq // _TQ_A, WIDTH, _TQ_A), _BF16)
    v_blocks = pl.BlockSpec((1, tm // Q_BLOCK, KV_WIDTH, Q_BLOCK), lambda b, i: (b, i, 0, 0))
    shp_v_blocks = jax.ShapeDtypeStruct((bsz, seq // Q_BLOCK, KV_WIDTH, Q_BLOCK), _BF16)
    return pl.pallas_call(
        _inproj_kernel,
        out_shape=(shp_tiled, shp_m(KV_WIDTH), shp_t(KV_WIDTH), shp_tiled,
                   shp_tiled, shp_m(KV_WIDTH), shp_v_blocks, shp_tiled),
        grid=(bsz, seq // tm),
        in_specs=[pl.BlockSpec((1, tm, d), lambda b, i: (b, i, 0)),
                  mod_row(_SHIFT), mod_row(_SCALE),
                  pl.BlockSpec((1, d), lambda b, i: (0, 0)),
                  pl.BlockSpec((d, IN_COLS), lambda b, i: (0, 0)),
                  tab, tab, gain, gain],
        out_specs=(tiled, tok_m(KV_WIDTH), tok_t(KV_WIDTH), tiled,
                   tiled, tok_m(KV_WIDTH), v_blocks, tiled),
        compiler_params=_params(2),
        name="in_proj",
    )(x, mod, mod, g_pre.reshape(1, d), w_in, cos_t, sin_t,
      jnp.broadcast_to(qn[:, None], (HEAD_DIM, tm)),
      jnp.broadcast_to(kn[:, None], (HEAD_DIM, tm)))


_PAIRS = N_HEADS // 2
_ONES_ROWS = 16
_LOOKAHEAD = 2
_TILES_PER_TRIP = 2


def _head_rows(hd):
    return slice(hd * HEAD_DIM, (hd + 1) * HEAD_DIM)


def _pair_q(head, pair):
    def padded(hd):
        qh = head(hd)
        z = jnp.zeros_like(qh)
        return jnp.concatenate([qh, z] if hd // GROUP == 0 else [z, qh], axis=0)
    return jnp.concatenate([padded(2 * pair), padded(2 * pair + 1)], axis=1)


def _pair_out(o, inv_l, gate, pair, tq):
    o = o * inv_l
    g0 = gate(2 * pair).astype(_F32)
    g1 = gate(2 * pair + 1).astype(_F32)
    return jnp.concatenate([o[:, :tq] * g0, o[:, tq:] * g1], axis=0).T.astype(_BF16)


def _swap_middle(a):
    b = Q_BLOCK
    return jnp.concatenate([a[:, :b], a[:, 2 * b:3 * b], a[:, b:2 * b], a[:, 3 * b:]], axis=1)


def _v_with_ones(v):
    return jnp.concatenate([v, jnp.ones((_ONES_ROWS, v.shape[1]), v.dtype)], axis=0)


def _mixers_kernel(qa_ref, ka_ref, va_ref, ga_ref, qb_ref, kb_ref, vb_ref, gb_ref,
                   sink_ref, bias_ref, o_ref, sa0_ref, sa1_ref, sb0_ref, sb1_ref):
    n_tiles, tq = qa_ref.shape[1], qa_ref.shape[3]
    seq = ka_ref.shape[1]
    n_blocks = seq // Q_BLOCK
    a_chunks = [slice(c * _TK_A, (c + 1) * _TK_A) for c in range(seq // _TK_A)]
    items = [(kind, p) for p in range(_PAIRS) for kind in ("a", "b")]
    bufs = (sa0_ref, sb0_ref, sa1_ref, sb1_ref) * (_PAIRS // 2)

    def staged(start, size, zero):
        return pl.ds(pl.multiple_of(zero + start, Q_BLOCK), size)

    def window_blocks(tile):
        first = (tq // Q_BLOCK) * tile - 1
        return [jnp.clip(first + j, 0, n_blocks - 1) for j in range(_KEYS_B // Q_BLOCK)]

    def score_chunks(kind, pair, buf, tile, zero):
        q_ref = qa_ref if kind == "a" else qb_ref

        def weights():
            return _pair_q(lambda hd: q_ref[0, tile, _head_rows(hd), :], pair)

        def global_chunk(rows):
            s = jnp.dot(ka_ref[0, rows, :], weights(), preferred_element_type=_F32)
            buf[staged(rows.start, _TK_A, zero), :] = s
            return jnp.max(s, axis=0, keepdims=True)

        def window():
            k = jnp.concatenate(
                [kb_ref[0, pl.ds(pl.multiple_of(blk * Q_BLOCK, Q_BLOCK), Q_BLOCK), :]
                 for blk in window_blocks(tile)], axis=0)
            w = _swap_middle(weights())

            def bias(keys, queries):
                return jnp.concatenate([bias_ref[2 * pair, keys, queries],
                                        bias_ref[2 * pair + 1, keys, queries]], axis=1)

            left = (jnp.dot(k[_KEYS_Q0], w[:, :tq], preferred_element_type=_F32)
                    + bias(_KEYS_Q0, slice(0, Q_BLOCK)))
            right = (jnp.dot(k[_KEYS_Q1], w[:, tq:], preferred_element_type=_F32)
                     + bias(_KEYS_Q1, slice(Q_BLOCK, 2 * Q_BLOCK)))
            left = jnp.concatenate(
                [jnp.where(tile == 0, MASK_VALUE, left[:Q_BLOCK]), left[Q_BLOCK:]], axis=0)
            right = jnp.concatenate(
                [right[:-Q_BLOCK], jnp.where(tile == n_tiles - 1, MASK_VALUE, right[-Q_BLOCK:])],
                axis=0)
            buf[staged(_KEYS_Q0.start, _KEYS_Q, zero), :tq] = left
            buf[staged(_KEYS_Q1.start, _KEYS_Q, zero), tq:] = right
            m = jnp.concatenate([jnp.max(left, axis=0, keepdims=True),
                                 jnp.max(right, axis=0, keepdims=True)], axis=1)
            return jnp.maximum(m, sink_ref[pair] * LOG2E)

        if kind == "a":
            return [functools.partial(global_chunk, rows) for rows in a_chunks]
        return [window]

    def pv_chunks(kind, pair, buf, tile, zero, m):
        kv = _head_rows(pair // (GROUP // 2))

        def global_chunk(rows):
            e = jnp.exp2(buf[staged(rows.start, _TK_A, zero), :] - m)
            return jnp.dot(_v_with_ones(va_ref[0, kv, rows]), e.astype(_BF16),
                           preferred_element_type=_F32)

        def window():
            e_left = jnp.exp2(buf[staged(_KEYS_Q0.start, _KEYS_Q, zero), :tq] - m[:, :tq])
            e_right = jnp.exp2(buf[staged(_KEYS_Q1.start, _KEYS_Q, zero), tq:] - m[:, tq:])
            v = jnp.concatenate([vb_ref[0, blk] for blk in window_blocks(tile)], axis=1)
            v = _v_with_ones(v[kv, :])
            return jnp.concatenate(
                [jnp.dot(v[:, _KEYS_Q0], e_left.astype(_BF16), preferred_element_type=_F32),
                 jnp.dot(v[:, _KEYS_Q1], e_right.astype(_BF16), preferred_element_type=_F32)],
                axis=1)

        if kind == "a":
            return [functools.partial(global_chunk, rows) for rows in a_chunks]
        return [window]

    def finish(kind, pair, tile, o, m):
        l = o[HEAD_DIM:HEAD_DIM + 1]
        if kind == "b":
            l = _swap_middle(l + jnp.exp2(sink_ref[pair] * LOG2E - m))
            o = _swap_middle(o)
        g_ref = ga_ref if kind == "a" else gb_ref
        lane0 = (0 if kind == "a" else WIDTH) + pair * 2 * HEAD_DIM
        o_ref[0, pl.ds(pl.multiple_of(tile * tq, tq), tq), lane0:lane0 + 2 * HEAD_DIM] = _pair_out(
            o[:HEAD_DIM], 1.0 / l, lambda hd: g_ref[0, tile, _head_rows(hd), :], pair, tq)

    def col_max(parts):
        return functools.reduce(jnp.maximum, parts)

    def tile_body(tile, carry):
        zero = jnp.minimum(tile, 0)
        next_tile = jnp.minimum(tile + 1, n_tiles - 1)
        col_maxes = list(carry)
        for idx, (kind, pair) in enumerate(items):
            ahead = (idx + _LOOKAHEAD) % len(items)
            ahead_tile = tile if idx + _LOOKAHEAD < len(items) else next_tile
            nxt = score_chunks(*items[ahead], bufs[ahead], ahead_tile, zero)
            m = col_maxes[idx]
            cur = pv_chunks(kind, pair, bufs[idx], tile, zero, m)
            parts, o = [], None
            for i in range(max(len(nxt), len(cur))):
                if i < len(nxt):
                    parts.append(nxt[i]())
                if i < len(cur):
                    pv = cur[i]()
                    o = pv if o is None else o + pv
            finish(kind, pair, tile, o, m)
            col_maxes.append(col_max(parts))
        return tuple(col_maxes[len(items):])

    first = jnp.minimum(pl.program_id(0), 0)
    carry0 = tuple(
        col_max([chunk() for chunk in score_chunks(*items[i], bufs[i], first, first)])
        for i in range(_LOOKAHEAD))
    def trip(i, carry):
        for j in range(_TILES_PER_TRIP):
            carry = tile_body(_TILES_PER_TRIP * i + j, carry)
        return carry

    lax.fori_loop(0, n_tiles // _TILES_PER_TRIP, trip, carry0)


def _mixers(qa, ka, va, ga, qb, kb, vb, gb, sink, bias):
    bsz, n_tiles, _, tq = qa.shape
    seq = ka.shape[1]
    assert tq == 2 * Q_BLOCK == 2 * WINDOW and seq % _TK_A == 0
    assert n_tiles % _TILES_PER_TRIP == 0
    tiled = pl.BlockSpec((1, n_tiles, WIDTH, tq), lambda b: (b, 0, 0, 0))
    k_spec = pl.BlockSpec((1, seq, KV_WIDTH), lambda b: (b, 0, 0))
    const = lambda shape: pl.BlockSpec(shape, lambda b: (0,) * len(shape))
    return pl.pallas_call(
        _mixers_kernel,
        out_shape=jax.ShapeDtypeStruct((bsz, seq, 2 * WIDTH), _BF16),
        grid=(bsz,),
        in_specs=[tiled, k_spec, pl.BlockSpec((1, KV_WIDTH, seq), lambda b: (b, 0, 0)), tiled,
                  tiled, k_spec,
                  pl.BlockSpec((1, seq // Q_BLOCK, KV_WIDTH, Q_BLOCK), lambda b: (b, 0, 0, 0)),
                  tiled,
                  const((_PAIRS, 1, 2 * tq)), const((N_HEADS, _KEYS_B, tq))],
        out_specs=pl.BlockSpec((1, seq, 2 * WIDTH), lambda b: (b, 0, 0)),
        scratch_shapes=([pltpu.VMEM((seq, 2 * tq), _F32)] * 2
                        + [pltpu.VMEM((_KEYS_B, 2 * tq), _F32)] * 2),
        compiler_params=_params(1),
        name="mixers",
    )(qa, ka, va, ga, qb, kb, vb, gb,
      jnp.tile(jnp.broadcast_to(sink[:, None], (N_HEADS, Q_BLOCK)).reshape(_PAIRS, 1, 2 * Q_BLOCK),
               (1, 1, tq // Q_BLOCK)),
      bias)


def _outproj_kernel(a_ref, w_ref, x_ref, gate_ref, gpost_ref, o_ref):
    w = w_ref[...].astype(_BF16)
    starts = np.cumsum((0,) + _TM_OUT_CHUNKS)
    assert starts[-1] == a_ref.shape[1]
    for lo, hi in zip(starts[:-1], starts[1:]):
        rows = slice(int(lo), int(hi))
        y = jnp.dot(a_ref[0, rows, :], w, preferred_element_type=_F32)
        ms = jnp.mean(y * y, axis=-1, keepdims=True)
        yn = y * lax.rsqrt(ms + EPS) * gpost_ref[...]
        o_ref[0, rows, :] = x_ref[0, rows, :] + gate_ref[0, 0] * yn


def _out_proj(o, w_out, x, mod, g_post):
    bsz, seq, d = x.shape
    tm = _TM_OUT
    assert seq % tm == 0 and w_out.shape == (2 * WIDTH, d)
    xo = pl.BlockSpec((1, tm, d), lambda b, i: (b, i, 0))
    windows = 2 * (2 * tm * d * 4 + tm * 2 * WIDTH * 2) + 2 * WIDTH * d * 4
    vmem_limit = windows + 8 * max(_TM_OUT_CHUNKS) * d * 4
    assert vmem_limit <= _VMEM_BYTES_V7X - (4 << 20)
    return pl.pallas_call(
        _outproj_kernel,
        out_shape=jax.ShapeDtypeStruct((bsz, seq, d), _F32),
        grid=(bsz, seq // tm),
        in_specs=[pl.BlockSpec((1, tm, 2 * WIDTH), lambda b, i: (b, i, 0)),
                  pl.BlockSpec((2 * WIDTH, d), lambda b, i: (0, 0)),
                  xo,
                  pl.BlockSpec((1, 1, 1, d), lambda b, i: (b, _GATE, 0, 0)),
                  pl.BlockSpec((1, d), lambda b, i: (0, 0))],
        out_specs=xo,
        compiler_params=_params(2, vmem_limit),
        name="out_proj",
    )(o, w_out, x, mod, g_post.reshape(1, d))


def kernel(x, c, w_ada, b_ada, g_pre, g_post, w_in, qn_a, kn_a, sink_b, w_out, rel_table):
    bsz, seq, d = x.shape
    depth = w_ada.shape[0]
    cos_t, sin_t = _rope_tables(seq)
    bias = _t5_bias_tiles(rel_table)
    for l in range(depth):
        mod = _adaln_mod(c, w_ada[l], b_ada[l]).reshape(bsz, 3, 1, d)
        qa, ka, va, ga, qb, kb, vb, gb = _in_proj(
            x, mod, g_pre[l], w_in[l], cos_t, sin_t, qn_a[l], kn_a[l])
        o = _mixers(qa, ka, va, ga, qb, kb, vb, gb, sink_b[l], bias)
        x = _out_proj(o, w_out[l], x, mod, g_post[l])
    return x
```

```python
import functools

import jax
import jax.numpy as jnp
import numpy as np
from jax import lax
from jax.experimental import pallas as pl
from jax.experimental.pallas import tpu as pltpu

HEAD_DIM = 64
N_HEADS = 8
N_KV = 2
GROUP = N_HEADS // N_KV
WIDTH = N_HEADS * HEAD_DIM
KV_WIDTH = N_KV * HEAD_DIM
IN_COLS = 2 * (2 * WIDTH + 2 * KV_WIDTH)
Q_BLOCK = 128
WINDOW = 128
GRID_W = 64
ROPE_THETA = 10000.0
N_BUCKETS = 32
MAX_DISTANCE = 128
EPS = 1e-6
MASK_VALUE = -1e30
LOG2E = 1.4426950408889634
Q_SCALE = HEAD_DIM ** -0.5 * LOG2E

_QA, _KA, _VA, _GA = 0, WIDTH, WIDTH + KV_WIDTH, WIDTH + 2 * KV_WIDTH
_HALF = 2 * WIDTH + 2 * KV_WIDTH
_QB, _KB, _VB, _GB = _HALF + _QA, _HALF + _KA, _HALF + _VA, _HALF + _GA
_QK_GROUP = ((_QA, WIDTH), (_QB, WIDTH), (_KA, KV_WIDTH), (_KB, KV_WIDTH))
_GATE_GROUP = ((_GA, WIDTH), (_GB, WIDTH))
_V_GROUP = ((_VA, KV_WIDTH), (_VB, KV_WIDTH))

_F32 = jnp.float32
_BF16 = jnp.bfloat16

_VMEM_BYTES_V7X = 64 * 1024 * 1024
_VMEM_LIMIT_BYTES = 48 * 1024 * 1024

_TM_IN = 1024
_TQ_A = 256
_TK_A = 256
_KEYS_B = _TQ_A + 2 * WINDOW
_KEYS_Q = Q_BLOCK + 2 * WINDOW
_KEYS_Q0 = slice(0, _KEYS_Q)
_KEYS_Q1 = slice(Q_BLOCK, Q_BLOCK + _KEYS_Q)
_TM_OUT = 1024
_TM_OUT_CHUNKS = (256, 256, 256, 128, 128)
_TN_MOD = 1536

_TNT = (((0,), (1,)), ((), ()))
_SHIFT, _SCALE, _GATE = 0, 1, 2


def _silu(v):
    return 0.5 * v * (1.0 + jnp.tanh(0.5 * v))


def _params(n_grid, vmem_limit_bytes=_VMEM_LIMIT_BYTES):
    return pltpu.CompilerParams(
        dimension_semantics=("arbitrary",) * n_grid,
        vmem_limit_bytes=vmem_limit_bytes)


def _mod_kernel(c_ref, w_ref, b_ref, o_ref):
    ca = _silu(c_ref[...])
    o_ref[...] = jnp.dot(ca.astype(_BF16), w_ref[...].astype(_BF16),
                         preferred_element_type=_F32) + b_ref[...]


def _adaln_mod(c, w, b):
    bsz, d = c.shape
    n = w.shape[1]
    assert n % _TN_MOD == 0
    return pl.pallas_call(
        _mod_kernel,
        out_shape=jax.ShapeDtypeStruct((bsz, n), _F32),
        grid=(n // _TN_MOD,),
        in_specs=[pl.BlockSpec((bsz, d), lambda j: (0, 0)),
                  pl.BlockSpec((d, _TN_MOD), lambda j: (0, j)),
                  pl.BlockSpec((1, _TN_MOD), lambda j: (0, j))],
        out_specs=pl.BlockSpec((bsz, _TN_MOD), lambda j: (0, j)),
        compiler_params=_params(1),
        name="adaln_mod",
    )(c, w, b.reshape(1, n))


def _bias_kernel(present, table_ref, bucket_ref, o_ref):
    masked = jnp.full((Q_BLOCK, Q_BLOCK), MASK_VALUE, _F32)
    for h in range(N_HEADS):
        table = table_ref[h] * LOG2E
        tiles = []
        for d in range(3):
            bucket = bucket_ref[d]
            acc = masked
            for b in present[d]:
                acc = jnp.where(bucket == b, table[b:b + 1, :], acc)
            tiles.append(acc)
        lo, mid, hi = tiles
        for jb, (left, right) in enumerate(((lo, masked), (mid, lo), (hi, mid), (masked, hi))):
            rows = slice(jb * Q_BLOCK, (jb + 1) * Q_BLOCK)
            o_ref[h, rows, 0:Q_BLOCK] = left
            o_ref[h, rows, Q_BLOCK:2 * Q_BLOCK] = right


def _t5_bucket(rel):
    nb = N_BUCKETS // 2
    max_exact = nb // 2
    ret = (rel > 0).astype(np.int32) * nb
    n = np.abs(rel)
    assert MAX_DISTANCE == 2 * max_exact ** 2 and nb - max_exact == 8
    n2 = n.astype(np.int64) ** 2
    large = max_exact + sum((n2 >= 2 ** k).astype(np.int32) for k in range(7, 31))
    large = np.minimum(large, nb - 1)
    return ret + np.where(n < max_exact, n, large)


def _t5_bias_tiles(rel_table):
    r = np.arange(Q_BLOCK)
    rel = ((np.arange(3) - 1)[:, None, None] * Q_BLOCK
           + r[None, :, None] - r[None, None, :])
    bucket = np.where(np.abs(rel) <= WINDOW, _t5_bucket(rel), -1).astype(np.int32)
    present = tuple(tuple(int(b) for b in np.unique(tile) if b >= 0) for tile in bucket)
    return pl.pallas_call(
        functools.partial(_bias_kernel, present),
        out_shape=jax.ShapeDtypeStruct((N_HEADS, _KEYS_B, _TQ_A), _F32),
        in_specs=[pl.BlockSpec(memory_space=pltpu.VMEM),
                  pl.BlockSpec(memory_space=pltpu.VMEM)],
        out_specs=pl.BlockSpec(memory_space=pltpu.VMEM),
        compiler_params=pltpu.CompilerParams(vmem_limit_bytes=_VMEM_LIMIT_BYTES),
        name="t5_bias",
    )(jnp.broadcast_to(rel_table.T[:, :, None], (N_HEADS, N_BUCKETS, Q_BLOCK)),
      jnp.asarray(bucket))


def _norm_rope(xh, gain, cos, sin):
    ms = jnp.mean(xh * xh, axis=0, keepdims=True)
    y = xh * lax.rsqrt(ms + EPS) * gain
    q = HEAD_DIM // 4
    rot = jnp.concatenate([-y[q:2 * q], y[0:q], -y[3 * q:4 * q], y[2 * q:3 * q]], axis=0)
    return y * cos + rot * sin


def _inproj_kernel(x_ref, shift_ref, scale_ref, gpre_ref, w_ref, cos_ref, sin_ref,
                   qn_ref, kn_ref,
                   qa_ref, ka_ref, va_ref, ga_ref, qb_ref, kb_ref, vb_ref, gb_ref):
    x = x_ref[0]
    ms = jnp.mean(x * x, axis=-1, keepdims=True)
    h = x * lax.rsqrt(ms + EPS) * gpre_ref[...]
    h = h * (1.0 + scale_ref[0, 0]) + shift_ref[0, 0]
    hb = h.astype(_BF16)

    def proj(group):
        w = jnp.concatenate([w_ref[:, lo:lo + n] for lo, n in group], axis=1).astype(_BF16)
        return lax.dot_general(w, hb, _TNT, preferred_element_type=_F32)

    cos = cos_ref[...]
    sin = sin_ref[...]
    qn = qn_ref[...]
    kn = kn_ref[...]

    def store_tiled(ref, rows, val):
        tq = ref.shape[3]
        for j in range(ref.shape[1]):
            ref[0, j, rows, :] = val[:, j * tq:(j + 1) * tq]

    gates = _silu(proj(_GATE_GROUP))
    store_tiled(ga_ref, slice(0, WIDTH), gates[:WIDTH].astype(_BF16))
    store_tiled(gb_ref, slice(0, WIDTH), gates[WIDTH:].astype(_BF16))

    qk = proj(_QK_GROUP)
    for hd in range(N_HEADS):
        rows = slice(hd * HEAD_DIM, (hd + 1) * HEAD_DIM)
        store_tiled(qa_ref, rows, (_norm_rope(qk[rows], qn, cos, sin) * Q_SCALE).astype(_BF16))
    store_tiled(qb_ref, slice(0, WIDTH), (qk[WIDTH:2 * WIDTH] * Q_SCALE).astype(_BF16))
    ka = qk[2 * WIDTH:2 * WIDTH + KV_WIDTH]
    ka = jnp.concatenate(
        [_norm_rope(ka[hd * HEAD_DIM:(hd + 1) * HEAD_DIM], kn, cos, sin) for hd in range(N_KV)],
        axis=0)
    ka_ref[0] = ka.T.astype(_BF16)
    kb_ref[0] = qk[2 * WIDTH + KV_WIDTH:].T.astype(_BF16)

    v = proj(_V_GROUP)
    va_ref[0] = v[:KV_WIDTH].astype(_BF16)
    store_tiled(vb_ref, slice(0, KV_WIDTH), v[KV_WIDTH:].astype(_BF16))


def _rope_tables(seq):
    f32 = np.float32
    rows = seq // GRID_W
    row = np.repeat(np.arange(rows), GRID_W).astype(f32)
    col = np.tile(np.arange(GRID_W), rows).astype(f32)
    half = HEAD_DIM // 2
    freqs = np.power(f32(ROPE_THETA), -np.arange(0, half, 2, dtype=f32) / f32(half)).astype(f32)

    def ang2(pos):
        ang = pos[:, None] * freqs[None, :]
        return np.concatenate([ang, ang], axis=-1)

    ang = np.concatenate([ang2(row), ang2(col)], axis=-1)
    return jnp.asarray(np.cos(ang).T), jnp.asarray(np.sin(ang).T)


def _in_proj(x, mod, g_pre, w_in, cos_t, sin_t, qn, kn):
    bsz, seq, d = x.shape
    tm = _TM_IN
    assert seq % tm == 0 and tm % _TQ_A == 0 and w_in.shape == (d, IN_COLS)
    tok_t = lambda n: pl.BlockSpec((1, n, tm), lambda b, i: (b, 0, i))
    tok_m = lambda n: pl.BlockSpec((1, tm, n), lambda b, i: (b, i, 0))
    mod_row = lambda r: pl.BlockSpec((1, 1, 1, d), lambda b, i: (b, r, 0, 0))
    tab = pl.BlockSpec((HEAD_DIM, tm), lambda b, i: (0, i))
    gain = pl.BlockSpec((HEAD_DIM, tm), lambda b, i: (0, 0))
    shp_t = lambda n: jax.ShapeDtypeStruct((bsz, n, seq), _BF16)
    shp_m = lambda n: jax.ShapeDtypeStruct((bsz, seq, n), _BF16)
    tiled = pl.BlockSpec((1, tm // _TQ_A, WIDTH, _TQ_A), lambda b, i: (b, i, 0, 0))
    shp_tiled = jax.ShapeDtypeStruct((bsz, seq // _TQ_A, WIDTH, _TQ_A), _BF16)
    v_blocks = pl.BlockSpec((1, tm // Q_BLOCK, KV_WIDTH, Q_BLOCK), lambda b, i: (b, i, 0, 0))
    shp_v_blocks = jax.ShapeDtypeStruct((bsz, seq // Q_BLOCK, KV_WIDTH, Q_BLOCK), _BF16)
    return pl.pallas_call(
        _inproj_kernel,
        out_shape=(shp_tiled, shp_m(KV_WIDTH), shp_t(KV_WIDTH), shp_tiled,
                   shp_tiled, shp_m(KV_WIDTH), shp_v_blocks, shp_tiled),
        grid=(bsz, seq // tm),
        in_specs=[pl.BlockSpec((1, tm, d), lambda b, i: (b, i, 0)),
                  mod_row(_SHIFT), mod_row(_SCALE),
                  pl.BlockSpec((1, d), lambda b, i: (0, 0)),
                  pl.BlockSpec((d, IN_COLS), lambda b, i: (0, 0)),
                  tab, tab, gain, gain],
        out_specs=(tiled, tok_m(KV_WIDTH), tok_t(KV_WIDTH), tiled,
                   tiled, tok_m(KV_WIDTH), v_blocks, tiled),
        compiler_params=_params(2),
        name="in_proj",
    )(x, mod, mod, g_pre.reshape(1, d), w_in, cos_t, sin_t,
      jnp.broadcast_to(qn[:, None], (HEAD_DIM, tm)),
      jnp.broadcast_to(kn[:, None], (HEAD_DIM, tm)))


_PAIRS = N_HEADS // 2
_ONES_ROWS = 16
_LOOKAHEAD = 2
_TILES_PER_TRIP = 2


def _head_rows(hd):
    return slice(hd * HEAD_DIM, (hd + 1) * HEAD_DIM)


def _pair_q(head, pair):
    def padded(hd):
        qh = head(hd)
        z = jnp.zeros_like(qh)
        return jnp.concatenate([qh, z] if hd // GROUP == 0 else [z, qh], axis=0)
    return jnp.concatenate([padded(2 * pair), padded(2 * pair + 1)], axis=1)


def _pair_out(o, inv_l, gate, pair, tq):
    o = o * inv_l
    g0 = gate(2 * pair).astype(_F32)
    g1 = gate(2 * pair + 1).astype(_F32)
    return jnp.concatenate([o[:, :tq] * g0, o[:, tq:] * g1], axis=0).T.astype(_BF16)


def _swap_middle(a):
    b = Q_BLOCK
    return jnp.concatenate([a[:, :b], a[:, 2 * b:3 * b], a[:, b:2 * b], a[:, 3 * b:]], axis=1)


def _v_with_ones(v):
    return jnp.concatenate([v, jnp.ones((_ONES_ROWS, v.shape[1]), v.dtype)], axis=0)


def _mixers_kernel(qa_ref, ka_ref, va_ref, ga_ref, qb_ref, kb_ref, vb_ref, gb_ref,
                   qa_nx_ref, ka_nx_ref, qb_nx_ref, kb_nx_ref, sink_ref, bias_ref,
                   o_ref, sa0_ref, sa1_ref, sb0_ref, sb1_ref, m_ref):
    n_tiles, tq = qa_ref.shape[1], qa_ref.shape[3]
    seq = ka_ref.shape[1]
    n_blocks = seq // Q_BLOCK
    a_chunks = [slice(c * _TK_A, (c + 1) * _TK_A) for c in range(seq // _TK_A)]
    items = [(kind, p) for p in range(_PAIRS) for kind in ("a", "b")]
    bufs = (sa0_ref, sb0_ref, sa1_ref, sb1_ref) * (_PAIRS // 2)

    def staged(start, size, zero):
        return pl.ds(pl.multiple_of(zero + start, Q_BLOCK), size)

    def window_blocks(tile):
        first = (tq // Q_BLOCK) * tile - 1
        return [jnp.clip(first + j, 0, n_blocks - 1) for j in range(_KEYS_B // Q_BLOCK)]

    def score_chunks(kind, pair, buf, tile, zero, next_batch=False):
        if next_batch:
            q_ref, k_ref = (qa_nx_ref, ka_nx_ref) if kind == "a" else (qb_nx_ref, kb_nx_ref)
        else:
            q_ref, k_ref = (qa_ref, ka_ref) if kind == "a" else (qb_ref, kb_ref)

        def weights():
            return _pair_q(lambda hd: q_ref[0, tile, _head_rows(hd), :], pair)

        def global_chunk(rows):
            s = jnp.dot(k_ref[0, rows, :], weights(), preferred_element_type=_F32)
            buf[staged(rows.start, _TK_A, zero), :] = s
            return jnp.max(s, axis=0, keepdims=True)

        def window():
            k = jnp.concatenate(
                [k_ref[0, pl.ds(pl.multiple_of(blk * Q_BLOCK, Q_BLOCK), Q_BLOCK), :]
                 for blk in window_blocks(tile)], axis=0)
            w = _swap_middle(weights())

            def bias(keys, queries):
                return jnp.concatenate([bias_ref[2 * pair, keys, queries],
                                        bias_ref[2 * pair + 1, keys, queries]], axis=1)

            left = (jnp.dot(k[_KEYS_Q0], w[:, :tq], preferred_element_type=_F32)
                    + bias(_KEYS_Q0, slice(0, Q_BLOCK)))
            right = (jnp.dot(k[_KEYS_Q1], w[:, tq:], preferred_element_type=_F32)
                     + bias(_KEYS_Q1, slice(Q_BLOCK, 2 * Q_BLOCK)))
            left = jnp.concatenate(
                [jnp.where(tile == 0, MASK_VALUE, left[:Q_BLOCK]), left[Q_BLOCK:]], axis=0)
            right = jnp.concatenate(
                [right[:-Q_BLOCK], jnp.where(tile == n_tiles - 1, MASK_VALUE, right[-Q_BLOCK:])],
                axis=0)
            buf[staged(_KEYS_Q0.start, _KEYS_Q, zero), :tq] = left
            buf[staged(_KEYS_Q1.start, _KEYS_Q, zero), tq:] = right
            m = jnp.concatenate([jnp.max(left, axis=0, keepdims=True),
                                 jnp.max(right, axis=0, keepdims=True)], axis=1)
            return jnp.maximum(m, sink_ref[pair] * LOG2E)

        if kind == "a":
            return [functools.partial(global_chunk, rows) for rows in a_chunks]
        return [window]

    def pv_chunks(kind, pair, buf, tile, zero, m):
        kv = _head_rows(pair // (GROUP // 2))

        def global_chunk(rows):
            e = jnp.exp2(buf[staged(rows.start, _TK_A, zero), :] - m)
            return jnp.dot(_v_with_ones(va_ref[0, kv, rows]), e.astype(_BF16),
                           preferred_element_type=_F32)

        def window():
            e_left = jnp.exp2(buf[staged(_KEYS_Q0.start, _KEYS_Q, zero), :tq] - m[:, :tq])
            e_right = jnp.exp2(buf[staged(_KEYS_Q1.start, _KEYS_Q, zero), tq:] - m[:, tq:])
            v = jnp.concatenate([vb_ref[0, blk] for blk in window_blocks(tile)], axis=1)
            v = _v_with_ones(v[kv, :])
            return jnp.concatenate(
                [jnp.dot(v[:, _KEYS_Q0], e_left.astype(_BF16), preferred_element_type=_F32),
                 jnp.dot(v[:, _KEYS_Q1], e_right.astype(_BF16), preferred_element_type=_F32)],
                axis=1)

        if kind == "a":
            return [functools.partial(global_chunk, rows) for rows in a_chunks]
        return [window]

    def finish(kind, pair, tile, o, m):
        l = o[HEAD_DIM:HEAD_DIM + 1]
        if kind == "b":
            l = _swap_middle(l + jnp.exp2(sink_ref[pair] * LOG2E - m))
            o = _swap_middle(o)
        g_ref = ga_ref if kind == "a" else gb_ref
        lane0 = (0 if kind == "a" else WIDTH) + pair * 2 * HEAD_DIM
        o_ref[0, pl.ds(pl.multiple_of(tile * tq, tq), tq), lane0:lane0 + 2 * HEAD_DIM] = _pair_out(
            o[:HEAD_DIM], 1.0 / l, lambda hd: g_ref[0, tile, _head_rows(hd), :], pair, tq)

    def col_max(parts):
        return functools.reduce(jnp.maximum, parts)

    def tile_body(tile, carry, last_tile=False):
        zero = jnp.minimum(tile, 0)
        col_maxes = list(carry)
        for idx, (kind, pair) in enumerate(items):
            ahead = (idx + _LOOKAHEAD) % len(items)
            if idx + _LOOKAHEAD < len(items):
                nxt = score_chunks(*items[ahead], bufs[ahead], tile, zero)
            elif last_tile:
                nxt = score_chunks(*items[ahead], bufs[ahead], 0, zero, next_batch=True)
            else:
                nxt = score_chunks(*items[ahead], bufs[ahead], tile + 1, zero)
            m = col_maxes[idx]
            cur = pv_chunks(kind, pair, bufs[idx], tile, zero, m)
            parts, o = [], None
            for i in range(max(len(nxt), len(cur))):
                if i < len(nxt):
                    parts.append(nxt[i]())
                if i < len(cur):
                    pv = cur[i]()
                    o = pv if o is None else o + pv
            finish(kind, pair, tile, o, m)
            col_maxes.append(col_max(parts))
        return tuple(col_maxes[len(items):])

    first = jnp.minimum(pl.program_id(0), 0)

    @pl.when(pl.program_id(0) == 0)
    def _():
        for i in range(_LOOKAHEAD):
            m_ref[i] = col_max(
                [chunk() for chunk in score_chunks(*items[i], bufs[i], first, first)])

    def trip(i, carry, last_trip=False):
        for j in range(_TILES_PER_TRIP):
            carry = tile_body(_TILES_PER_TRIP * i + j, carry,
                              last_tile=last_trip and j == _TILES_PER_TRIP - 1)
        return carry

    n_trips = n_tiles // _TILES_PER_TRIP
    carry = lax.fori_loop(0, n_trips - 1, trip, tuple(m_ref[i] for i in range(_LOOKAHEAD)))
    carry = trip(first + (n_trips - 1), carry, last_trip=True)
    for i in range(_LOOKAHEAD):
        m_ref[i] = carry[i]


def _mixers(qa, ka, va, ga, qb, kb, vb, gb, sink, bias):
    bsz, n_tiles, _, tq = qa.shape
    seq = ka.shape[1]
    assert tq == 2 * Q_BLOCK == 2 * WINDOW and seq % _TK_A == 0
    assert n_tiles % _TILES_PER_TRIP == 0
    tiled = pl.BlockSpec((1, n_tiles, WIDTH, tq), lambda b: (b, 0, 0, 0))
    k_spec = pl.BlockSpec((1, seq, KV_WIDTH), lambda b: (b, 0, 0))
    const = lambda shape: pl.BlockSpec(shape, lambda b: (0,) * len(shape))
    nxt = lambda b: jnp.minimum(b + 1, bsz - 1)
    q_next = pl.BlockSpec((1, 1, WIDTH, tq), lambda b: (nxt(b), 0, 0, 0))
    return pl.pallas_call(
        _mixers_kernel,
        out_shape=jax.ShapeDtypeStruct((bsz, seq, 2 * WIDTH), _BF16),
        grid=(bsz,),
        in_specs=[tiled, k_spec, pl.BlockSpec((1, KV_WIDTH, seq), lambda b: (b, 0, 0)), tiled,
                  tiled, k_spec,
                  pl.BlockSpec((1, seq // Q_BLOCK, KV_WIDTH, Q_BLOCK), lambda b: (b, 0, 0, 0)),
                  tiled,
                  q_next, pl.BlockSpec((1, seq, KV_WIDTH), lambda b: (nxt(b), 0, 0)),
                  q_next, pl.BlockSpec((1, _KEYS_B, KV_WIDTH), lambda b: (nxt(b), 0, 0)),
                  const((_PAIRS, 1, 2 * tq)), const((N_HEADS, _KEYS_B, tq))],
        out_specs=pl.BlockSpec((1, seq, 2 * WIDTH), lambda b: (b, 0, 0)),
        scratch_shapes=([pltpu.VMEM((seq, 2 * tq), _F32)] * 2
                        + [pltpu.VMEM((_KEYS_B, 2 * tq), _F32)] * 2
                        + [pltpu.VMEM((_LOOKAHEAD, 1, 2 * tq), _F32)]),
        compiler_params=_params(1),
        name="mixers",
    )(qa, ka, va, ga, qb, kb, vb, gb, qa, ka, qb, kb,
      jnp.tile(jnp.broadcast_to(sink[:, None], (N_HEADS, Q_BLOCK)).reshape(_PAIRS, 1, 2 * Q_BLOCK),
               (1, 1, tq // Q_BLOCK)),
      bias)


def _outproj_kernel(a_ref, w_ref, x_ref, gate_ref, gpost_ref, o_ref):
    w = w_ref[...].astype(_BF16)
    starts = np.cumsum((0,) + _TM_OUT_CHUNKS)
    assert starts[-1] == a_ref.shape[1]
    for lo, hi in zip(starts[:-1], starts[1:]):
        rows = slice(int(lo), int(hi))
        y = jnp.dot(a_ref[0, rows, :], w, preferred_element_type=_F32)
        ms = jnp.mean(y * y, axis=-1, keepdims=True)
        yn = y * lax.rsqrt(ms + EPS) * gpost_ref[...]
        o_ref[0, rows, :] = x_ref[0, rows, :] + gate_ref[0, 0] * yn


def _out_proj(o, w_out, x, mod, g_post):
    bsz, seq, d = x.shape
    tm = _TM_OUT
    assert seq % tm == 0 and w_out.shape == (2 * WIDTH, d)
    xo = pl.BlockSpec((1, tm, d), lambda b, i: (b, i, 0))
    windows = 2 * (2 * tm * d * 4 + tm * 2 * WIDTH * 2) + 2 * WIDTH * d * 4
    vmem_limit = windows + 8 * max(_TM_OUT_CHUNKS) * d * 4
    assert vmem_limit <= _VMEM_BYTES_V7X - (4 << 20)
    return pl.pallas_call(
        _outproj_kernel,
        out_shape=jax.ShapeDtypeStruct((bsz, seq, d), _F32),
        grid=(bsz, seq // tm),
        in_specs=[pl.BlockSpec((1, tm, 2 * WIDTH), lambda b, i: (b, i, 0)),
                  pl.BlockSpec((2 * WIDTH, d), lambda b, i: (0, 0)),
                  xo,
                  pl.BlockSpec((1, 1, 1, d), lambda b, i: (b, _GATE, 0, 0)),
                  pl.BlockSpec((1, d), lambda b, i: (0, 0))],
        out_specs=xo,
        compiler_params=_params(2, vmem_limit),
        name="out_proj",
    )(o, w_out, x, mod, g_post.reshape(1, d))


def kernel(x, c, w_ada, b_ada, g_pre, g_post, w_in, qn_a, kn_a, sink_b, w_out, rel_table):
    bsz, seq, d = x.shape
    depth = w_ada.shape[0]
    cos_t, sin_t = _rope_tables(seq)
    bias = _t5_bias_tiles(rel_table)
    for l in range(depth):
        mod = _adaln_mod(c, w_ada[l], b_ada[l]).reshape(bsz, 3, 1, d)
        qa, ka, va, ga, qb, kb, vb, gb = _in_proj(
            x, mod, g_pre[l], w_in[l], cos_t, sin_t, qn_a[l], kn_a[l])
        o = _mixers(qa, ka, va, ga, qb, kb, vb, gb, sink_b[l], bias)
        x = _out_proj(o, w_out[l], x, mod, g_post[l])
    return x
```

```python
import functools

import jax
import jax.numpy as jnp
import numpy as np
from jax import lax
from jax.experimental import pallas as pl
from jax.experimental.pallas import tpu as pltpu

HEAD_DIM = 64
N_HEADS = 8
N_KV = 2
GROUP = N_HEADS // N_KV
WIDTH = N_HEADS * HEAD_DIM
KV_WIDTH = N_KV * HEAD_DIM
IN_COLS = 2 * (2 * WIDTH + 2 * KV_WIDTH)
Q_BLOCK = 128
WINDOW = 128
GRID_W = 64
ROPE_THETA = 10000.0
N_BUCKETS = 32
MAX_DISTANCE = 128
EPS = 1e-6
MASK_VALUE = -1e30
LOG2E = 1.4426950408889634
Q_SCALE = HEAD_DIM ** -0.5 * LOG2E

_QA, _KA, _VA, _GA = 0, WIDTH, WIDTH + KV_WIDTH, WIDTH + 2 * KV_WIDTH
_HALF = 2 * WIDTH + 2 * KV_WIDTH
_QB, _KB, _VB, _GB = _HALF + _QA, _HALF + _KA, _HALF + _VA, _HALF + _GA
_QK_GROUP = ((_QA, WIDTH), (_QB, WIDTH), (_KA, KV_WIDTH), (_KB, KV_WIDTH))
_GATE_GROUP = ((_GA, WIDTH), (_GB, WIDTH))
_V_GROUP = ((_VA, KV_WIDTH), (_VB, KV_WIDTH))

_F32 = jnp.float32
_BF16 = jnp.bfloat16

_VMEM_BYTES_V7X = 64 * 1024 * 1024
_VMEM_LIMIT_BYTES = 48 * 1024 * 1024

_TM_IN = 1024
_TQ_A = 256
_TK_A = 256
_KEYS_B = _TQ_A + 2 * WINDOW
_KEYS_Q = Q_BLOCK + 2 * WINDOW
_KEYS_Q0 = slice(0, _KEYS_Q)
_KEYS_Q1 = slice(Q_BLOCK, Q_BLOCK + _KEYS_Q)
_TM_OUT = 1024
_TM_OUT_CHUNKS = (256, 256, 256, 128, 128)
_TN_MOD = 1536

_TNT = (((0,), (1,)), ((), ()))
_SHIFT, _SCALE, _GATE = 0, 1, 2


def _silu(v):
    return 0.5 * v * (1.0 + jnp.tanh(0.5 * v))


def _params(n_grid, vmem_limit_bytes=_VMEM_LIMIT_BYTES):
    return pltpu.CompilerParams(
        dimension_semantics=("arbitrary",) * n_grid,
        vmem_limit_bytes=vmem_limit_bytes)


def _mod_kernel(c_ref, w_ref, b_ref, o_ref):
    ca = _silu(c_ref[...])
    o_ref[...] = jnp.dot(ca.astype(_BF16), w_ref[...].astype(_BF16),
                         preferred_element_type=_F32) + b_ref[...]


def _adaln_mod(c, w, b):
    bsz, d = c.shape
    n = w.shape[1]
    assert n % _TN_MOD == 0
    return pl.pallas_call(
        _mod_kernel,
        out_shape=jax.ShapeDtypeStruct((bsz, n), _F32),
        grid=(n // _TN_MOD,),
        in_specs=[pl.BlockSpec((bsz, d), lambda j: (0, 0)),
                  pl.BlockSpec((d, _TN_MOD), lambda j: (0, j)),
                  pl.BlockSpec((1, _TN_MOD), lambda j: (0, j))],
        out_specs=pl.BlockSpec((bsz, _TN_MOD), lambda j: (0, j)),
        compiler_params=_params(1),
        name="adaln_mod",
    )(c, w, b.reshape(1, n))


def _bias_kernel(present, table_ref, bucket_ref, o_ref):
    masked = jnp.full((Q_BLOCK, Q_BLOCK), MASK_VALUE, _F32)
    for h in range(table_ref.shape[0]):
        table = table_ref[h] * LOG2E
        tiles = []
        for d in range(3):
            bucket = bucket_ref[d]
            acc = masked
            for b in present[d]:
                acc = jnp.where(bucket == b, table[b:b + 1, :], acc)
            tiles.append(acc)
        lo, mid, hi = tiles
        for jb, (left, right) in enumerate(((lo, masked), (mid, lo), (hi, mid), (masked, hi))):
            rows = slice(jb * Q_BLOCK, (jb + 1) * Q_BLOCK)
            o_ref[h, rows, 0:Q_BLOCK] = left
            o_ref[h, rows, Q_BLOCK:2 * Q_BLOCK] = right


def _t5_bucket(rel):
    nb = N_BUCKETS // 2
    max_exact = nb // 2
    ret = (rel > 0).astype(np.int32) * nb
    n = np.abs(rel)
    assert MAX_DISTANCE == 2 * max_exact ** 2 and nb - max_exact == 8
    n2 = n.astype(np.int64) ** 2
    large = max_exact + sum((n2 >= 2 ** k).astype(np.int32) for k in range(7, 31))
    large = np.minimum(large, nb - 1)
    return ret + np.where(n < max_exact, n, large)


def _t5_bias_tiles(rel_table):
    r = np.arange(Q_BLOCK)
    rel = ((np.arange(3) - 1)[:, None, None] * Q_BLOCK
           + r[None, :, None] - r[None, None, :])
    bucket = np.where(np.abs(rel) <= WINDOW, _t5_bucket(rel), -1).astype(np.int32)
    present = tuple(tuple(int(b) for b in np.unique(tile) if b >= 0) for tile in bucket)
    return pl.pallas_call(
        functools.partial(_bias_kernel, present),
        out_shape=jax.ShapeDtypeStruct((N_HEADS, _KEYS_B, _TQ_A), _F32),
        grid=(2,),
        in_specs=[pl.BlockSpec((N_HEADS // 2, N_BUCKETS, Q_BLOCK), lambda j: (j, 0, 0)),
                  pl.BlockSpec((3, Q_BLOCK, Q_BLOCK), lambda j: (0, 0, 0))],
        out_specs=pl.BlockSpec((N_HEADS // 2, _KEYS_B, _TQ_A), lambda j: (j, 0, 0)),
        compiler_params=_params(1),
        name="t5_bias",
    )(jnp.broadcast_to(rel_table.T[:, :, None], (N_HEADS, N_BUCKETS, Q_BLOCK)),
      jnp.asarray(bucket))


def _norm_rope(xh, gain, cos, sin):
    ms = jnp.mean(xh * xh, axis=0, keepdims=True)
    y = xh * lax.rsqrt(ms + EPS) * gain
    q = HEAD_DIM // 4
    rot = jnp.concatenate([-y[q:2 * q], y[0:q], -y[3 * q:4 * q], y[2 * q:3 * q]], axis=0)
    return y * cos + rot * sin


def _inproj_kernel(x_ref, shift_ref, scale_ref, gpre_ref, w_ref, cos_ref, sin_ref,
                   qn_ref, kn_ref,
                   qa_ref, ka_ref, va_ref, ga_ref, qb_ref, kb_ref, vb_ref, gb_ref):
    x = x_ref[0]
    ms = jnp.mean(x * x, axis=-1, keepdims=True)
    h = x * lax.rsqrt(ms + EPS) * gpre_ref[...]
    h = h * (1.0 + scale_ref[0, 0]) + shift_ref[0, 0]
    hb = h.astype(_BF16)

    def proj(group):
        w = jnp.concatenate([w_ref[:, lo:lo + n] for lo, n in group], axis=1).astype(_BF16)
        return lax.dot_general(w, hb, _TNT, preferred_element_type=_F32)

    cos = cos_ref[...]
    sin = sin_ref[...]
    qn = qn_ref[...]
    kn = kn_ref[...]

    def store_tiled(ref, rows, val):
        tq = ref.shape[3]
        for j in range(ref.shape[1]):
            ref[0, j, rows, :] = val[:, j * tq:(j + 1) * tq]

    gates = _silu(proj(_GATE_GROUP))
    store_tiled(ga_ref, slice(0, WIDTH), gates[:WIDTH].astype(_BF16))
    store_tiled(gb_ref, slice(0, WIDTH), gates[WIDTH:].astype(_BF16))

    qk = proj(_QK_GROUP)
    for hd in range(N_HEADS):
        rows = slice(hd * HEAD_DIM, (hd + 1) * HEAD_DIM)
        store_tiled(qa_ref, rows, (_norm_rope(qk[rows], qn, cos, sin) * Q_SCALE).astype(_BF16))
    store_tiled(qb_ref, slice(0, WIDTH), (qk[WIDTH:2 * WIDTH] * Q_SCALE).astype(_BF16))
    ka = qk[2 * WIDTH:2 * WIDTH + KV_WIDTH]
    ka = jnp.concatenate(
        [_norm_rope(ka[hd * HEAD_DIM:(hd + 1) * HEAD_DIM], kn, cos, sin) for hd in range(N_KV)],
        axis=0)
    ka_ref[0] = ka.T.astype(_BF16)
    kb_ref[0] = qk[2 * WIDTH + KV_WIDTH:].T.astype(_BF16)

    v = proj(_V_GROUP)
    va_ref[0] = v[:KV_WIDTH].astype(_BF16)
    store_tiled(vb_ref, slice(0, KV_WIDTH), v[KV_WIDTH:].astype(_BF16))


def _rope_tables(seq):
    f32 = np.float32
    rows = seq // GRID_W
    row = np.repeat(np.arange(rows), GRID_W).astype(f32)
    col = np.tile(np.arange(GRID_W), rows).astype(f32)
    half = HEAD_DIM // 2
    freqs = np.power(f32(ROPE_THETA), -np.arange(0, half, 2, dtype=f32) / f32(half)).astype(f32)

    def ang2(pos):
        ang = pos[:, None] * freqs[None, :]
        return np.concatenate([ang, ang], axis=-1)

    ang = np.concatenate([ang2(row), ang2(col)], axis=-1)
    return jnp.asarray(np.cos(ang).T), jnp.asarray(np.sin(ang).T)


def _in_proj(x, mod, g_pre, w_in, cos_t, sin_t, qn, kn):
    bsz, seq, d = x.shape
    tm = _TM_IN
    assert seq % tm == 0 and tm % _TQ_A == 0 and w_in.shape == (d, IN_COLS)
    tok_t = lambda n: pl.BlockSpec((1, n, tm), lambda b, i: (b, 0, i))
    tok_m = lambda n: pl.BlockSpec((1, tm, n), lambda b, i: (b, i, 0))
    mod_row = lambda r: pl.BlockSpec((1, 1, 1, d), lambda b, i: (b, r, 0, 0))
    tab = pl.BlockSpec((HEAD_DIM, tm), lambda b, i: (0, i))
    gain = pl.BlockSpec((HEAD_DIM, tm), lambda b, i: (0, 0))
    shp_t = lambda n: jax.ShapeDtypeStruct((bsz, n, seq), _BF16)
    shp_m = lambda n: jax.ShapeDtypeStruct((bsz, seq, n), _BF16)
    tiled = pl.BlockSpec((1, tm // _TQ_A, WIDTH, _TQ_A), lambda b, i: (b, i, 0, 0))
    shp_tiled = jax.ShapeDtypeStruct((bsz, seq // _TQ_A, WIDTH, _TQ_A), _BF16)
    v_blocks = pl.BlockSpec((1, tm // Q_BLOCK, KV_WIDTH, Q_BLOCK), lambda b, i: (b, i, 0, 0))
    shp_v_blocks = jax.ShapeDtypeStruct((bsz, seq // Q_BLOCK, KV_WIDTH, Q_BLOCK), _BF16)
    return pl.pallas_call(
        _inproj_kernel,
        out_shape=(shp_tiled, shp_m(KV_WIDTH), shp_t(KV_WIDTH), shp_tiled,
                   shp_tiled, shp_m(KV_WIDTH), shp_v_blocks, shp_tiled),
        grid=(bsz, seq // tm),
        in_specs=[pl.BlockSpec((1, tm, d), lambda b, i: (b, i, 0)),
                  mod_row(_SHIFT), mod_row(_SCALE),
                  pl.BlockSpec((1, d), lambda b, i: (0, 0)),
                  pl.BlockSpec((d, IN_COLS), lambda b, i: (0, 0)),
                  tab, tab, gain, gain],
        out_specs=(tiled, tok_m(KV_WIDTH), tok_t(KV_WIDTH), tiled,
                   tiled, tok_m(KV_WIDTH), v_blocks, tiled),
        compiler_params=_params(2),
        name="in_proj",
    )(x, mod, mod, g_pre.reshape(1, d), w_in, cos_t, sin_t,
      jnp.broadcast_to(qn[:, None], (HEAD_DIM, tm)),
      jnp.broadcast_to(kn[:, None], (HEAD_DIM, tm)))


_PAIRS = N_HEADS // 2
_ONES_ROWS = 16
_LOOKAHEAD = 2
_TILES_PER_TRIP = 2


def _head_rows(hd):
    return slice(hd * HEAD_DIM, (hd + 1) * HEAD_DIM)


def _pair_q(head, pair):
    def padded(hd):
        qh = head(hd)
        z = jnp.zeros_like(qh)
        return jnp.concatenate([qh, z] if hd // GROUP == 0 else [z, qh], axis=0)
    return jnp.concatenate([padded(2 * pair), padded(2 * pair + 1)], axis=1)


def _pair_out(o, inv_l, gate, pair, tq):
    o = o * inv_l
    g0 = gate(2 * pair).astype(_F32)
    g1 = gate(2 * pair + 1).astype(_F32)
    return jnp.concatenate([o[:, :tq] * g0, o[:, tq:] * g1], axis=0).T.astype(_BF16)


def _swap_middle(a):
    b = Q_BLOCK
    return jnp.concatenate([a[:, :b], a[:, 2 * b:3 * b], a[:, b:2 * b], a[:, 3 * b:]], axis=1)


def _v_with_ones(v):
    return jnp.concatenate([v, jnp.ones((_ONES_ROWS, v.shape[1]), v.dtype)], axis=0)


def _mixers_kernel(layer, qa_ref, ka_ref, va_ref, ga_ref, qb_ref, kb_ref, vb_ref, gb_ref,
                   qa_nx_ref, ka_nx_ref, qb_nx_ref, kb_nx_ref, sink_ref, bias_ref,
                   o_ref, sa0_ref, sa1_ref, sb0_ref, sb1_ref, m_ref):
    n_tiles, tq = qa_ref.shape[1], qa_ref.shape[3]
    seq = ka_ref.shape[1]
    n_blocks = seq // Q_BLOCK
    a_chunks = [slice(c * _TK_A, (c + 1) * _TK_A) for c in range(seq // _TK_A)]
    items = [(kind, p) for p in range(_PAIRS) for kind in ("a", "b")]
    bufs = (sa0_ref, sb0_ref, sa1_ref, sb1_ref) * (_PAIRS // 2)

    def staged(start, size, zero):
        return pl.ds(pl.multiple_of(zero + start, Q_BLOCK), size)

    def sink_row(pair):
        first_head = lax.broadcasted_iota(jnp.int32, (1, 2 * tq), 1) // Q_BLOCK % 2 == 0
        return jnp.where(first_head, sink_ref[layer, 2 * pair], sink_ref[layer, 2 * pair + 1]) * LOG2E

    def window_blocks(tile):
        first = (tq // Q_BLOCK) * tile - 1
        return [jnp.clip(first + j, 0, n_blocks - 1) for j in range(_KEYS_B // Q_BLOCK)]

    def score_chunks(kind, pair, buf, tile, zero, next_batch=False):
        if next_batch:
            q_ref, k_ref = (qa_nx_ref, ka_nx_ref) if kind == "a" else (qb_nx_ref, kb_nx_ref)
        else:
            q_ref, k_ref = (qa_ref, ka_ref) if kind == "a" else (qb_ref, kb_ref)

        def weights():
            return _pair_q(lambda hd: q_ref[0, tile, _head_rows(hd), :], pair)

        def global_chunk(rows):
            s = jnp.dot(k_ref[0, rows, :], weights(), preferred_element_type=_F32)
            buf[staged(rows.start, _TK_A, zero), :] = s
            return jnp.max(s, axis=0, keepdims=True)

        def window():
            k = jnp.concatenate(
                [k_ref[0, pl.ds(pl.multiple_of(blk * Q_BLOCK, Q_BLOCK), Q_BLOCK), :]
                 for blk in window_blocks(tile)], axis=0)
            w = _swap_middle(weights())

            def bias(keys, queries):
                return jnp.concatenate([bias_ref[2 * pair, keys, queries],
                                        bias_ref[2 * pair + 1, keys, queries]], axis=1)

            left = (jnp.dot(k[_KEYS_Q0], w[:, :tq], preferred_element_type=_F32)
                    + bias(_KEYS_Q0, slice(0, Q_BLOCK)))
            right = (jnp.dot(k[_KEYS_Q1], w[:, tq:], preferred_element_type=_F32)
                     + bias(_KEYS_Q1, slice(Q_BLOCK, 2 * Q_BLOCK)))
            left = jnp.concatenate(
                [jnp.where(tile == 0, MASK_VALUE, left[:Q_BLOCK]), left[Q_BLOCK:]], axis=0)
            right = jnp.concatenate(
                [right[:-Q_BLOCK], jnp.where(tile == n_tiles - 1, MASK_VALUE, right[-Q_BLOCK:])],
                axis=0)
            buf[staged(_KEYS_Q0.start, _KEYS_Q, zero), :tq] = left
            buf[staged(_KEYS_Q1.start, _KEYS_Q, zero), tq:] = right
            m = jnp.concatenate([jnp.max(left, axis=0, keepdims=True),
                                 jnp.max(right, axis=0, keepdims=True)], axis=1)
            return jnp.maximum(m, sink_row(pair))

        if kind == "a":
            return [functools.partial(global_chunk, rows) for rows in a_chunks]
        return [window]

    def pv_chunks(kind, pair, buf, tile, zero, m):
        kv = _head_rows(pair // (GROUP // 2))

        def global_chunk(rows):
            e = jnp.exp2(buf[staged(rows.start, _TK_A, zero), :] - m)
            return jnp.dot(_v_with_ones(va_ref[0, kv, rows]), e.astype(_BF16),
                           preferred_element_type=_F32)

        def window():
            e_left = jnp.exp2(buf[staged(_KEYS_Q0.start, _KEYS_Q, zero), :tq] - m[:, :tq])
            e_right = jnp.exp2(buf[staged(_KEYS_Q1.start, _KEYS_Q, zero), tq:] - m[:, tq:])
            v = jnp.concatenate([vb_ref[0, blk] for blk in window_blocks(tile)], axis=1)
            v = _v_with_ones(v[kv, :])
            return jnp.concatenate(
                [jnp.dot(v[:, _KEYS_Q0], e_left.astype(_BF16), preferred_element_type=_F32),
                 jnp.dot(v[:, _KEYS_Q1], e_right.astype(_BF16), preferred_element_type=_F32)],
                axis=1)

        if kind == "a":
            return [functools.partial(global_chunk, rows) for rows in a_chunks]
        return [window]

    def finish(kind, pair, tile, o, m):
        l = o[HEAD_DIM:HEAD_DIM + 1]
        if kind == "b":
            l = _swap_middle(l + jnp.exp2(sink_row(pair) - m))
            o = _swap_middle(o)
        g_ref = ga_ref if kind == "a" else gb_ref
        lane0 = (0 if kind == "a" else WIDTH) + pair * 2 * HEAD_DIM
        o_ref[0, pl.ds(pl.multiple_of(tile * tq, tq), tq), lane0:lane0 + 2 * HEAD_DIM] = _pair_out(
            o[:HEAD_DIM], 1.0 / l, lambda hd: g_ref[0, tile, _head_rows(hd), :], pair, tq)

    def col_max(parts):
        return functools.reduce(jnp.maximum, parts)

    def tile_body(tile, carry, last_tile=False):
        zero = jnp.minimum(tile, 0)
        col_maxes = list(carry)
        for idx, (kind, pair) in enumerate(items):
            ahead = (idx + _LOOKAHEAD) % len(items)
            if idx + _LOOKAHEAD < len(items):
                nxt = score_chunks(*items[ahead], bufs[ahead], tile, zero)
            elif last_tile:
                nxt = score_chunks(*items[ahead], bufs[ahead], 0, zero, next_batch=True)
            else:
                nxt = score_chunks(*items[ahead], bufs[ahead], tile + 1, zero)
            m = col_maxes[idx]
            cur = pv_chunks(kind, pair, bufs[idx], tile, zero, m)
            parts, o = [], None
            for i in range(max(len(nxt), len(cur))):
                if i < len(nxt):
                    parts.append(nxt[i]())
                if i < len(cur):
                    pv = cur[i]()
                    o = pv if o is None else o + pv
            finish(kind, pair, tile, o, m)
            col_maxes.append(col_max(parts))
        return tuple(col_maxes[len(items):])

    first = jnp.minimum(pl.program_id(0), 0)

    @pl.when(pl.program_id(0) == 0)
    def _():
        for i in range(_LOOKAHEAD):
            m_ref[i] = col_max(
                [chunk() for chunk in score_chunks(*items[i], bufs[i], first, first)])

    def trip(i, carry, last_trip=False):
        for j in range(_TILES_PER_TRIP):
            carry = tile_body(_TILES_PER_TRIP * i + j, carry,
                              last_tile=last_trip and j == _TILES_PER_TRIP - 1)
        return carry

    n_trips = n_tiles // _TILES_PER_TRIP
    carry = lax.fori_loop(0, n_trips - 1, trip, tuple(m_ref[i] for i in range(_LOOKAHEAD)))
    carry = trip(first + (n_trips - 1), carry, last_trip=True)
    for i in range(_LOOKAHEAD):
        m_ref[i] = carry[i]


def _mixers(qa, ka, va, ga, qb, kb, vb, gb, sink_all, layer, bias):
    bsz, n_tiles, _, tq = qa.shape
    seq = ka.shape[1]
    assert tq == 2 * Q_BLOCK == 2 * WINDOW and seq % _TK_A == 0
    assert n_tiles % _TILES_PER_TRIP == 0
    tiled = pl.BlockSpec((1, n_tiles, WIDTH, tq), lambda b: (b, 0, 0, 0))
    k_spec = pl.BlockSpec((1, seq, KV_WIDTH), lambda b: (b, 0, 0))
    const = lambda shape: pl.BlockSpec(shape, lambda b: (0,) * len(shape))
    nxt = lambda b: jnp.minimum(b + 1, bsz - 1)
    q_next = pl.BlockSpec((1, 1, WIDTH, tq), lambda b: (nxt(b), 0, 0, 0))
    return pl.pallas_call(
        functools.partial(_mixers_kernel, layer),
        out_shape=jax.ShapeDtypeStruct((bsz, seq, 2 * WIDTH), _BF16),
        grid=(bsz,),
        in_specs=[tiled, k_spec, pl.BlockSpec((1, KV_WIDTH, seq), lambda b: (b, 0, 0)), tiled,
                  tiled, k_spec,
                  pl.BlockSpec((1, seq // Q_BLOCK, KV_WIDTH, Q_BLOCK), lambda b: (b, 0, 0, 0)),
                  tiled,
                  q_next, pl.BlockSpec((1, seq, KV_WIDTH), lambda b: (nxt(b), 0, 0)),
                  q_next, pl.BlockSpec((1, _KEYS_B, KV_WIDTH), lambda b: (nxt(b), 0, 0)),
                  pl.BlockSpec(memory_space=pltpu.SMEM), const((N_HEADS, _KEYS_B, tq))],
        out_specs=pl.BlockSpec((1, seq, 2 * WIDTH), lambda b: (b, 0, 0)),
        scratch_shapes=([pltpu.VMEM((seq, 2 * tq), _F32)] * 2
                        + [pltpu.VMEM((_KEYS_B, 2 * tq), _F32)] * 2
                        + [pltpu.VMEM((_LOOKAHEAD, 1, 2 * tq), _F32)]),
        compiler_params=_params(1),
        name="mixers",
    )(qa, ka, va, ga, qb, kb, vb, gb, qa, ka, qb, kb, sink_all, bias)


def _outproj_kernel(a_ref, w_ref, x_ref, gate_ref, gpost_ref, o_ref):
    w = w_ref[...].astype(_BF16)
    starts = np.cumsum((0,) + _TM_OUT_CHUNKS)
    assert starts[-1] == a_ref.shape[1]
    for lo, hi in zip(starts[:-1], starts[1:]):
        rows = slice(int(lo), int(hi))
        y = jnp.dot(a_ref[0, rows, :], w, preferred_element_type=_F32)
        ms = jnp.mean(y * y, axis=-1, keepdims=True)
        yn = y * lax.rsqrt(ms + EPS) * gpost_ref[...]
        o_ref[0, rows, :] = x_ref[0, rows, :] + gate_ref[0, 0] * yn


def _out_proj(o, w_out, x, mod, g_post):
    bsz, seq, d = x.shape
    tm = _TM_OUT
    assert seq % tm == 0 and w_out.shape == (2 * WIDTH, d)
    xo = pl.BlockSpec((1, tm, d), lambda b, i: (b, i, 0))
    windows = 2 * (2 * tm * d * 4 + tm * 2 * WIDTH * 2) + 2 * WIDTH * d * 4
    vmem_limit = windows + 8 * max(_TM_OUT_CHUNKS) * d * 4
    assert vmem_limit <= _VMEM_BYTES_V7X - (4 << 20)
    return pl.pallas_call(
        _outproj_kernel,
        out_shape=jax.ShapeDtypeStruct((bsz, seq, d), _F32),
        grid=(bsz, seq // tm),
        in_specs=[pl.BlockSpec((1, tm, 2 * WIDTH), lambda b, i: (b, i, 0)),
                  pl.BlockSpec((2 * WIDTH, d), lambda b, i: (0, 0)),
                  xo,
                  pl.BlockSpec((1, 1, 1, d), lambda b, i: (b, _GATE, 0, 0)),
                  pl.BlockSpec((1, d), lambda b, i: (0, 0))],
        out_specs=xo,
        compiler_params=_params(2, vmem_limit),
        name="out_proj",
    )(o, w_out, x, mod, g_post.reshape(1, d))


def kernel(x, c, w_ada, b_ada, g_pre, g_post, w_in, qn_a, kn_a, sink_b, w_out, rel_table):
    bsz, seq, d = x.shape
    depth = w_ada.shape[0]
    cos_t, sin_t = _rope_tables(seq)
    bias = _t5_bias_tiles(rel_table)
    for l in range(depth):
        mod = _adaln_mod(c, w_ada[l], b_ada[l]).reshape(bsz, 3, 1, d)
        qa, ka, va, ga, qb, kb, vb, gb = _in_proj(
            x, mod, g_pre[l], w_in[l], cos_t, sin_t, qn_a[l], kn_a[l])
        o = _mixers(qa, ka, va, ga, qb, kb, vb, gb, sink_b, l, bias)
        x = _out_proj(o, w_out[l], x, mod, g_post[l])
    return x
```

```python
import functools

import jax
import jax.numpy as jnp
import numpy as np
from jax import lax
from jax.experimental import pallas as pl
from jax.experimental.pallas import tpu as pltpu

HEAD_DIM = 64
N_HEADS = 8
N_KV = 2
GROUP = N_HEADS // N_KV
WIDTH = N_HEADS * HEAD_DIM
KV_WIDTH = N_KV * HEAD_DIM
IN_COLS = 2 * (2 * WIDTH + 2 * KV_WIDTH)
Q_BLOCK = 128
WINDOW = 128
GRID_W = 64
ROPE_THETA = 10000.0
N_BUCKETS = 32
MAX_DISTANCE = 128
EPS = 1e-6
MASK_VALUE = -1e30
LOG2E = 1.4426950408889634
Q_SCALE = HEAD_DIM ** -0.5 * LOG2E

_QA, _KA, _VA, _GA = 0, WIDTH, WIDTH + KV_WIDTH, WIDTH + 2 * KV_WIDTH
_HALF = 2 * WIDTH + 2 * KV_WIDTH
_QB, _KB, _VB, _GB = _HALF + _QA, _HALF + _KA, _HALF + _VA, _HALF + _GA
_QK_GROUP = ((_QA, WIDTH), (_QB, WIDTH), (_KA, KV_WIDTH), (_KB, KV_WIDTH))
_GATE_GROUP = ((_GA, WIDTH), (_GB, WIDTH))
_V_GROUP = ((_VA, KV_WIDTH), (_VB, KV_WIDTH))

_F32 = jnp.float32
_BF16 = jnp.bfloat16

_VMEM_BYTES_V7X = 64 * 1024 * 1024
_VMEM_LIMIT_BYTES = 48 * 1024 * 1024

_TM_IN = 1024
_TQ_A = 256
_TK_A = 256
_KEYS_B = _TQ_A + 2 * WINDOW
_KEYS_Q = Q_BLOCK + 2 * WINDOW
_KEYS_Q0 = slice(0, _KEYS_Q)
_KEYS_Q1 = slice(Q_BLOCK, Q_BLOCK + _KEYS_Q)
_TM_OUT = 1024
_TM_OUT_CHUNKS = (256, 256, 256, 128, 128)
_TN_MOD = 1536

_TNT = (((0,), (1,)), ((), ()))
_SHIFT, _SCALE, _GATE = 0, 1, 2


def _silu(v):
    return 0.5 * v * (1.0 + jnp.tanh(0.5 * v))


def _params(n_grid, vmem_limit_bytes=_VMEM_LIMIT_BYTES):
    return pltpu.CompilerParams(
        dimension_semantics=("arbitrary",) * n_grid,
        vmem_limit_bytes=vmem_limit_bytes)


def _mod_kernel(c_ref, w_ref, b_ref, o_ref):
    ca = _silu(c_ref[...])
    o_ref[...] = jnp.dot(ca.astype(_BF16), w_ref[...].astype(_BF16),
                         preferred_element_type=_F32) + b_ref[...]


def _adaln_mod(c, w, b):
    bsz, d = c.shape
    n = w.shape[1]
    assert n % _TN_MOD == 0
    return pl.pallas_call(
        _mod_kernel,
        out_shape=jax.ShapeDtypeStruct((bsz, n), _F32),
        grid=(n // _TN_MOD,),
        in_specs=[pl.BlockSpec((bsz, d), lambda j: (0, 0)),
                  pl.BlockSpec((d, _TN_MOD), lambda j: (0, j)),
                  pl.BlockSpec((1, _TN_MOD), lambda j: (0, j))],
        out_specs=pl.BlockSpec((bsz, _TN_MOD), lambda j: (0, j)),
        compiler_params=_params(1),
        name="adaln_mod",
    )(c, w, b.reshape(1, n))


def _bias_kernel(present, table_ref, bucket_ref, o_ref):
    masked = jnp.full((Q_BLOCK, Q_BLOCK), MASK_VALUE, _F32)
    for h in range(table_ref.shape[0]):
        table = table_ref[h] * LOG2E
        tiles = []
        for d in range(3):
            bucket = bucket_ref[d]
            acc = masked
            for b in present[d]:
                acc = jnp.where(bucket == b, table[b:b + 1, :], acc)
            tiles.append(acc)
        lo, mid, hi = tiles
        for jb, (left, right) in enumerate(((lo, masked), (mid, lo), (hi, mid), (masked, hi))):
            rows = slice(jb * Q_BLOCK, (jb + 1) * Q_BLOCK)
            o_ref[h, rows, 0:Q_BLOCK] = left
            o_ref[h, rows, Q_BLOCK:2 * Q_BLOCK] = right


def _t5_bucket(rel):
    nb = N_BUCKETS // 2
    max_exact = nb // 2
    ret = (rel > 0).astype(np.int32) * nb
    n = np.abs(rel)
    assert MAX_DISTANCE == 2 * max_exact ** 2 and nb - max_exact == 8
    n2 = n.astype(np.int64) ** 2
    large = max_exact + sum((n2 >= 2 ** k).astype(np.int32) for k in range(7, 31))
    large = np.minimum(large, nb - 1)
    return ret + np.where(n < max_exact, n, large)


def _t5_bias_tiles(rel_table):
    r = np.arange(Q_BLOCK)
    rel = ((np.arange(3) - 1)[:, None, None] * Q_BLOCK
           + r[None, :, None] - r[None, None, :])
    bucket = np.where(np.abs(rel) <= WINDOW, _t5_bucket(rel), -1).astype(np.int32)
    present = tuple(tuple(int(b) for b in np.unique(tile) if b >= 0) for tile in bucket)
    return pl.pallas_call(
        functools.partial(_bias_kernel, present),
        out_shape=jax.ShapeDtypeStruct((N_HEADS, _KEYS_B, _TQ_A), _F32),
        grid=(2,),
        in_specs=[pl.BlockSpec((N_HEADS // 2, N_BUCKETS, Q_BLOCK), lambda j: (j, 0, 0)),
                  pl.BlockSpec((3, Q_BLOCK, Q_BLOCK), lambda j: (0, 0, 0))],
        out_specs=pl.BlockSpec((N_HEADS // 2, _KEYS_B, _TQ_A), lambda j: (j, 0, 0)),
        compiler_params=_params(1),
        name="t5_bias",
    )(jnp.broadcast_to(rel_table.T[:, :, None], (N_HEADS, N_BUCKETS, Q_BLOCK)),
      jnp.asarray(bucket))


def _norm_rope(xh, gain, cos, sin):
    ms = jnp.mean(xh * xh, axis=0, keepdims=True)
    y = xh * lax.rsqrt(ms + EPS) * gain
    q = HEAD_DIM // 4
    rot = jnp.concatenate([-y[q:2 * q], y[0:q], -y[3 * q:4 * q], y[2 * q:3 * q]], axis=0)
    return y * cos + rot * sin


def _inproj_kernel(x_ref, shift_ref, scale_ref, gpre_ref, w_ref, cos_ref, sin_ref,
                   qn_ref, kn_ref,
                   qa_ref, ka_ref, va_ref, ga_ref, qb_ref, kb_ref, vb_ref, gb_ref):
    x = x_ref[0]
    ms = jnp.mean(x * x, axis=-1, keepdims=True)
    h = x * lax.rsqrt(ms + EPS) * gpre_ref[...]
    h = h * (1.0 + scale_ref[0, 0]) + shift_ref[0, 0]
    hb = h.astype(_BF16)

    def proj(group):
        w = jnp.concatenate([w_ref[:, lo:lo + n] for lo, n in group], axis=1).astype(_BF16)
        return lax.dot_general(w, hb, _TNT, preferred_element_type=_F32)

    cos = cos_ref[...]
    sin = sin_ref[...]
    qn = qn_ref[...]
    kn = kn_ref[...]

    def store_tiled(ref, rows, val):
        tq = ref.shape[3]
        for j in range(ref.shape[1]):
            ref[0, j, rows, :] = val[:, j * tq:(j + 1) * tq]

    gates = _silu(proj(_GATE_GROUP))
    store_tiled(ga_ref, slice(0, WIDTH), gates[:WIDTH].astype(_BF16))
    store_tiled(gb_ref, slice(0, WIDTH), gates[WIDTH:].astype(_BF16))

    qk = proj(_QK_GROUP)
    for hd in range(N_HEADS):
        rows = slice(hd * HEAD_DIM, (hd + 1) * HEAD_DIM)
        store_tiled(qa_ref, rows, (_norm_rope(qk[rows], qn, cos, sin) * Q_SCALE).astype(_BF16))
    store_tiled(qb_ref, slice(0, WIDTH), (qk[WIDTH:2 * WIDTH] * Q_SCALE).astype(_BF16))
    ka = qk[2 * WIDTH:2 * WIDTH + KV_WIDTH]
    ka = jnp.concatenate(
        [_norm_rope(ka[hd * HEAD_DIM:(hd + 1) * HEAD_DIM], kn, cos, sin) for hd in range(N_KV)],
        axis=0)
    ka_ref[0] = ka.T.astype(_BF16)
    kb_ref[0] = qk[2 * WIDTH + KV_WIDTH:].T.astype(_BF16)

    v = proj(_V_GROUP)
    va_ref[0] = v[:KV_WIDTH].astype(_BF16)
    store_tiled(vb_ref, slice(0, KV_WIDTH), v[KV_WIDTH:].astype(_BF16))


def _rope_tables(seq):
    f32 = np.float32
    rows = seq // GRID_W
    row = np.repeat(np.arange(rows), GRID_W).astype(f32)
    col = np.tile(np.arange(GRID_W), rows).astype(f32)
    half = HEAD_DIM // 2
    freqs = np.power(f32(ROPE_THETA), -np.arange(0, half, 2, dtype=f32) / f32(half)).astype(f32)

    def ang2(pos):
        ang = pos[:, None] * freqs[None, :]
        return np.concatenate([ang, ang], axis=-1)

    ang = np.concatenate([ang2(row), ang2(col)], axis=-1)
    return jnp.asarray(np.cos(ang).T), jnp.asarray(np.sin(ang).T)


def _in_proj(x, mod, g_pre, w_in, cos_t, sin_t, qn, kn):
    bsz, seq, d = x.shape
    tm = _TM_IN
    assert seq % tm == 0 and tm % _TQ_A == 0 and w_in.shape == (d, IN_COLS)
    tok_t = lambda n: pl.BlockSpec((1, n, tm), lambda b, i: (b, 0, i))
    tok_m = lambda n: pl.BlockSpec((1, tm, n), lambda b, i: (b, i, 0))
    mod_row = lambda r: pl.BlockSpec((1, 1, 1, d), lambda b, i: (b, r, 0, 0))
    tab = pl.BlockSpec((HEAD_DIM, tm), lambda b, i: (0, i))
    gain = pl.BlockSpec((HEAD_DIM, tm), lambda b, i: (0, 0))
    shp_t = lambda n: jax.ShapeDtypeStruct((bsz, n, seq), _BF16)
    shp_m = lambda n: jax.ShapeDtypeStruct((bsz, seq, n), _BF16)
    tiled = pl.BlockSpec((1, tm // _TQ_A, WIDTH, _TQ_A), lambda b, i: (b, i, 0, 0))
    shp_tiled = jax.ShapeDtypeStruct((bsz, seq // _TQ_A, WIDTH, _TQ_A), _BF16)
    v_blocks = pl.BlockSpec((1, tm // Q_BLOCK, KV_WIDTH, Q_BLOCK), lambda b, i: (b, i, 0, 0))
    shp_v_blocks = jax.ShapeDtypeStruct((bsz, seq // Q_BLOCK, KV_WIDTH, Q_BLOCK), _BF16)
    return pl.pallas_call(
        _inproj_kernel,
        out_shape=(shp_tiled, shp_m(KV_WIDTH), shp_t(KV_WIDTH), shp_tiled,
                   shp_tiled, shp_m(KV_WIDTH), shp_v_blocks, shp_tiled),
        grid=(bsz, seq // tm),
        in_specs=[pl.BlockSpec((1, tm, d), lambda b, i: (b, i, 0)),
                  mod_row(_SHIFT), mod_row(_SCALE),
                  pl.BlockSpec((1, d), lambda b, i: (0, 0)),
                  pl.BlockSpec((d, IN_COLS), lambda b, i: (0, 0)),
                  tab, tab, gain, gain],
        out_specs=(tiled, tok_m(KV_WIDTH), tok_t(KV_WIDTH), tiled,
                   tiled, tok_m(KV_WIDTH), v_blocks, tiled),
        compiler_params=_params(2),
        name="in_proj",
    )(x, mod, mod, g_pre.reshape(1, d), w_in, cos_t, sin_t,
      jnp.broadcast_to(qn[:, None], (HEAD_DIM, tm)),
      jnp.broadcast_to(kn[:, None], (HEAD_DIM, tm)))


_PAIRS = N_HEADS // 2
_ONES_ROWS = 16
_LOOKAHEAD = 2
_TILES_PER_TRIP = 2


def _head_rows(hd):
    return slice(hd * HEAD_DIM, (hd + 1) * HEAD_DIM)


def _pair_q(head, pair):
    def padded(hd):
        qh = head(hd)
        z = jnp.zeros_like(qh)
        return jnp.concatenate([qh, z] if hd // GROUP == 0 else [z, qh], axis=0)
    return jnp.concatenate([padded(2 * pair), padded(2 * pair + 1)], axis=1)


def _pair_out(o, inv_l, gate, pair, tq):
    o = o * inv_l
    g0 = gate(2 * pair).astype(_F32)
    g1 = gate(2 * pair + 1).astype(_F32)
    return jnp.concatenate([o[:, :tq] * g0, o[:, tq:] * g1], axis=0).T.astype(_BF16)


def _swap_middle(a):
    b = Q_BLOCK
    return jnp.concatenate([a[:, :b], a[:, 2 * b:3 * b], a[:, b:2 * b], a[:, 3 * b:]], axis=1)


def _v_with_ones(v):
    return jnp.concatenate([v, jnp.ones((_ONES_ROWS, v.shape[1]), v.dtype)], axis=0)


def _mixers_kernel(layer, qa_ref, ka_ref, va_ref, ga_ref, qb_ref, kb_ref, vb_ref, gb_ref,
                   qa_nx_ref, ka_nx_ref, qb_nx_ref, kb_nx_ref, sink_ref, bias_ref,
                   o_ref, sa0_ref, sa1_ref, sb0_ref, sb1_ref, m_ref):
    n_tiles, tq = qa_ref.shape[1], qa_ref.shape[3]
    seq = ka_ref.shape[1]
    n_blocks = seq // Q_BLOCK
    a_chunks = [slice(c * _TK_A, (c + 1) * _TK_A) for c in range(seq // _TK_A)]
    items = [(kind, p) for p in range(_PAIRS) for kind in ("a", "b")]
    bufs = (sa0_ref, sb0_ref, sa1_ref, sb1_ref) * (_PAIRS // 2)

    def staged(start, size, zero):
        return pl.ds(pl.multiple_of(zero + start, Q_BLOCK), size)

    def sink_row(pair):
        first_head = lax.broadcasted_iota(jnp.int32, (1, 2 * tq), 1) // Q_BLOCK % 2 == 0
        return jnp.where(first_head, sink_ref[layer, 2 * pair], sink_ref[layer, 2 * pair + 1]) * LOG2E

    def window_blocks(tile):
        first = (tq // Q_BLOCK) * tile - 1
        return [jnp.clip(first + j, 0, n_blocks - 1) for j in range(_KEYS_B // Q_BLOCK)]

    def score_chunks(kind, pair, buf, tile, zero, next_batch=False):
        if next_batch:
            q_ref, k_ref = (qa_nx_ref, ka_nx_ref) if kind == "a" else (qb_nx_ref, kb_nx_ref)
        else:
            q_ref, k_ref = (qa_ref, ka_ref) if kind == "a" else (qb_ref, kb_ref)

        def weights():
            return _pair_q(lambda hd: q_ref[0, tile, _head_rows(hd), :], pair)

        def global_chunk(rows):
            s = jnp.dot(k_ref[0, rows, :], weights(), preferred_element_type=_F32)
            buf[staged(rows.start, _TK_A, zero), :] = s
            return jnp.max(s, axis=0, keepdims=True)

        def window():
            k = jnp.concatenate(
                [k_ref[0, pl.ds(pl.multiple_of(blk * Q_BLOCK, Q_BLOCK), Q_BLOCK), :]
                 for blk in window_blocks(tile)], axis=0)
            w = _swap_middle(weights())

            def bias(keys, queries):
                return jnp.concatenate([bias_ref[2 * pair, keys, queries],
                                        bias_ref[2 * pair + 1, keys, queries]], axis=1)

            left = (jnp.dot(k[_KEYS_Q0], w[:, :tq], preferred_element_type=_F32)
                    + bias(_KEYS_Q0, slice(0, Q_BLOCK)))
            right = (jnp.dot(k[_KEYS_Q1], w[:, tq:], preferred_element_type=_F32)
                     + bias(_KEYS_Q1, slice(Q_BLOCK, 2 * Q_BLOCK)))
            left = jnp.concatenate(
                [jnp.where(tile == 0, MASK_VALUE, left[:Q_BLOCK]), left[Q_BLOCK:]], axis=0)
            right = jnp.concatenate(
                [right[:-Q_BLOCK], jnp.where(tile == n_tiles - 1, MASK_VALUE, right[-Q_BLOCK:])],
                axis=0)
            buf[staged(_KEYS_Q0.start, _KEYS_Q, zero), :tq] = left
            buf[staged(_KEYS_Q1.start, _KEYS_Q, zero), tq:] = right
            m = jnp.concatenate([jnp.max(left, axis=0, keepdims=True),
                                 jnp.max(right, axis=0, keepdims=True)], axis=1)
            return jnp.maximum(m, sink_row(pair))

        if kind == "a":
            return [functools.partial(global_chunk, rows) for rows in a_chunks]
        return [window]

    def pv_chunks(kind, pair, buf, tile, zero, m):
        kv = _head_rows(pair // (GROUP // 2))

        def global_chunk(rows):
            e = jnp.exp2(buf[staged(rows.start, _TK_A, zero), :] - m)
            return jnp.dot(_v_with_ones(va_ref[0, kv, rows]), e.astype(_BF16),
                           preferred_element_type=_F32)

        def window():
            e_left = jnp.exp2(buf[staged(_KEYS_Q0.start, _KEYS_Q, zero), :tq] - m[:, :tq])
            e_right = jnp.exp2(buf[staged(_KEYS_Q1.start, _KEYS_Q, zero), tq:] - m[:, tq:])
            v = jnp.concatenate([vb_ref[0, blk] for blk in window_blocks(tile)], axis=1)
            v = _v_with_ones(v[kv, :])
            return jnp.concatenate(
                [jnp.dot(v[:, _KEYS_Q0], e_left.astype(_BF16), preferred_element_type=_F32),
                 jnp.dot(v[:, _KEYS_Q1], e_right.astype(_BF16), preferred_element_type=_F32)],
                axis=1)

        if kind == "a":
            return [functools.partial(global_chunk, rows) for rows in a_chunks]
        return [window]

    def finish(kind, pair, tile, o, m):
        l = o[HEAD_DIM:HEAD_DIM + 1]
        if kind == "b":
            l = _swap_middle(l + jnp.exp2(sink_row(pair) - m))
            o = _swap_middle(o)
        g_ref = ga_ref if kind == "a" else gb_ref
        lane0 = (0 if kind == "a" else WIDTH) + pair * 2 * HEAD_DIM
        o_ref[0, pl.ds(pl.multiple_of(tile * tq, tq), tq), lane0:lane0 + 2 * HEAD_DIM] = _pair_out(
            o[:HEAD_DIM], 1.0 / l, lambda hd: g_ref[0, tile, _head_rows(hd), :], pair, tq)

    def col_max(parts):
        return functools.reduce(jnp.maximum, parts)

    def tile_body(tile, carry, last_tile=False):
        zero = jnp.minimum(tile, 0)
        col_maxes = list(carry)
        for idx, (kind, pair) in enumerate(items):
            ahead = (idx + _LOOKAHEAD) % len(items)
            if idx + _LOOKAHEAD < len(items):
                nxt = score_chunks(*items[ahead], bufs[ahead], tile, zero)
            elif last_tile:
                nxt = score_chunks(*items[ahead], bufs[ahead], 0, zero, next_batch=True)
            else:
                nxt = score_chunks(*items[ahead], bufs[ahead], tile + 1, zero)
            m = col_maxes[idx]
            cur = pv_chunks(kind, pair, bufs[idx], tile, zero, m)
            parts, o = [], None
            for i in range(max(len(nxt), len(cur))):
                if i < len(nxt):
                    parts.append(nxt[i]())
                if i < len(cur):
                    pv = cur[i]()
                    o = pv if o is None else o + pv
            finish(kind, pair, tile, o, m)
            col_maxes.append(col_max(parts))
        return tuple(col_maxes[len(items):])

    first = jnp.minimum(pl.program_id(0), 0)

    @pl.when(pl.program_id(0) == 0)
    def _():
        for i in range(_LOOKAHEAD):
            m_ref[i] = col_max(
                [chunk() for chunk in score_chunks(*items[i], bufs[i], first, first)])

    def trip(i, carry, last_trip=False):
        for j in range(_TILES_PER_TRIP):
            carry = tile_body(_TILES_PER_TRIP * i + j, carry,
                              last_tile=last_trip and j == _TILES_PER_TRIP - 1)
        return carry

    n_trips = n_tiles // _TILES_PER_TRIP
    carry = lax.fori_loop(0, n_trips - 1, trip, tuple(m_ref[i] for i in range(_LOOKAHEAD)))
    carry = trip(first + (n_trips - 1), carry, last_trip=True)
    for i in range(_LOOKAHEAD):
        m_ref[i] = carry[i]


def _mixers(qa, ka, va, ga, qb, kb, vb, gb, sink_all, layer, bias):
    bsz, n_tiles, _, tq = qa.shape
    seq = ka.shape[1]
    assert tq == 2 * Q_BLOCK == 2 * WINDOW and seq % _TK_A == 0
    assert n_tiles % _TILES_PER_TRIP == 0
    tiled = pl.BlockSpec((1, n_tiles, WIDTH, tq), lambda b: (b, 0, 0, 0))
    k_spec = pl.BlockSpec((1, seq, KV_WIDTH), lambda b: (b, 0, 0))
    const = lambda shape: pl.BlockSpec(shape, lambda b: (0,) * len(shape))
    nxt = lambda b: jnp.minimum(b + 1, bsz - 1)
    q_next = pl.BlockSpec((1, 1, WIDTH, tq), lambda b: (nxt(b), 0, 0, 0))
    return pl.pallas_call(
        functools.partial(_mixers_kernel, layer),
        out_shape=jax.ShapeDtypeStruct((bsz, seq, 2 * WIDTH), _BF16),
        grid=(bsz,),
        in_specs=[tiled, k_spec, pl.BlockSpec((1, KV_WIDTH, seq), lambda b: (b, 0, 0)), tiled,
                  tiled, k_spec,
                  pl.BlockSpec((1, seq // Q_BLOCK, KV_WIDTH, Q_BLOCK), lambda b: (b, 0, 0, 0)),
                  tiled,
                  q_next, pl.BlockSpec((1, seq, KV_WIDTH), lambda b: (nxt(b), 0, 0)),
                  q_next, pl.BlockSpec((1, _KEYS_B, KV_WIDTH), lambda b: (nxt(b), 0, 0)),
                  pl.BlockSpec(memory_space=pltpu.SMEM), const((N_HEADS, _KEYS_B, tq))],
        out_specs=pl.BlockSpec((1, seq, 2 * WIDTH), lambda b: (b, 0, 0)),
        scratch_shapes=([pltpu.VMEM((seq, 2 * tq), _F32)] * 2
                        + [pltpu.VMEM((_KEYS_B, 2 * tq), _F32)] * 2
                        + [pltpu.VMEM((_LOOKAHEAD, 1, 2 * tq), _F32)]),
        compiler_params=_params(1),
        name="mixers",
    )(qa, ka, va, ga, qb, kb, vb, gb, qa, ka, qb, kb, sink_all, bias)


def _outproj_kernel(a_ref, w_ref, x_ref, gate_ref, gpost_ref, o_ref):
    w = w_ref[...].astype(_BF16)
    starts = np.cumsum((0,) + _TM_OUT_CHUNKS)
    assert starts[-1] == a_ref.shape[1]
    for lo, hi in zip(starts[:-1], starts[1:]):
        rows = slice(int(lo), int(hi))
        y = jnp.dot(a_ref[0, rows, :], w, preferred_element_type=_F32)
        ms = jnp.mean(y * y, axis=-1, keepdims=True)
        yn = y * lax.rsqrt(ms + EPS) * gpost_ref[...]
        o_ref[0, rows, :] = x_ref[0, rows, :] + gate_ref[0, 0] * yn


def _out_proj(o, w_out, x, mod, g_post):
    bsz, seq, d = x.shape
    tm = _TM_OUT
    assert seq % tm == 0 and w_out.shape == (2 * WIDTH, d)
    xo = pl.BlockSpec((1, tm, d), lambda b, i: (b, i, 0))
    windows = 2 * (2 * tm * d * 4 + tm * 2 * WIDTH * 2) + 2 * WIDTH * d * 4
    vmem_limit = windows + 8 * max(_TM_OUT_CHUNKS) * d * 4
    assert vmem_limit <= _VMEM_BYTES_V7X - (4 << 20)
    return pl.pallas_call(
        _outproj_kernel,
        out_shape=jax.ShapeDtypeStruct((bsz, seq, d), _F32),
        grid=(bsz, seq // tm),
        in_specs=[pl.BlockSpec((1, tm, 2 * WIDTH), lambda b, i: (b, i, 0)),
                  pl.BlockSpec((2 * WIDTH, d), lambda b, i: (0, 0)),
                  xo,
                  pl.BlockSpec((1, 1, 1, d), lambda b, i: (b, _GATE, 0, 0)),
                  pl.BlockSpec((1, d), lambda b, i: (0, 0))],
        out_specs=xo,
        compiler_params=_params(2, vmem_limit),
        name="out_proj",
    )(o, w_out, x, mod, g_post.reshape(1, d))


def kernel(x, c, w_ada, b_ada, g_pre, g_post, w_in, qn_a, kn_a, sink_b, w_out, rel_table):
    bsz, seq, d = x.shape
    depth = w_ada.shape[0]
    cos_t, sin_t = _rope_tables(seq)
    bias = None
    for l in range(depth):
        mod = _adaln_mod(c, w_ada[l], b_ada[l]).reshape(bsz, 3, 1, d)
        qa, ka, va, ga, qb, kb, vb, gb = _in_proj(
            x, mod, g_pre[l], w_in[l], cos_t, sin_t, qn_a[l], kn_a[l])
        if bias is None:
            bias = _t5_bias_tiles(rel_table)
        o = _mixers(qa, ka, va, ga, qb, kb, vb, gb, sink_b, l, bias)
        x = _out_proj(o, w_out[l], x, mod, g_post[l])
    return x
```

```python
import functools

import jax
import jax.numpy as jnp
import numpy as np
from jax import lax
from jax.experimental import pallas as pl
from jax.experimental.pallas import tpu as pltpu

HEAD_DIM = 64
N_HEADS = 8
N_KV = 2
GROUP = N_HEADS // N_KV
WIDTH = N_HEADS * HEAD_DIM
KV_WIDTH = N_KV * HEAD_DIM
IN_COLS = 2 * (2 * WIDTH + 2 * KV_WIDTH)
Q_BLOCK = 128
WINDOW = 128
GRID_W = 64
ROPE_THETA = 10000.0
N_BUCKETS = 32
MAX_DISTANCE = 128
EPS = 1e-6
MASK_VALUE = -1e30
LOG2E = 1.4426950408889634
Q_SCALE = HEAD_DIM ** -0.5 * LOG2E

_QA, _KA, _VA, _GA = 0, WIDTH, WIDTH + KV_WIDTH, WIDTH + 2 * KV_WIDTH
_HALF = 2 * WIDTH + 2 * KV_WIDTH
_QB, _KB, _VB, _GB = _HALF + _QA, _HALF + _KA, _HALF + _VA, _HALF + _GA
_QK_GROUP = ((_QA, WIDTH), (_QB, WIDTH), (_KA, KV_WIDTH), (_KB, KV_WIDTH))
_GATE_GROUP = ((_GA, WIDTH), (_GB, WIDTH))
_V_GROUP = ((_VA, KV_WIDTH), (_VB, KV_WIDTH))

_F32 = jnp.float32
_BF16 = jnp.bfloat16

_VMEM_BYTES_V7X = 64 * 1024 * 1024
_VMEM_LIMIT_BYTES = 48 * 1024 * 1024

_TM_IN = 1024
_TQ_A = 256
_TK_A = 256
_KEYS_B = _TQ_A + 2 * WINDOW
_KEYS_Q = Q_BLOCK + 2 * WINDOW
_KEYS_Q0 = slice(0, _KEYS_Q)
_KEYS_Q1 = slice(Q_BLOCK, Q_BLOCK + _KEYS_Q)
_TM_OUT = 1024
_TM_OUT_CHUNKS = (256, 256, 256, 128, 128)
_TN_MOD = 1536

_TNT = (((0,), (1,)), ((), ()))
_SHIFT, _SCALE, _GATE = 0, 1, 2


def _silu(v):
    return 0.5 * v * (1.0 + jnp.tanh(0.5 * v))


def _params(n_grid, vmem_limit_bytes=_VMEM_LIMIT_BYTES):
    return pltpu.CompilerParams(
        dimension_semantics=("arbitrary",) * n_grid,
        vmem_limit_bytes=vmem_limit_bytes)


def _mod_kernel(c_ref, w_ref, b_ref, o_ref):
    ca = _silu(c_ref[...])
    o_ref[...] = jnp.dot(ca.astype(_BF16), w_ref[...].astype(_BF16),
                         preferred_element_type=_F32) + b_ref[...]


def _adaln_mod(c, w, b):
    bsz, d = c.shape
    n = w.shape[1]
    assert n % _TN_MOD == 0
    return pl.pallas_call(
        _mod_kernel,
        out_shape=jax.ShapeDtypeStruct((bsz, n), _F32),
        grid=(n // _TN_MOD,),
        in_specs=[pl.BlockSpec((bsz, d), lambda j: (0, 0)),
                  pl.BlockSpec((d, _TN_MOD), lambda j: (0, j)),
                  pl.BlockSpec((1, _TN_MOD), lambda j: (0, j))],
        out_specs=pl.BlockSpec((bsz, _TN_MOD), lambda j: (0, j)),
        compiler_params=_params(1),
        name="adaln_mod",
    )(c, w, b.reshape(1, n))


def _bias_kernel(present, table_ref, bucket_ref, o_ref):
    masked = jnp.full((Q_BLOCK, Q_BLOCK), MASK_VALUE, _F32)
    for h in range(table_ref.shape[0]):
        table = table_ref[h] * LOG2E
        tiles = []
        for d in range(3):
            bucket = bucket_ref[d]
            acc = masked
            for b in present[d]:
                acc = jnp.where(bucket == b, table[b:b + 1, :], acc)
            tiles.append(acc)
        lo, mid, hi = tiles
        for jb, (left, right) in enumerate(((lo, masked), (mid, lo), (hi, mid), (masked, hi))):
            rows = slice(jb * Q_BLOCK, (jb + 1) * Q_BLOCK)
            o_ref[h, rows, 0:Q_BLOCK] = left
            o_ref[h, rows, Q_BLOCK:2 * Q_BLOCK] = right


def _t5_bucket(rel):
    nb = N_BUCKETS // 2
    max_exact = nb // 2
    ret = (rel > 0).astype(np.int32) * nb
    n = np.abs(rel)
    assert MAX_DISTANCE == 2 * max_exact ** 2 and nb - max_exact == 8
    n2 = n.astype(np.int64) ** 2
    large = max_exact + sum((n2 >= 2 ** k).astype(np.int32) for k in range(7, 31))
    large = np.minimum(large, nb - 1)
    return ret + np.where(n < max_exact, n, large)


def _t5_bias_tiles(rel_table):
    r = np.arange(Q_BLOCK)
    rel = ((np.arange(3) - 1)[:, None, None] * Q_BLOCK
           + r[None, :, None] - r[None, None, :])
    bucket = np.where(np.abs(rel) <= WINDOW, _t5_bucket(rel), -1).astype(np.int32)
    present = tuple(tuple(int(b) for b in np.unique(tile) if b >= 0) for tile in bucket)
    return pl.pallas_call(
        functools.partial(_bias_kernel, present),
        out_shape=jax.ShapeDtypeStruct((N_HEADS, _KEYS_B, _TQ_A), _F32),
        grid=(2,),
        in_specs=[pl.BlockSpec((N_HEADS // 2, N_BUCKETS, Q_BLOCK), lambda j: (j, 0, 0)),
                  pl.BlockSpec((3, Q_BLOCK, Q_BLOCK), lambda j: (0, 0, 0))],
        out_specs=pl.BlockSpec((N_HEADS // 2, _KEYS_B, _TQ_A), lambda j: (j, 0, 0)),
        compiler_params=_params(1),
        name="t5_bias",
    )(jnp.broadcast_to(rel_table.T[:, :, None], (N_HEADS, N_BUCKETS, Q_BLOCK)),
      jnp.asarray(bucket))


def _norm_rope(xh, gain, cos, sin):
    ms = jnp.mean(xh * xh, axis=0, keepdims=True)
    y = xh * lax.rsqrt(ms + EPS) * gain
    q = HEAD_DIM // 4
    rot = jnp.concatenate([-y[q:2 * q], y[0:q], -y[3 * q:4 * q], y[2 * q:3 * q]], axis=0)
    return y * cos + rot * sin


def _inproj_kernel(x_ref, shift_ref, scale_ref, gpre_ref, w_ref, cos_ref, sin_ref,
                   qn_ref, kn_ref,
                   qa_ref, ka_ref, va_ref, ga_ref, qb_ref, kb_ref, vb_ref, gb_ref):
    x = x_ref[0]
    ms = jnp.mean(x * x, axis=-1, keepdims=True)
    h = x * lax.rsqrt(ms + EPS) * gpre_ref[...]
    h = h * (1.0 + scale_ref[0, 0]) + shift_ref[0, 0]
    hb = h.astype(_BF16)

    def proj(group):
        w = jnp.concatenate([w_ref[:, lo:lo + n] for lo, n in group], axis=1).astype(_BF16)
        return lax.dot_general(w, hb, _TNT, preferred_element_type=_F32)

    cos = cos_ref[...]
    sin = sin_ref[...]
    qn = qn_ref[...]
    kn = kn_ref[...]

    def store_tiled(ref, rows, val):
        tq = ref.shape[3]
        for j in range(ref.shape[1]):
            ref[0, j, rows, :] = val[:, j * tq:(j + 1) * tq]

    gates = _silu(proj(_GATE_GROUP))
    store_tiled(ga_ref, slice(0, WIDTH), gates[:WIDTH].astype(_BF16))
    store_tiled(gb_ref, slice(0, WIDTH), gates[WIDTH:].astype(_BF16))

    qk = proj(_QK_GROUP)
    for hd in range(N_HEADS):
        rows = slice(hd * HEAD_DIM, (hd + 1) * HEAD_DIM)
        store_tiled(qa_ref, rows, (_norm_rope(qk[rows], qn, cos, sin) * Q_SCALE).astype(_BF16))
    store_tiled(qb_ref, slice(0, WIDTH), (qk[WIDTH:2 * WIDTH] * Q_SCALE).astype(_BF16))
    ka = qk[2 * WIDTH:2 * WIDTH + KV_WIDTH]
    ka = jnp.concatenate(
        [_norm_rope(ka[hd * HEAD_DIM:(hd + 1) * HEAD_DIM], kn, cos, sin) for hd in range(N_KV)],
        axis=0)
    ka_ref[0] = ka.T.astype(_BF16)
    kb_ref[0] = qk[2 * WIDTH + KV_WIDTH:].T.astype(_BF16)

    v = proj(_V_GROUP)
    va_ref[0] = v[:KV_WIDTH].astype(_BF16)
    store_tiled(vb_ref, slice(0, KV_WIDTH), v[KV_WIDTH:].astype(_BF16))


def _rope_tables(seq):
    f32 = np.float32
    rows = seq // GRID_W
    row = np.repeat(np.arange(rows), GRID_W).astype(f32)
    col = np.tile(np.arange(GRID_W), rows).astype(f32)
    half = HEAD_DIM // 2
    freqs = np.power(f32(ROPE_THETA), -np.arange(0, half, 2, dtype=f32) / f32(half)).astype(f32)

    def ang2(pos):
        ang = pos[:, None] * freqs[None, :]
        return np.concatenate([ang, ang], axis=-1)

    ang = np.concatenate([ang2(row), ang2(col)], axis=-1)
    return jnp.asarray(np.cos(ang).T), jnp.asarray(np.sin(ang).T)


def _in_proj(x, mod, g_pre, w_in, cos_t, sin_t, qn, kn):
    bsz, seq, d = x.shape
    tm = _TM_IN
    assert seq % tm == 0 and tm % _TQ_A == 0 and w_in.shape == (d, IN_COLS)
    tok_t = lambda n: pl.BlockSpec((1, n, tm), lambda b, i: (b, 0, i))
    tok_m = lambda n: pl.BlockSpec((1, tm, n), lambda b, i: (b, i, 0))
    mod_row = lambda r: pl.BlockSpec((1, 1, 1, d), lambda b, i: (b, r, 0, 0))
    tab = pl.BlockSpec((HEAD_DIM, tm), lambda b, i: (0, i))
    gain = pl.BlockSpec((HEAD_DIM, tm), lambda b, i: (0, 0))
    shp_t = lambda n: jax.ShapeDtypeStruct((bsz, n, seq), _BF16)
    shp_m = lambda n: jax.ShapeDtypeStruct((bsz, seq, n), _BF16)
    tiled = pl.BlockSpec((1, tm // _TQ_A, WIDTH, _TQ_A), lambda b, i: (b, i, 0, 0))
    shp_tiled = jax.ShapeDtypeStruct((bsz, seq // _TQ_A, WIDTH, _TQ_A), _BF16)
    v_blocks = pl.BlockSpec((1, tm // Q_BLOCK, KV_WIDTH, Q_BLOCK), lambda b, i: (b, i, 0, 0))
    shp_v_blocks = jax.ShapeDtypeStruct((bsz, seq // Q_BLOCK, KV_WIDTH, Q_BLOCK), _BF16)
    return pl.pallas_call(
        _inproj_kernel,
        out_shape=(shp_tiled, shp_m(KV_WIDTH), shp_t(KV_WIDTH), shp_tiled,
                   shp_tiled, shp_m(KV_WIDTH), shp_v_blocks, shp_tiled),
        grid=(bsz, seq // tm),
        in_specs=[pl.BlockSpec((1, tm, d), lambda b, i: (b, i, 0)),
                  mod_row(_SHIFT), mod_row(_SCALE),
                  pl.BlockSpec((1, d), lambda b, i: (0, 0)),
                  pl.BlockSpec((d, IN_COLS), lambda b, i: (0, 0)),
                  tab, tab, gain, gain],
        out_specs=(tiled, tok_m(KV_WIDTH), tok_t(KV_WIDTH), tiled,
                   tiled, tok_m(KV_WIDTH), v_blocks, tiled),
        compiler_params=_params(2),
        name="in_proj",
    )(x, mod, mod, g_pre.reshape(1, d), w_in, cos_t, sin_t,
      jnp.broadcast_to(qn[:, None], (HEAD_DIM, tm)),
      jnp.broadcast_to(kn[:, None], (HEAD_DIM, tm)))


_PAIRS = N_HEADS // 2
_ONES_ROWS = 16
_LOOKAHEAD = 2
_TILES_PER_TRIP = 2


def _head_rows(hd):
    return slice(hd * HEAD_DIM, (hd + 1) * HEAD_DIM)


def _pair_q(head, pair):
    def padded(hd):
        qh = head(hd)
        z = jnp.zeros_like(qh)
        return jnp.concatenate([qh, z] if hd // GROUP == 0 else [z, qh], axis=0)
    return jnp.concatenate([padded(2 * pair), padded(2 * pair + 1)], axis=1)


def _pair_out(o, inv_l, gate, pair, tq):
    o = o * inv_l
    g0 = gate(2 * pair).astype(_F32)
    g1 = gate(2 * pair + 1).astype(_F32)
    return jnp.concatenate([o[:, :tq] * g0, o[:, tq:] * g1], axis=0).T.astype(_BF16)


def _swap_middle(a):
    b = Q_BLOCK
    return jnp.concatenate([a[:, :b], a[:, 2 * b:3 * b], a[:, b:2 * b], a[:, 3 * b:]], axis=1)


def _v_with_ones(v):
    return jnp.concatenate([v, jnp.ones((_ONES_ROWS, v.shape[1]), v.dtype)], axis=0)


def _mixers_kernel(layer, qa_ref, ka_ref, va_ref, ga_ref, qb_ref, kb_ref, vb_ref, gb_ref,
                   qa_nx_ref, ka_nx_ref, qb_nx_ref, kb_nx_ref, sink_ref, bias_ref,
                   o_ref, sa0_ref, sa1_ref, sb0_ref, sb1_ref, m_ref):
    n_tiles, tq = qa_ref.shape[1], qa_ref.shape[3]
    seq = ka_ref.shape[1]
    n_blocks = seq // Q_BLOCK
    a_chunks = [slice(c * _TK_A, (c + 1) * _TK_A) for c in range(seq // _TK_A)]
    items = [(kind, p) for p in range(_PAIRS) for kind in ("a", "b")]
    bufs = (sa0_ref, sb0_ref, sa1_ref, sb1_ref) * (_PAIRS // 2)

    def staged(start, size, zero):
        return pl.ds(pl.multiple_of(zero + start, Q_BLOCK), size)

    def sink_row(pair):
        first_head = lax.broadcasted_iota(jnp.int32, (1, 2 * tq), 1) // Q_BLOCK % 2 == 0
        return jnp.where(first_head, sink_ref[layer, 2 * pair], sink_ref[layer, 2 * pair + 1]) * LOG2E

    def window_blocks(tile):
        first = (tq // Q_BLOCK) * tile - 1
        return [jnp.clip(first + j, 0, n_blocks - 1) for j in range(_KEYS_B // Q_BLOCK)]

    def score_chunks(kind, pair, buf, tile, zero, next_batch=False):
        if next_batch:
            q_ref, k_ref = (qa_nx_ref, ka_nx_ref) if kind == "a" else (qb_nx_ref, kb_nx_ref)
        else:
            q_ref, k_ref = (qa_ref, ka_ref) if kind == "a" else (qb_ref, kb_ref)

        def weights():
            return _pair_q(lambda hd: q_ref[0, tile, _head_rows(hd), :], pair)

        def global_chunk(rows):
            s = jnp.dot(k_ref[0, rows, :], weights(), preferred_element_type=_F32)
            buf[staged(rows.start, _TK_A, zero), :] = s
            return jnp.max(s, axis=0, keepdims=True)

        def window():
            k = jnp.concatenate(
                [k_ref[0, pl.ds(pl.multiple_of(blk * Q_BLOCK, Q_BLOCK), Q_BLOCK), :]
                 for blk in window_blocks(tile)], axis=0)
            w = _swap_middle(weights())

            def bias(keys, queries):
                return jnp.concatenate([bias_ref[2 * pair, keys, queries],
                                        bias_ref[2 * pair + 1, keys, queries]], axis=1)

            left = (jnp.dot(k[_KEYS_Q0], w[:, :tq], preferred_element_type=_F32)
                    + bias(_KEYS_Q0, slice(0, Q_BLOCK)))
            right = (jnp.dot(k[_KEYS_Q1], w[:, tq:], preferred_element_type=_F32)
                     + bias(_KEYS_Q1, slice(Q_BLOCK, 2 * Q_BLOCK)))
            left = jnp.concatenate(
                [jnp.where(tile == 0, MASK_VALUE, left[:Q_BLOCK]), left[Q_BLOCK:]], axis=0)
            right = jnp.concatenate(
                [right[:-Q_BLOCK], jnp.where(tile == n_tiles - 1, MASK_VALUE, right[-Q_BLOCK:])],
                axis=0)
            buf[staged(_KEYS_Q0.start, _KEYS_Q, zero), :tq] = left
            buf[staged(_KEYS_Q1.start, _KEYS_Q, zero), tq:] = right
            m = jnp.concatenate([jnp.max(left, axis=0, keepdims=True),
                                 jnp.max(right, axis=0, keepdims=True)], axis=1)
            return jnp.maximum(m, sink_row(pair))

        if kind == "a":
            return [functools.partial(global_chunk, rows) for rows in a_chunks]
        return [window]

    def pv_chunks(kind, pair, buf, tile, zero, m):
        kv = _head_rows(pair // (GROUP // 2))

        def global_chunk(rows):
            e = jnp.exp2(buf[staged(rows.start, _TK_A, zero), :] - m)
            return jnp.dot(_v_with_ones(va_ref[0, kv, rows]), e.astype(_BF16),
                           preferred_element_type=_F32)

        def window():
            e_left = jnp.exp2(buf[staged(_KEYS_Q0.start, _KEYS_Q, zero), :tq] - m[:, :tq])
            e_right = jnp.exp2(buf[staged(_KEYS_Q1.start, _KEYS_Q, zero), tq:] - m[:, tq:])
            v = jnp.concatenate([vb_ref[0, blk] for blk in window_blocks(tile)], axis=1)
            v = _v_with_ones(v[kv, :])
            return jnp.concatenate(
                [jnp.dot(v[:, _KEYS_Q0], e_left.astype(_BF16), preferred_element_type=_F32),
                 jnp.dot(v[:, _KEYS_Q1], e_right.astype(_BF16), preferred_element_type=_F32)],
                axis=1)

        if kind == "a":
            return [functools.partial(global_chunk, rows) for rows in a_chunks]
        return [window]

    def finish(kind, pair, tile, o, m):
        l = o[HEAD_DIM:HEAD_DIM + 1]
        if kind == "b":
            l = _swap_middle(l + jnp.exp2(sink_row(pair) - m))
            o = _swap_middle(o)
        g_ref = ga_ref if kind == "a" else gb_ref
        lane0 = (0 if kind == "a" else WIDTH) + pair * 2 * HEAD_DIM
        o_ref[0, pl.ds(pl.multiple_of(tile * tq, tq), tq), lane0:lane0 + 2 * HEAD_DIM] = _pair_out(
            o[:HEAD_DIM], 1.0 / l, lambda hd: g_ref[0, tile, _head_rows(hd), :], pair, tq)

    def col_max(parts):
        return functools.reduce(jnp.maximum, parts)

    def tile_body(tile, carry, last_tile=False):
        zero = jnp.minimum(tile, 0)
        col_maxes = list(carry)
        for idx, (kind, pair) in enumerate(items):
            ahead = (idx + _LOOKAHEAD) % len(items)
            if idx + _LOOKAHEAD < len(items):
                nxt = score_chunks(*items[ahead], bufs[ahead], tile, zero)
            elif last_tile:
                nxt = score_chunks(*items[ahead], bufs[ahead], 0, zero, next_batch=True)
            else:
                nxt = score_chunks(*items[ahead], bufs[ahead], tile + 1, zero)
            m = col_maxes[idx]
            cur = pv_chunks(kind, pair, bufs[idx], tile, zero, m)
            parts, o = [], None
            for i in range(max(len(nxt), len(cur))):
                if i < len(nxt):
                    parts.append(nxt[i]())
                if i < len(cur):
                    pv = cur[i]()
                    o = pv if o is None else o + pv
            finish(kind, pair, tile, o, m)
            col_maxes.append(col_max(parts))
        return tuple(col_maxes[len(items):])

    first = jnp.minimum(pl.program_id(0), 0)

    @pl.when(pl.program_id(0) == 0)
    def _():
        for i in range(_LOOKAHEAD):
            m_ref[i] = col_max(
                [chunk() for chunk in score_chunks(*items[i], bufs[i], first, first)])

    def trip(i, carry, last_trip=False):
        for j in range(_TILES_PER_TRIP):
            carry = tile_body(_TILES_PER_TRIP * i + j, carry,
                              last_tile=last_trip and j == _TILES_PER_TRIP - 1)
        return carry

    n_trips = n_tiles // _TILES_PER_TRIP
    carry = lax.fori_loop(0, n_trips - 1, trip, tuple(m_ref[i] for i in range(_LOOKAHEAD)))
    carry = trip(first + (n_trips - 1), carry, last_trip=True)
    for i in range(_LOOKAHEAD):
        m_ref[i] = carry[i]


def _mixers(qa, ka, va, ga, qb, kb, vb, gb, sink_all, layer, bias):
    bsz, n_tiles, _, tq = qa.shape
    seq = ka.shape[1]
    assert tq == 2 * Q_BLOCK == 2 * WINDOW and seq % _TK_A == 0
    assert n_tiles % _TILES_PER_TRIP == 0
    tiled = pl.BlockSpec((1, n_tiles, WIDTH, tq), lambda b: (b, 0, 0, 0))
    k_spec = pl.BlockSpec((1, seq, KV_WIDTH), lambda b: (b, 0, 0))
    const = lambda shape: pl.BlockSpec(shape, lambda b: (0,) * len(shape))
    nxt = lambda b: jnp.minimum(b + 1, bsz - 1)
    q_next = pl.BlockSpec((1, 1, WIDTH, tq), lambda b: (nxt(b), 0, 0, 0))
    return pl.pallas_call(
        functools.partial(_mixers_kernel, layer),
        out_shape=jax.ShapeDtypeStruct((bsz, seq, 2 * WIDTH), _BF16),
        grid=(bsz,),
        in_specs=[tiled, k_spec, pl.BlockSpec((1, KV_WIDTH, seq), lambda b: (b, 0, 0)), tiled,
                  tiled, k_spec,
                  pl.BlockSpec((1, seq // Q_BLOCK, KV_WIDTH, Q_BLOCK), lambda b: (b, 0, 0, 0)),
                  tiled,
                  q_next, pl.BlockSpec((1, seq, KV_WIDTH), lambda b: (nxt(b), 0, 0)),
                  q_next, pl.BlockSpec((1, _KEYS_B, KV_WIDTH), lambda b: (nxt(b), 0, 0)),
                  pl.BlockSpec(memory_space=pltpu.SMEM), pl.BlockSpec(memory_space=pltpu.VMEM)],
        out_specs=pl.BlockSpec((1, seq, 2 * WIDTH), lambda b: (b, 0, 0)),
        scratch_shapes=([pltpu.VMEM((seq, 2 * tq), _F32)] * 2
                        + [pltpu.VMEM((_KEYS_B, 2 * tq), _F32)] * 2
                        + [pltpu.VMEM((_LOOKAHEAD, 1, 2 * tq), _F32)]),
        compiler_params=_params(1),
        name="mixers",
    )(qa, ka, va, ga, qb, kb, vb, gb, qa, ka, qb, kb, sink_all, bias)


def _outproj_kernel(a_ref, w_ref, x_ref, gate_ref, gpost_ref, o_ref):
    w = w_ref[...].astype(_BF16)
    starts = np.cumsum((0,) + _TM_OUT_CHUNKS)
    assert starts[-1] == a_ref.shape[1]
    for lo, hi in zip(starts[:-1], starts[1:]):
        rows = slice(int(lo), int(hi))
        y = jnp.dot(a_ref[0, rows, :], w, preferred_element_type=_F32)
        ms = jnp.mean(y * y, axis=-1, keepdims=True)
        yn = y * lax.rsqrt(ms + EPS) * gpost_ref[...]
        o_ref[0, rows, :] = x_ref[0, rows, :] + gate_ref[0, 0] * yn


def _out_proj(o, w_out, x, mod, g_post):
    bsz, seq, d = x.shape
    tm = _TM_OUT
    assert seq % tm == 0 and w_out.shape == (2 * WIDTH, d)
    xo = pl.BlockSpec((1, tm, d), lambda b, i: (b, i, 0))
    windows = 2 * (2 * tm * d * 4 + tm * 2 * WIDTH * 2) + 2 * WIDTH * d * 4
    vmem_limit = windows + 8 * max(_TM_OUT_CHUNKS) * d * 4
    assert vmem_limit <= _VMEM_BYTES_V7X - (4 << 20)
    return pl.pallas_call(
        _outproj_kernel,
        out_shape=jax.ShapeDtypeStruct((bsz, seq, d), _F32),
        grid=(bsz, seq // tm),
        in_specs=[pl.BlockSpec((1, tm, 2 * WIDTH), lambda b, i: (b, i, 0)),
                  pl.BlockSpec((2 * WIDTH, d), lambda b, i: (0, 0)),
                  xo,
                  pl.BlockSpec((1, 1, 1, d), lambda b, i: (b, _GATE, 0, 0)),
                  pl.BlockSpec((1, d), lambda b, i: (0, 0))],
        out_specs=xo,
        compiler_params=_params(2, vmem_limit),
        name="out_proj",
    )(o, w_out, x, mod, g_post.reshape(1, d))


def kernel(x, c, w_ada, b_ada, g_pre, g_post, w_in, qn_a, kn_a, sink_b, w_out, rel_table):
    bsz, seq, d = x.shape
    depth = w_ada.shape[0]
    cos_t, sin_t = _rope_tables(seq)
    bias = _t5_bias_tiles(rel_table)
    for l in range(depth):
        mod = _adaln_mod(c, w_ada[l], b_ada[l]).reshape(bsz, 3, 1, d)
        qa, ka, va, ga, qb, kb, vb, gb = _in_proj(
            x, mod, g_pre[l], w_in[l], cos_t, sin_t, qn_a[l], kn_a[l])
        o = _mixers(qa, ka, va, ga, qb, kb, vb, gb, sink_b, l, bias)
        x = _out_proj(o, w_out[l], x, mod, g_post[l])
    return x
```
